```python
import jax, jax.numpy as jnp
from jax import lax
import numpy as np

D_MODEL = 1024
BATCH = 2
SEQ = 8192
DEPTH = 2
DEC_BATCH = 16
DEC_SEQ = 32
PAST_LEN = 2048

CHUNK = 64
NORM_EPS = 1e-6

SSD_EXPAND = 2
SSD_D_INNER = SSD_EXPAND * D_MODEL
SSD_HEAD_DIM = 64
SSD_N_HEADS = SSD_D_INNER // SSD_HEAD_DIM
SSD_N_GROUPS = 4
SSD_HEADS_PER_GROUP = SSD_N_HEADS // SSD_N_GROUPS
SSD_D_STATE = 128
SSD_CONV_W = 4
SSD_CONV_DIM = SSD_D_INNER + 2 * SSD_N_GROUPS * SSD_D_STATE
SSD_IN_DIM = SSD_D_INNER + SSD_CONV_DIM + SSD_N_HEADS
SSD_NORM_GROUP = SSD_D_INNER // SSD_N_GROUPS
SSD_DT_MIN = 1e-3
SSD_DT_MAX = 1e-1

HGRN_DIM = D_MODEL
HGRN_HEAD_DIM = 128
HGRN_N_HEADS = HGRN_DIM // HGRN_HEAD_DIM

FFN_HIDDEN = -(-(8 * D_MODEL) // (3 * 256)) * 256

N_SSD_LAYERS = (DEPTH + 1) // 2
N_HGRN_LAYERS = DEPTH // 2

kernel_name = "hybrid_ssd_hgrn2_streaming_step"


def rmsnorm(x, w):
    xf = x.astype(jnp.float32)
    y = xf * lax.rsqrt(jnp.mean(xf * xf, axis=-1, keepdims=True) + NORM_EPS)
    return (y * w.astype(jnp.float32)).astype(x.dtype)


def _chunkify(a, n_chunks):
    pad = n_chunks * CHUNK - a.shape[1]
    a = jnp.pad(a, [(0, 0), (0, pad)] + [(0, 0)] * (a.ndim - 2))
    a = a.reshape((a.shape[0], n_chunks, CHUNK) + a.shape[2:])
    return jnp.moveaxis(a, 1, 0)


def _unchunkify(a, length):
    a = jnp.moveaxis(a, 0, 1)
    a = a.reshape((a.shape[0], -1) + a.shape[3:])
    return a[:, :length]


def ssd_scan(x, dt, A, Bm, Cm, S0):
    Bsz, L = x.shape[0], x.shape[1]
    G, Hg, P, N = SSD_N_GROUPS, SSD_HEADS_PER_GROUP, SSD_HEAD_DIM, SSD_D_STATE
    n = -(-L // CHUNK)
    xs = _chunkify(x.reshape(Bsz, L, G, Hg, P), n)
    dts = _chunkify(dt.reshape(Bsz, L, G, Hg), n)
    Bs = _chunkify(Bm, n)
    Cs = _chunkify(Cm, n)
    Ag = A.reshape(G, Hg)
    mask = jnp.tril(jnp.ones((CHUNK, CHUNK), dtype=bool))[None, :, :, None, None]

    def step(S, inp):
        xc, dtc, Bc, Cc = inp
        cum = jnp.cumsum(dtc * Ag, axis=1)
        seg = cum[:, :, None] - cum[:, None, :]
        decay = jnp.where(mask, jnp.exp(jnp.where(mask, seg, 0.0)), 0.0)
        xdt = xc * dtc[..., None]
        cb = jnp.einsum('btgn,bsgn->btsg', Cc, Bc)
        y = jnp.einsum('btsg,btsgh,bsghp->btghp', cb, decay, xdt)
        y = y + jnp.einsum('btgn,bghpn->btghp', Cc, S) * jnp.exp(cum)[..., None]
        last = cum[:, -1]
        w_end = jnp.exp(last[:, None] - cum)
        S = jnp.exp(last)[..., None, None] * S + jnp.einsum('bsgh,bsghp,bsgn->bghpn', w_end, xdt, Bc)
        return S, y

    S, ys = lax.scan(step, S0.reshape(Bsz, G, Hg, P, N), (xs, dts, Bs, Cs))
    y = _unchunkify(ys, L).reshape(Bsz, L, G * Hg, P)
    return y, S.reshape(Bsz, G * Hg, P, N)


def gla_scan(q, k, v, logg, S0):
    L = q.shape[1]
    n = -(-L // CHUNK)
    qs, ks, vs, gs = (_chunkify(a, n) for a in (q, k, v, logg))
    mask = jnp.tril(jnp.ones((CHUNK, CHUNK), dtype=bool))[None, None]

    def step(S, inp):
        qc, kc, vc, gc = inp
        b = jnp.cumsum(gc, axis=1)
        qe = qc * jnp.exp(b)
        ke = kc * jnp.exp(-b)
        sc = jnp.where(mask, jnp.einsum('bthk,bshk->bhts', qe, ke), 0.0)
        o = jnp.einsum('bhts,bshv->bthv', sc, vc) + jnp.einsum('bthk,bhkv->bthv', qe, S)
        last = b[:, -1]
        S = jnp.exp(last)[..., None] * S + jnp.einsum('bshk,bshv->bhkv', kc * jnp.exp(last[:, None] - b), vc)
        return S, o

    S, os_ = lax.scan(step, S0, (qs, ks, vs, gs))
    return _unchunkify(os_, L), S


def ssd_mixer(h, conv_buf, ssm_state, in_w, conv_w, conv_b, dt_bias, A_log, D_skip, gnorm_w, out_w):
    Bsz, L, _ = h.shape
    proj = h @ in_w
    z = proj[..., :SSD_D_INNER]
    xbc = proj[..., SSD_D_INNER:SSD_D_INNER + SSD_CONV_DIM]
    dt_raw = proj[..., SSD_D_INNER + SSD_CONV_DIM:]
    xpad = jnp.concatenate([conv_buf.astype(xbc.dtype), xbc], axis=1)
    new_conv = xpad[:, -(SSD_CONV_W - 1):]
    acc = conv_b.astype(jnp.float32)
    for tap in range(SSD_CONV_W):
        acc = acc + xpad[:, tap:tap + L].astype(jnp.float32) * conv_w[tap].astype(jnp.float32)
    xbc = jax.nn.silu(acc)
    xs = xbc[..., :SSD_D_INNER].reshape(Bsz, L, SSD_N_HEADS, SSD_HEAD_DIM)
    Bm = xbc[..., SSD_D_INNER:SSD_D_INNER + SSD_N_GROUPS * SSD_D_STATE].reshape(Bsz, L, SSD_N_GROUPS, SSD_D_STATE)
    Cm = xbc[..., SSD_D_INNER + SSD_N_GROUPS * SSD_D_STATE:].reshape(Bsz, L, SSD_N_GROUPS, SSD_D_STATE)
    dt = jax.nn.softplus(dt_raw.astype(jnp.float32) + dt_bias.astype(jnp.float32))
    A = -jnp.exp(A_log.astype(jnp.float32))
    y, S = ssd_scan(xs, dt, A, Bm, Cm, ssm_state.astype(jnp.float32))
    y = y + D_skip.astype(jnp.float32)[:, None] * xs
    yg = y.reshape(Bsz, L, SSD_D_INNER) * jax.nn.silu(z.astype(jnp.float32))
    yg = yg.reshape(Bsz, L, SSD_N_GROUPS, SSD_NORM_GROUP)
    yg = yg * lax.rsqrt(jnp.mean(yg * yg, axis=-1, keepdims=True) + NORM_EPS)
    yg = yg.reshape(Bsz, L, SSD_D_INNER) * gnorm_w.astype(jnp.float32)
    out = yg.astype(h.dtype) @ out_w
    return out, new_conv, S


def hgrn_mixer(h, state, lower_bound, in_w, gnorm_w, out_w):
    Bsz, L, _ = h.shape
    proj = h @ in_w
    q, f, i, g = jnp.split(proj.astype(jnp.float32), 4, axis=-1)
    heads = (Bsz, L, HGRN_N_HEADS, HGRN_HEAD_DIM)
    q = jax.nn.silu(q).reshape(heads)
    lb = lower_bound.astype(jnp.float32).reshape(HGRN_N_HEADS, HGRN_HEAD_DIM)
    forget = lb + (1.0 - lb) * jax.nn.sigmoid(f.reshape(heads))
    k = 1.0 - forget
    logg = jnp.log(forget)
    o, S = gla_scan(q, k, i.reshape(heads), logg, state.astype(jnp.float32))
    o = o * lax.rsqrt(jnp.mean(o * o, axis=-1, keepdims=True) + NORM_EPS) * gnorm_w.astype(jnp.float32)
    o = o.reshape(Bsz, L, HGRN_DIM) * jax.nn.silu(g)
    out = o.astype(h.dtype) @ out_w
    return out, S


def swiglu(h, w_gate, w_up, w_down):
    return (jax.nn.silu(h @ w_gate) * (h @ w_up)) @ w_down


def setup_inputs(seed: int = 0) -> dict:
    key = jax.random.key(seed)
    ks = jax.random.split(key, 24)
    f32 = jnp.float32

    def nrm(k, shape, scale):
        return jax.random.normal(k, shape, f32) * scale

    u = jax.random.uniform(ks[8], (N_SSD_LAYERS, SSD_N_HEADS), f32)
    dt0 = jnp.exp(u * (np.log(SSD_DT_MAX) - np.log(SSD_DT_MIN)) + np.log(SSD_DT_MIN))
    dt_bias = dt0 + jnp.log(-jnp.expm1(-dt0))
    A_log = jnp.log(jax.random.uniform(ks[9], (N_SSD_LAYERS, SSD_N_HEADS), f32, 1.0, 16.0))
    return {
        "x_prompt": nrm(ks[0], (BATCH, SEQ, D_MODEL), 1.0),
        "x_sample": nrm(ks[1], (DEC_BATCH, DEC_SEQ, D_MODEL), 1.0),
        "state_ssd": nrm(ks[2], (N_SSD_LAYERS, DEC_BATCH, SSD_N_HEADS, SSD_HEAD_DIM, SSD_D_STATE), 0.5),
        "cache_conv": nrm(ks[3], (N_SSD_LAYERS, DEC_BATCH, SSD_CONV_W - 1, SSD_CONV_DIM), 1.0),
        "state_hgrn": nrm(ks[4], (N_HGRN_LAYERS, DEC_BATCH, HGRN_N_HEADS, HGRN_HEAD_DIM, HGRN_HEAD_DIM), 0.5),
        "ssd_norm_w": 1.0 + nrm(ks[5], (N_SSD_LAYERS, D_MODEL), 0.02),
        "ssd_in_w": nrm(ks[6], (N_SSD_LAYERS, D_MODEL, SSD_IN_DIM), D_MODEL ** -0.5),
        "ssd_conv_w": nrm(ks[7], (N_SSD_LAYERS, SSD_CONV_W, SSD_CONV_DIM), SSD_CONV_W ** -0.5),
        "ssd_conv_b": nrm(ks[10], (N_SSD_LAYERS, SSD_CONV_DIM), 0.02),
        "ssd_dt_bias": dt_bias,
        "ssd_A_log": A_log,
        "ssd_D": 1.0 + nrm(ks[11], (N_SSD_LAYERS, SSD_N_HEADS), 0.1),
        "ssd_gnorm_w": 1.0 + nrm(ks[12], (N_SSD_LAYERS, SSD_D_INNER), 0.02),
        "ssd_out_w": nrm(ks[13], (N_SSD_LAYERS, SSD_D_INNER, D_MODEL), SSD_D_INNER ** -0.5),
        "hgrn_norm_w": 1.0 + nrm(ks[14], (N_HGRN_LAYERS, D_MODEL), 0.02),
        "hgrn_in_w": nrm(ks[15], (N_HGRN_LAYERS, D_MODEL, 4 * HGRN_DIM), D_MODEL ** -0.5),
        "hgrn_lower_bounds": nrm(ks[16], (DEPTH, HGRN_DIM), 0.1),
        "hgrn_gnorm_w": 1.0 + nrm(ks[17], (N_HGRN_LAYERS, HGRN_HEAD_DIM), 0.02),
        "hgrn_out_w": nrm(ks[18], (N_HGRN_LAYERS, HGRN_DIM, D_MODEL), HGRN_DIM ** -0.5),
        "ffn_norm_w": 1.0 + nrm(ks[19], (DEPTH, D_MODEL), 0.02),
        "ffn_w_gate": nrm(ks[20], (DEPTH, D_MODEL, FFN_HIDDEN), D_MODEL ** -0.5),
        "ffn_w_up": nrm(ks[21], (DEPTH, D_MODEL, FFN_HIDDEN), D_MODEL ** -0.5),
        "ffn_w_down": nrm(ks[22], (DEPTH, FFN_HIDDEN, D_MODEL), FFN_HIDDEN ** -0.5),
        "final_norm_w": 1.0 + nrm(ks[23], (D_MODEL,), 0.02),
    }


def reference(x_prompt, x_sample, state_ssd, cache_conv, state_hgrn,
              ssd_norm_w, ssd_in_w, ssd_conv_w, ssd_conv_b, ssd_dt_bias, ssd_A_log, ssd_D,
              ssd_gnorm_w, ssd_out_w, hgrn_norm_w, hgrn_in_w, hgrn_lower_bounds, hgrn_gnorm_w,
              hgrn_out_w, ffn_norm_w, ffn_w_gate, ffn_w_up, ffn_w_down, final_norm_w):
    lb_soft = jax.nn.softmax(hgrn_lower_bounds.astype(jnp.float32), axis=0)
    lbs = jnp.cumsum(lb_soft, axis=0) - lb_soft[0]

    def trunk(x, ssd_states, conv_bufs, hgrn_states):
        new_ssd, new_conv, new_hgrn = [], [], []
        for layer in range(DEPTH):
            j = layer // 2
            h = rmsnorm(x, ssd_norm_w[j] if layer % 2 == 0 else hgrn_norm_w[j])
            if layer % 2 == 0:
                out, cb, ss = ssd_mixer(h, conv_bufs[j], ssd_states[j], ssd_in_w[j], ssd_conv_w[j],
                                        ssd_conv_b[j], ssd_dt_bias[j], ssd_A_log[j], ssd_D[j],
                                        ssd_gnorm_w[j], ssd_out_w[j])
                new_conv.append(cb)
                new_ssd.append(ss)
            else:
                out, hs = hgrn_mixer(h, hgrn_states[j], lbs[layer], hgrn_in_w[j], hgrn_gnorm_w[j], hgrn_out_w[j])
                new_hgrn.append(hs)
            x = x + out.astype(x.dtype)
            x = x + swiglu(rmsnorm(x, ffn_norm_w[layer]), ffn_w_gate[layer], ffn_w_up[layer], ffn_w_down[layer])
        y = rmsnorm(x, final_norm_w)
        return y, jnp.stack(new_ssd), jnp.stack(new_conv), jnp.stack(new_hgrn)

    bp = x_prompt.shape[0]
    zero_ssd = jnp.zeros((N_SSD_LAYERS, bp, SSD_N_HEADS, SSD_HEAD_DIM, SSD_D_STATE), jnp.float32)
    zero_conv = jnp.zeros((N_SSD_LAYERS, bp, SSD_CONV_W - 1, SSD_CONV_DIM), x_prompt.dtype)
    zero_hgrn = jnp.zeros((N_HGRN_LAYERS, bp, HGRN_N_HEADS, HGRN_HEAD_DIM, HGRN_HEAD_DIM), jnp.float32)
    y_prompt, ssd_p, conv_p, hgrn_p = trunk(x_prompt, zero_ssd, zero_conv, zero_hgrn)
    y_sample, ssd_s, conv_s, hgrn_s = trunk(x_sample, state_ssd, cache_conv, state_hgrn)
    return (y_prompt, y_sample, ssd_p, conv_p, hgrn_p, ssd_s, conv_s, hgrn_s)
```

```python
import functools

import jax
import jax.numpy as jnp
from jax import lax
from jax.experimental import pallas as pl
from jax.experimental.pallas import tpu as pltpu

F32 = jnp.float32
BF16 = jnp.bfloat16

D_MODEL = 1024
NORM_EPS = 1e-6

SSD_D_INNER = 2048
SSD_HEAD_DIM = 64
SSD_N_HEADS = 32
SSD_N_GROUPS = 4
SSD_HEADS_PER_GROUP = 8
SSD_D_STATE = 128
SSD_CONV_W = 4
SSD_CONV_DIM = 3072
SSD_NORM_GROUP = 512
SSD_N_PAIRS = SSD_N_HEADS // 2

HGRN_HEAD_DIM = 128
HGRN_N_HEADS = 8

FFN_HIDDEN = 2816

LANES = 128
CONV_PAD = 8
VMEM_LIMIT = 56 * 1024 * 1024


def _dot(a, b):
    return jnp.dot(a, b, preferred_element_type=F32)


def _dot_nt(a, b):
    return lax.dot_general(a, b, (((1,), (1,)), ((), ())), preferred_element_type=F32)


def _dot_tn(a, b):
    return lax.dot_general(a, b, (((0,), (0,)), ((), ())), preferred_element_type=F32)


def _split3(a):
    hi = a.astype(BF16)
    r1 = a - hi.astype(F32)
    mid = r1.astype(BF16)
    lo = (r1 - mid.astype(F32)).astype(BF16)
    return hi, mid, lo


def _rms(x, w):
    return x * lax.rsqrt(jnp.mean(x * x, axis=-1, keepdims=True) + NORM_EPS) * w


def _silu(x):
    return x * jax.nn.sigmoid(x)


def _ssd_kernel(*refs, has_init, NB, TL, C):
    it = iter(refs)
    x_ref = next(it)
    s0_ref = cv0_ref = None
    if has_init:
        s0_ref = next(it)
        cv0_ref = next(it)
    (nw_ref, wz_ref, wx_ref, wdt_ref, wdtT_ref, cw_ref, cb_ref, dtb_ref, dtbT_ref,
     alog_ref, alogp_ref, dexp_ref, gnw_ref, wo_ref) = [next(it) for _ in range(14)]
    xo_ref = next(it)
    so_ref = next(it)
    cvo_ref = next(it)
    st_scr, xpad_scr, z_scr, xbc_scr, y_scr, dt_scr, dtTp_scr = [next(it) for _ in range(7)]

    R = NB * TL
    n_ch = R // C
    ch_per_seq = TL // C
    W2 = 2 * C
    i = pl.program_id(1)
    n_i = pl.num_programs(1)

    @pl.when(i == 0)
    def _init():
        if has_init:
            for nb in range(NB):
                st_scr[nb] = s0_ref[nb].T
            xpad_scr[:, CONV_PAD - 3:CONV_PAD, :] = cv0_ref[...]
        else:
            st_scr[...] = jnp.zeros_like(st_scr)
            xpad_scr[:, CONV_PAD - 3:CONV_PAD, :] = jnp.zeros((NB, 3, SSD_CONV_DIM), F32)

    x = x_ref[...].reshape(R, D_MODEL)
    h16 = _rms(x, nw_ref[...]).astype(BF16)
    z_scr[...] = _dot(h16, wz_ref[...])
    xraw = _dot(h16, wx_ref[...])
    dt_scr[...] = jax.nn.softplus(_dot(h16, wdt_ref[...]) + dtb_ref[...])
    dtT = jax.nn.softplus(_dot_nt(wdtT_ref[...], h16) + dtbT_ref[...])
    for c in range(n_ch):
        dtTp_scr[c] = jnp.concatenate(
            [dtT[0:SSD_N_PAIRS, c * C:(c + 1) * C], dtT[SSD_N_PAIRS:, c * C:(c + 1) * C]], axis=1)

    xpad_scr[:, CONV_PAD:CONV_PAD + TL, :] = xraw.reshape(NB, TL, SSD_CONV_DIM)
    for nb in range(NB):
        acc = cb_ref[...]
        for tap in range(SSD_CONV_W):
            off = CONV_PAD - 3 + tap
            acc = acc + xpad_scr[nb, off:off + TL, :] * cw_ref[tap:tap + 1, :]
        xbc_scr[nb * TL:(nb + 1) * TL, :] = _silu(acc)
    hist = xpad_scr[:, CONV_PAD + TL - 3:CONV_PAD + TL, :]
    xpad_scr[:, CONV_PAD - 3:CONV_PAD, :] = hist

    a_row = -jnp.exp(alog_ref[...])
    a_pair = -jnp.exp(alogp_ref[...])
    rr = lax.broadcasted_iota(jnp.int32, (C, C), 0)
    cc = lax.broadcasted_iota(jnp.int32, (C, C), 1)
    tri16 = (cc <= rr).astype(BF16)
    r2 = lax.broadcasted_iota(jnp.int32, (W2, W2), 0)
    c2 = lax.broadcasted_iota(jnp.int32, (W2, W2), 1)
    triT2_16 = ((r2 <= c2) & ((r2 < C) == (c2 < C))).astype(BF16)
    tp = lax.broadcasted_iota(jnp.int32, (C, W2), 0)
    sp = lax.broadcasted_iota(jnp.int32, (C, W2), 1)
    tril_pair = jnp.where(sp < C, sp, sp - C) <= tp
    left_pair = sp < C
    left_e = lax.broadcasted_iota(jnp.int32, (C, LANES), 1) < SSD_HEAD_DIM

    def chunk_body(j, carry):
        r0 = pl.multiple_of(j * C, C)
        nb = 0 if NB == 1 else j // ch_per_seq
        rows = pl.ds(r0, C)
        dt_c = dt_scr[rows, :]
        a = dt_c * a_row
        cum = sum(_dot(tri16, p) for p in _split3(a))
        dtTp = dtTp_scr[j]
        cumTp = sum(_dot(p, triT2_16) for p in _split3(dtTp * a_pair))
        last = cum[C - 1:C, :]
        wdt = jnp.exp(last - cum) * dt_c

        for g in range(SSD_N_GROUPS):
            bo = SSD_D_INNER + g * SSD_D_STATE
            co = SSD_D_INNER + SSD_N_GROUPS * SSD_D_STATE + g * SSD_D_STATE
            Bg16 = xbc_scr[rows, bo:bo + SSD_D_STATE].astype(BF16)
            Cg16 = xbc_scr[rows, co:co + SSD_D_STATE].astype(BF16)
            cbp = _dot_nt(Cg16, jnp.concatenate([Bg16, Bg16], axis=0))
            so = g * SSD_NORM_GROUP
            st_g = st_scr[nb, :, so:so + SSD_NORM_GROUP]
            y_inter = _dot(Cg16, st_g.astype(BF16))
            xw_parts = []
            el_parts = []
            for q in range(4):
                jp = g * 4 + q
                h0 = 2 * jp
                b0 = jnp.broadcast_to(cum[:, h0:h0 + 1], (C, LANES))
                b1 = jnp.broadcast_to(cum[:, h0 + 1:h0 + 2], (C, LANES))
                colc_e = jnp.where(left_e, b0, b1)
                if W2 == LANES:
                    colc_l = colc_e
                else:
                    colc_l = jnp.where(left_pair, b0[:, :W2], b1[:, :W2])
                seg = colc_l - cumTp[jp:jp + 1, :]
                lm = jnp.where(tril_pair, jnp.exp(jnp.where(tril_pair, seg, 0.0)), 0.0)
                mp = cbp * lm * dtTp[jp:jp + 1, :]
                xs_p = xbc_scr[rows, jp * LANES:(jp + 1) * LANES]
                rhs = jnp.concatenate(
                    [jnp.where(left_e, xs_p, 0.0).astype(BF16),
                     jnp.where(left_e, 0.0, xs_p).astype(BF16)], axis=0)
                e_c = jnp.exp(colc_e)
                y_p = _dot(mp.astype(BF16), rhs) + e_c * y_inter[:, q * LANES:(q + 1) * LANES]
                y_scr[rows, jp * LANES:(jp + 1) * LANES] = y_p
                w0 = jnp.broadcast_to(wdt[:, h0:h0 + 1], (C, LANES))
                w1 = jnp.broadcast_to(wdt[:, h0 + 1:h0 + 2], (C, LANES))
                xw_parts.append((xs_p * jnp.where(left_e, w0, w1)).astype(BF16))
                el_parts.append(e_c[C - 1:C, :])
            xw_g = jnp.concatenate(xw_parts, axis=1)
            el_g = jnp.concatenate(el_parts, axis=1)
            st_scr[nb, :, so:so + SSD_NORM_GROUP] = el_g * st_g + _dot_tn(Bg16, xw_g)
        return carry

    lax.fori_loop(0, n_ch, chunk_body, 0)

    y = y_scr[...] + dexp_ref[...] * xbc_scr[:, :SSD_D_INNER]
    yg = y * _silu(z_scr[...])
    parts = []
    for g in range(SSD_N_GROUPS):
        blk = yg[:, g * SSD_NORM_GROUP:(g + 1) * SSD_NORM_GROUP]
        parts.append(blk * lax.rsqrt(jnp.mean(blk * blk, axis=-1, keepdims=True) + NORM_EPS))
    yn = jnp.concatenate(parts, axis=1) * gnw_ref[...]
    out = _dot(yn.astype(BF16), wo_ref[...])
    xo_ref[...] = (x + out).reshape(NB, TL, D_MODEL)

    @pl.when(i == n_i - 1)
    def _fin():
        for nb in range(NB):
            so_ref[nb] = st_scr[nb].T
        cvo_ref[...] = xpad_scr[:, CONV_PAD - 3:CONV_PAD, :]


def _const_spec(shape):
    nd = len(shape)
    return pl.BlockSpec(shape, lambda *_: (0,) * nd, pipeline_mode=pl.Buffered(1))


def _ssd_mixer(x, s0, cv0, p, *, NB, TL, C):
    n_seq, L, _ = x.shape
    has_init = s0 is not None
    R = NB * TL
    grid = (n_seq // NB, L // TL)
    in_specs = [pl.BlockSpec((NB, TL, D_MODEL), lambda b, i: (b, i, 0))]
    args = [x]
    if has_init:
        in_specs += [pl.BlockSpec((NB, SSD_D_INNER, SSD_D_STATE), lambda b, i: (b, 0, 0)),
                     pl.BlockSpec((NB, 3, SSD_CONV_DIM), lambda b, i: (b, 0, 0))]
        args += [s0, cv0]
    alog_pair = jnp.concatenate(
        [jnp.broadcast_to(p["alog"][0::2, None], (SSD_N_PAIRS, C)),
         jnp.broadcast_to(p["alog"][1::2, None], (SSD_N_PAIRS, C))], axis=1)
    consts = [p["nw"], p["wz"], p["wx"], p["wdt"], p["wdtT"], p["cw"], p["cb"], p["dtb"], p["dtbT"],
              p["alog"].reshape(1, SSD_N_HEADS), alog_pair, p["dexp"], p["gnw"], p["wo"]]
    in_specs += [_const_spec(c.shape) for c in consts]
    args += consts
    out_shape = (jax.ShapeDtypeStruct((n_seq, L, D_MODEL), F32),
                 jax.ShapeDtypeStruct((n_seq, SSD_D_INNER, SSD_D_STATE), F32),
                 jax.ShapeDtypeStruct((n_seq, 3, SSD_CONV_DIM), F32))
    out_specs = (pl.BlockSpec((NB, TL, D_MODEL), lambda b, i: (b, i, 0)),
                 pl.BlockSpec((NB, SSD_D_INNER, SSD_D_STATE), lambda b, i: (b, 0, 0)),
                 pl.BlockSpec((NB, 3, SSD_CONV_DIM), lambda b, i: (b, 0, 0)))
    scratch = [pltpu.VMEM((NB, SSD_D_STATE, SSD_D_INNER), F32),
               pltpu.VMEM((NB, CONV_PAD + TL, SSD_CONV_DIM), F32),
               pltpu.VMEM((R, SSD_D_INNER), F32),
               pltpu.VMEM((R, SSD_CONV_DIM), F32),
               pltpu.VMEM((R, SSD_D_INNER), F32),
               pltpu.VMEM((R, SSD_N_HEADS), F32),
               pltpu.VMEM((R // C, SSD_N_PAIRS, 2 * C), F32)]
    return pl.pallas_call(
        functools.partial(_ssd_kernel, has_init=has_init, NB=NB, TL=TL, C=C),
        grid=grid, in_specs=in_specs, out_specs=out_specs, out_shape=out_shape,
        scratch_shapes=scratch,
        compiler_params=pltpu.CompilerParams(
            dimension_semantics=("arbitrary", "arbitrary"), vmem_limit_bytes=VMEM_LIMIT),
        name="ssd_mixer_init" if has_init else "ssd_mixer_zero",
    )(*args)


def _hgrn_kernel(*refs, has_init, NB, TL, C):
    it = iter(refs)
    x_ref = next(it)
    s0_ref = next(it) if has_init else None
    nw_ref, win_ref, lb_ref, gnw_ref, wo_ref = [next(it) for _ in range(5)]
    xo_ref = next(it)
    so_ref = next(it)
    st_scr, q_scr, k_scr, v_scr, lg_scr, g_scr, o_scr = [next(it) for _ in range(7)]

    R = NB * TL
    n_ch = R // C
    ch_per_seq = TL // C
    HD = HGRN_HEAD_DIM
    i = pl.program_id(1)
    n_i = pl.num_programs(1)

    @pl.when(i == 0)
    def _init():
        if has_init:
            for nb in range(NB):
                for hh in range(HGRN_N_HEADS):
                    st_scr[nb, hh] = s0_ref[nb, hh].T
        else:
            st_scr[...] = jnp.zeros_like(st_scr)

    x = x_ref[...].reshape(R, D_MODEL)
    h16 = _rms(x, nw_ref[...]).astype(BF16)
    proj = _dot(h16, win_ref[...])
    lb_soft = jax.nn.softmax(lb_ref[...], axis=0)
    lb = (lb_soft[0:1, :] + lb_soft[1:2, :]) - lb_soft[0:1, :]
    q_scr[...] = _silu(proj[:, 0:D_MODEL])
    forget = lb + (1.0 - lb) * jax.nn.sigmoid(proj[:, D_MODEL:2 * D_MODEL])
    k_scr[...] = 1.0 - forget
    lg_scr[...] = jnp.log(forget)
    v_scr[...] = proj[:, 2 * D_MODEL:3 * D_MODEL]
    g_scr[...] = proj[:, 3 * D_MODEL:]

    rr = lax.broadcasted_iota(jnp.int32, (C, C), 0)
    cc = lax.broadcasted_iota(jnp.int32, (C, C), 1)
    tril = cc <= rr
    tri16 = tril.astype(BF16)

    def chunk_body(j, carry):
        r0 = pl.multiple_of(j * C, C)
        nb = 0 if NB == 1 else j // ch_per_seq
        rows = pl.ds(r0, C)
        b = sum(_dot(tri16, p) for p in _split3(lg_scr[rows, :]))
        kc = k_scr[rows, :]
        qe16 = (q_scr[rows, :] * jnp.exp(b)).astype(BF16)
        ke16 = (kc * jnp.exp(-b)).astype(BF16)
        last = b[C - 1:C, :]
        kw16 = (kc * jnp.exp(last - b)).astype(BF16)
        elast = jnp.exp(last)
        v16 = v_scr[rows, :].astype(BF16)
        for hh in range(HGRN_N_HEADS):
            sl = slice(hh * HD, (hh + 1) * HD)
            sc = jnp.where(tril, _dot_nt(qe16[:, sl], ke16[:, sl]), 0.0)
            s_h = st_scr[nb, hh]
            o_h = _dot(sc.astype(BF16), v16[:, sl]) + _dot_nt(qe16[:, sl], s_h.astype(BF16))
            o_scr[rows, sl] = o_h
            st_scr[nb, hh] = elast[:, sl] * s_h + _dot_tn(v16[:, sl], kw16[:, sl])
        return carry

    lax.fori_loop(0, n_ch, chunk_body, 0)

    o = o_scr[...]
    parts = []
    for hh in range(HGRN_N_HEADS):
        blk = o[:, hh * HD:(hh + 1) * HD]
        parts.append(blk * lax.rsqrt(jnp.mean(blk * blk, axis=-1, keepdims=True) + NORM_EPS))
    on = jnp.concatenate(parts, axis=1) * gnw_ref[...] * _silu(g_scr[...])
    out = _dot(on.astype(BF16), wo_ref[...])
    xo_ref[...] = (x + out).reshape(NB, TL, D_MODEL)

    @pl.when(i == n_i - 1)
    def _fin():
        for nb in range(NB):
            for hh in range(HGRN_N_HEADS):
                so_ref[nb, hh] = st_scr[nb, hh].T


def _hgrn_mixer(x, s0, p, *, NB, TL, C):
    n_seq, L, _ = x.shape
    has_init = s0 is not None
    R = NB * TL
    grid = (n_seq // NB, L // TL)
    st_block = (NB, HGRN_N_HEADS, HGRN_HEAD_DIM, HGRN_HEAD_DIM)
    in_specs = [pl.BlockSpec((NB, TL, D_MODEL), lambda b, i: (b, i, 0))]
    args = [x]
    if has_init:
        in_specs.append(pl.BlockSpec(st_block, lambda b, i: (b, 0, 0, 0)))
        args.append(s0)
    consts = [p["nw"], p["win"], p["lb"], p["gnw"], p["wo"]]
    in_specs += [_const_spec(c.shape) for c in consts]
    args += consts
    out_shape = (jax.ShapeDtypeStruct((n_seq, L, D_MODEL), F32),
                 jax.ShapeDtypeStruct((n_seq,) + st_block[1:], F32))
    out_specs = (pl.BlockSpec((NB, TL, D_MODEL), lambda b, i: (b, i, 0)),
                 pl.BlockSpec(st_block, lambda b, i: (b, 0, 0, 0)))
    scratch = [pltpu.VMEM(st_block, F32)] + [pltpu.VMEM((R, D_MODEL), F32) for _ in range(6)]
    return pl.pallas_call(
        functools.partial(_hgrn_kernel, has_init=has_init, NB=NB, TL=TL, C=C),
        grid=grid, in_specs=in_specs, out_specs=out_specs, out_shape=out_shape,
        scratch_shapes=scratch,
        compiler_params=pltpu.CompilerParams(
            dimension_semantics=("arbitrary", "arbitrary"), vmem_limit_bytes=VMEM_LIMIT),
        name="hgrn_mixer_init" if has_init else "hgrn_mixer_zero",
    )(*args)


def _ffn_kernel(*refs, final):
    if final:
        x_ref, nw_ref, wg_ref, wu_ref, wd_ref, fw_ref, o_ref = refs
    else:
        x_ref, nw_ref, wg_ref, wu_ref, wd_ref, o_ref = refs
    x = x_ref[...]
    h16 = _rms(x, nw_ref[...]).astype(BF16)
    act = _silu(_dot(h16, wg_ref[...])) * _dot(h16, wu_ref[...])
    y = x + _dot(act.astype(BF16), wd_ref[...])
    if final:
        y = _rms(y, fw_ref[...])
    o_ref[...] = y


def _ffn(x2d, p, final_w, *, TM):
    rows = x2d.shape[0]
    final = final_w is not None
    consts = [p["nw"], p["wg"], p["wu"], p["wd"]] + ([final_w] if final else [])
    return pl.pallas_call(
        functools.partial(_ffn_kernel, final=final),
        grid=(rows // TM,),
        in_specs=[pl.BlockSpec((TM, D_MODEL), lambda i: (i, 0))] + [_const_spec(c.shape) for c in consts],
        out_specs=pl.BlockSpec((TM, D_MODEL), lambda i: (i, 0)),
        out_shape=jax.ShapeDtypeStruct((rows, D_MODEL), F32),
        compiler_params=pltpu.CompilerParams(
            dimension_semantics=("arbitrary",), vmem_limit_bytes=VMEM_LIMIT),
        name="swiglu_final" if final else "swiglu",
    )(x2d, *consts)


def _trunk(x, s_ssd, cv, s_hgrn, ssd_p, hgrn_p, ffn_p, final_w, *, NB, TL, C, TM):
    n_seq, L, _ = x.shape
    s0 = None if s_ssd is None else s_ssd.reshape(n_seq, SSD_D_INNER, SSD_D_STATE)
    x, s_new, cv_new = _ssd_mixer(x, s0, cv, ssd_p, NB=NB, TL=TL, C=C)
    x = _ffn(x.reshape(n_seq * L, D_MODEL), ffn_p[0], None, TM=TM).reshape(n_seq, L, D_MODEL)
    x, h_new = _hgrn_mixer(x, s_hgrn, hgrn_p, NB=NB, TL=TL, C=C)
    y = _ffn(x.reshape(n_seq * L, D_MODEL), ffn_p[1], final_w, TM=TM).reshape(n_seq, L, D_MODEL)
    s_new = s_new.reshape(1, n_seq, SSD_N_HEADS, SSD_HEAD_DIM, SSD_D_STATE)
    return y, s_new, cv_new[None], h_new[None]


def kernel(x_prompt, x_sample, state_ssd, cache_conv, state_hgrn, ssd_norm_w, ssd_in_w, ssd_conv_w, ssd_conv_b, ssd_dt_bias, ssd_A_log, ssd_D, ssd_gnorm_w, ssd_out_w, hgrn_norm_w, hgrn_in_w, hgrn_lower_bounds, hgrn_gnorm_w, hgrn_out_w, ffn_norm_w, ffn_w_gate, ffn_w_up, ffn_w_down, final_norm_w):
    in_w = ssd_in_w[0]
    w_dt = in_w[:, SSD_D_INNER + SSD_CONV_DIM:]
    pair_order = jnp.concatenate([jnp.arange(0, SSD_N_HEADS, 2), jnp.arange(1, SSD_N_HEADS, 2)])
    ssd_p = {
        "nw": ssd_norm_w[0].reshape(1, D_MODEL),
        "wz": in_w[:, :SSD_D_INNER].astype(BF16),
        "wx": in_w[:, SSD_D_INNER:SSD_D_INNER + SSD_CONV_DIM].astype(BF16),
        "wdt": w_dt.astype(BF16),
        "wdtT": w_dt.T[pair_order].astype(BF16),
        "cw": ssd_conv_w[0],
        "cb": ssd_conv_b[0].reshape(1, SSD_CONV_DIM),
        "dtb": ssd_dt_bias[0].reshape(1, SSD_N_HEADS),
        "dtbT": ssd_dt_bias[0][pair_order].reshape(SSD_N_HEADS, 1),
        "alog": ssd_A_log[0],
        "dexp": jnp.repeat(ssd_D[0], SSD_HEAD_DIM).reshape(1, SSD_D_INNER),
        "gnw": ssd_gnorm_w[0].reshape(1, SSD_D_INNER),
        "wo": ssd_out_w[0].astype(BF16),
    }
    hgrn_p = {
        "nw": hgrn_norm_w[0].reshape(1, D_MODEL),
        "win": hgrn_in_w[0].astype(BF16),
        "lb": hgrn_lower_bounds,
        "gnw": jnp.tile(hgrn_gnorm_w[0], HGRN_N_HEADS).reshape(1, D_MODEL),
        "wo": hgrn_out_w[0].astype(BF16),
    }
    ffn_p = [{"nw": ffn_norm_w[l].reshape(1, D_MODEL), "wg": ffn_w_gate[l].astype(BF16),
              "wu": ffn_w_up[l].astype(BF16), "wd": ffn_w_down[l].astype(BF16)} for l in range(2)]
    final_w = final_norm_w.reshape(1, D_MODEL)

    y_p, ssd_s_p, conv_p, hgrn_s_p = _trunk(
        x_prompt, None, None, None, ssd_p, hgrn_p, ffn_p, final_w, NB=1, TL=256, C=64, TM=512)
    y_s, ssd_s_s, conv_s, hgrn_s_s = _trunk(
        x_sample, state_ssd[0], cache_conv[0], state_hgrn[0], ssd_p, hgrn_p, ffn_p, final_w,
        NB=4, TL=32, C=32, TM=512)
    return (y_p, y_s, ssd_s_p, conv_p, hgrn_s_p, ssd_s_s, conv_s, hgrn_s_s)
```

```python
import functools

import jax
import jax.numpy as jnp
from jax import lax
from jax.experimental import pallas as pl
from jax.experimental.pallas import tpu as pltpu

F32 = jnp.float32
BF16 = jnp.bfloat16

D_MODEL = 1024
NORM_EPS = 1e-6

SSD_D_INNER = 2048
SSD_HEAD_DIM = 64
SSD_N_HEADS = 32
SSD_N_GROUPS = 4
SSD_HEADS_PER_GROUP = 8
SSD_D_STATE = 128
SSD_CONV_W = 4
SSD_CONV_DIM = 3072
SSD_NORM_GROUP = 512
SSD_N_PAIRS = SSD_N_HEADS // 2

HGRN_HEAD_DIM = 128
HGRN_N_HEADS = 8

FFN_HIDDEN = 2816

LANES = 128
SUBLANES = 8
CONV_COLS = 512
HGRN_COLS = 256
VMEM_LIMIT = 56 * 1024 * 1024


def _dot(a, b):
    return jnp.dot(a, b, preferred_element_type=F32)


def _dot_nt(a, b):
    return lax.dot_general(a, b, (((1,), (1,)), ((), ())), preferred_element_type=F32)


def _dot_tn(a, b):
    return lax.dot_general(a, b, (((0,), (0,)), ((), ())), preferred_element_type=F32)


def _split3(a):
    hi = a.astype(BF16)
    r1 = a - hi.astype(F32)
    mid = r1.astype(BF16)
    lo = (r1 - mid.astype(F32)).astype(BF16)
    return hi, mid, lo


def _stack3(a):
    hi = a.astype(BF16).astype(F32)
    r1 = a - hi
    mid = r1.astype(BF16).astype(F32)
    return jnp.concatenate([hi, mid, r1 - mid, jnp.zeros_like(a)], axis=1).astype(BF16)


def _rms(x, w):
    return x * lax.rsqrt(jnp.mean(x * x, axis=-1, keepdims=True) + NORM_EPS) * w


def _silu(x):
    return x * jax.nn.sigmoid(x)


def _shift_rows(u, first_row):
    rows, cols = u.shape
    nv = rows // SUBLANES
    r = pltpu.roll(u.reshape(nv, SUBLANES, cols), 1, axis=1)
    first = jnp.broadcast_to(first_row, (1, SUBLANES, cols))
    prev = jnp.concatenate([first, r[:nv - 1]], axis=0)
    sub = lax.broadcasted_iota(jnp.int32, (nv, SUBLANES, cols), 1)
    return jnp.where(sub == 0, prev, r).reshape(rows, cols)


def _ssd_kernel(*refs, has_init, NB, TL, C):
    it = iter(refs)
    x_ref = next(it)
    s0_ref = cv0_ref = None
    if has_init:
        s0_ref = next(it)
        cv0_ref = next(it)
    (nw_ref, wz_ref, wx_ref, wdt_ref, wdtT_ref, cw_ref, cb_ref, dtb_ref, dtbT_ref,
     alog_ref, alogp_ref, dexp_ref, gnw_ref, wo_ref) = [next(it) for _ in range(14)]
    xo_ref = next(it)
    so_ref = next(it)
    cvo_ref = next(it)
    (st_scr, hist_scr, z_scr, xbc_scr, y_scr, e_scr, tri_scr,
     colc_scr, we_scr) = [next(it) for _ in range(9)]
    es_scr = colcs_scr = None
    if 2 * C != LANES:
        es_scr = next(it)
        colcs_scr = next(it)

    R = NB * TL
    n_ch = R // C
    ch_per_seq = TL // C
    W2 = 2 * C
    i = pl.program_id(1)
    n_i = pl.num_programs(1)

    @pl.when(i == 0)
    def _init():
        ek = lax.broadcasted_iota(jnp.int32, e_scr.shape, 0)
        el = lax.broadcasted_iota(jnp.int32, e_scr.shape, 1)
        e_scr[...] = ((ek < 3 * SSD_N_HEADS) & ((ek & (SSD_N_HEADS - 1)) == el // SSD_HEAD_DIM)).astype(BF16)
        if es_scr is not None:
            ek = lax.broadcasted_iota(jnp.int32, es_scr.shape, 0)
            el = lax.broadcasted_iota(jnp.int32, es_scr.shape, 1)
            es_scr[...] = ((ek < 3 * SSD_N_HEADS) & ((ek & (SSD_N_HEADS - 1)) == el // C)).astype(BF16)
        rr = lax.broadcasted_iota(jnp.int32, (R, R), 0)
        cc = lax.broadcasted_iota(jnp.int32, (R, R), 1)
        tri_scr[...] = ((cc <= rr) & ((rr & -C) == (cc & -C))).astype(BF16)
        hist_scr[...] = jnp.zeros_like(hist_scr)
        if has_init:
            for nb in range(NB):
                st_scr[nb] = s0_ref[nb].T
            hist_scr[:, SUBLANES - 3:SUBLANES, :] = cv0_ref[...]
        else:
            st_scr[...] = jnp.zeros_like(st_scr)

    x = x_ref[...].reshape(R, D_MODEL)
    h16 = _rms(x, nw_ref[...]).astype(BF16)
    for cb0 in range(0, SSD_CONV_DIM, CONV_COLS):
        cols = slice(cb0, cb0 + CONV_COLS)
        xr = _dot(h16, wx_ref[:, cols])
        w0, w1, w2, w3 = (cw_ref[k:k + 1, cols] for k in range(SSD_CONV_W))
        for nb in range(NB):
            xn = xr[nb * TL:(nb + 1) * TL]
            hm = hist_scr[nb, :, cols]
            xm3, xm2, xm1 = hm[5:6], hm[6:7], hm[7:8]
            u = _shift_rows(w0 * xn, w0 * xm1)
            u = _shift_rows(w1 * xn + u, w1 * xm1 + w0 * xm2)
            u = _shift_rows(w2 * xn + u, w2 * xm1 + w1 * xm2 + w0 * xm3)
            xbc_scr[nb * TL:(nb + 1) * TL, cols] = _silu(w3 * xn + u + cb_ref[:, cols])
            hist_scr[nb, :, cols] = xn[TL - SUBLANES:TL]
    for zb0 in range(0, SSD_D_INNER, CONV_COLS):
        cols = slice(zb0, zb0 + CONV_COLS)
        z_scr[:, cols] = _silu(_dot(h16, wz_ref[:, cols]))
    dt =jax.nn.softplus(_dot(h16, wdt_ref[...]) + dtb_ref[...])
    dtT = jax.nn.softplus(_dot_nt(wdtT_ref[...], h16) + dtbT_ref[...])
    dtTp = jnp.concatenate(
        [jnp.concatenate([dtT[0:SSD_N_PAIRS, c * C:(c + 1) * C],
                          dtT[SSD_N_PAIRS:, c * C:(c + 1) * C]], axis=1) for c in range(n_ch)],
        axis=0)

    a_row = -jnp.exp(alog_ref[...])
    a_pair = -jnp.exp(alogp_ref[...])
    cum = sum(_dot(tri_scr[...], p) for p in _split3(dt * a_row))
    last = jnp.concatenate(
        [jnp.broadcast_to(cum[(c + 1) * C - 1:(c + 1) * C, :], (C, SSD_N_HEADS)) for c in range(n_ch)], axis=0)
    wdt = jnp.exp(last - cum) * dt
    cum3 = _stack3(cum)
    colc_scr[...] = _dot(cum3, e_scr[...])
    we_scr[...] = _dot(_stack3(wdt), e_scr[...])
    if colcs_scr is not None:
        colcs_scr[...] = _dot(cum3, es_scr[...])
    r2 = lax.broadcasted_iota(jnp.int32, (W2, W2), 0)
    c2 = lax.broadcasted_iota(jnp.int32, (W2, W2), 1)
    triT2_16 = ((r2 <= c2) & ((r2 < C) == (c2 < C))).astype(BF16)
    aTp = dtTp * jnp.concatenate([a_pair] * n_ch, axis=0)
    cumTp = sum(_dot(p, triT2_16) for p in _split3(aTp))

    tp = lax.broadcasted_iota(jnp.int32, (C, W2), 0)
    sp = lax.broadcasted_iota(jnp.int32, (C, W2), 1)
    tril_pair = jnp.where(sp < C, sp, sp - C) <= tp
    left_e = lax.broadcasted_iota(jnp.int32, (C, LANES), 1) < SSD_HEAD_DIM

    for c in range(n_ch):
        nb = c // ch_per_seq
        rows = slice(c * C, (c + 1) * C)
        dtTp_c = dtTp[c * SSD_N_PAIRS:(c + 1) * SSD_N_PAIRS, :]
        cumTp_c = cumTp[c * SSD_N_PAIRS:(c + 1) * SSD_N_PAIRS, :]
        for g in range(SSD_N_GROUPS):
            bo = SSD_D_INNER + g * SSD_D_STATE
            co = SSD_D_INNER + SSD_N_GROUPS * SSD_D_STATE + g * SSD_D_STATE
            Bg16 = xbc_scr[rows, bo:bo + SSD_D_STATE].astype(BF16)
            Cg16 = xbc_scr[rows, co:co + SSD_D_STATE].astype(BF16)
            cbp = _dot_nt(Cg16, jnp.concatenate([Bg16, Bg16], axis=0))
            so = g * SSD_NORM_GROUP
            st_g = st_scr[nb, :, so:so + SSD_NORM_GROUP]
            y_inter = _dot(Cg16, st_g.astype(BF16))
            xw_parts = []
            el_parts = []
            for q in range(4):
                jp = g * 4 + q
                lanes = slice(jp * LANES, (jp + 1) * LANES)
                colc_e = colc_scr[rows, lanes]
                if W2 == LANES:
                    colc_l = colc_e
                else:
                    colc_l = colcs_scr[rows, jp * W2:(jp + 1) * W2]
                seg = colc_l - cumTp_c[q + 4 * g:q + 4 * g + 1, :]
                lm = jnp.where(tril_pair, jnp.exp(jnp.where(tril_pair, seg, 0.0)), 0.0)
                mp = cbp * lm * dtTp_c[jp:jp + 1, :]
                xs_p = xbc_scr[rows, lanes]
                rhs = jnp.concatenate(
                    [jnp.where(left_e, xs_p, 0.0).astype(BF16),
                     jnp.where(left_e, 0.0, xs_p).astype(BF16)], axis=0)
                e_c = jnp.exp(colc_e)
                y_p = _dot(mp.astype(BF16), rhs) + e_c * y_inter[:, q * LANES:(q + 1) * LANES]
                y_scr[rows, lanes] = y_p
                xw_parts.append((xs_p * we_scr[rows, lanes]).astype(BF16))
                el_parts.append(e_c[C - 1:C, :])
            xw_g = jnp.concatenate(xw_parts, axis=1)
            el_g = jnp.concatenate(el_parts, axis=1)
            st_scr[nb, :, so:so + SSD_NORM_GROUP] = el_g * st_g + _dot_tn(Bg16, xw_g)

    y = y_scr[...] + dexp_ref[...] * xbc_scr[:, :SSD_D_INNER]
    yg = y * z_scr[...]
    parts = []
    for g in range(SSD_N_GROUPS):
        blk = yg[:, g * SSD_NORM_GROUP:(g + 1) * SSD_NORM_GROUP]
        parts.append(blk * lax.rsqrt(jnp.mean(blk * blk, axis=-1, keepdims=True) + NORM_EPS))
    yn = jnp.concatenate(parts, axis=1) * gnw_ref[...]
    out = _dot(yn.astype(BF16), wo_ref[...])
    xo_ref[...] = (x + out).reshape(NB, TL, D_MODEL)

    @pl.when(i == n_i - 1)
    def _fin():
        for nb in range(NB):
            so_ref[nb] = st_scr[nb].T
        cvo_ref[...] = hist_scr[:, SUBLANES - 3:SUBLANES, :]


def _const_spec(shape):
    nd = len(shape)
    return pl.BlockSpec(shape, lambda *_: (0,) * nd, pipeline_mode=pl.Buffered(1))


def _ssd_mixer(x, s0, cv0, p, *, NB, TL, C):
    n_seq, L, _ = x.shape
    has_init = s0 is not None
    R = NB * TL
    grid = (n_seq // NB, L // TL)
    in_specs = [pl.BlockSpec((NB, TL, D_MODEL), lambda b, i: (b, i, 0))]
    args = [x]
    if has_init:
        in_specs += [pl.BlockSpec((NB, SSD_D_INNER, SSD_D_STATE), lambda b, i: (b, 0, 0)),
                     pl.BlockSpec((NB, 3, SSD_CONV_DIM), lambda b, i: (b, 0, 0))]
        args += [s0, cv0]
    alog_pair = jnp.concatenate(
        [jnp.broadcast_to(p["alog"][0::2, None], (SSD_N_PAIRS, C)),
         jnp.broadcast_to(p["alog"][1::2, None], (SSD_N_PAIRS, C))], axis=1)
    consts = [p["nw"], p["wz"], p["wx"], p["wdt"], p["wdtT"], p["cw"], p["cb"], p["dtb"], p["dtbT"],
              p["alog"].reshape(1, SSD_N_HEADS), alog_pair, p["dexp"], p["gnw"], p["wo"]]
    in_specs += [_const_spec(c.shape) for c in consts]
    args += consts
    out_shape = (jax.ShapeDtypeStruct((n_seq, L, D_MODEL), F32),
                 jax.ShapeDtypeStruct((n_seq, SSD_D_INNER, SSD_D_STATE), F32),
                 jax.ShapeDtypeStruct((n_seq, 3, SSD_CONV_DIM), F32))
    out_specs = (pl.BlockSpec((NB, TL, D_MODEL), lambda b, i: (b, i, 0)),
                 pl.BlockSpec((NB, SSD_D_INNER, SSD_D_STATE), lambda b, i: (b, 0, 0)),
                 pl.BlockSpec((NB, 3, SSD_CONV_DIM), lambda b, i: (b, 0, 0)))
    scratch = [pltpu.VMEM((NB, SSD_D_STATE, SSD_D_INNER), F32),
               pltpu.VMEM((NB, SUBLANES, SSD_CONV_DIM), F32),
               pltpu.VMEM((R, SSD_D_INNER), F32),
               pltpu.VMEM((R, SSD_CONV_DIM), F32),
               pltpu.VMEM((R, SSD_D_INNER), F32),
               pltpu.VMEM((LANES, SSD_D_INNER), BF16),
               pltpu.VMEM((R, R), BF16),
               pltpu.VMEM((R, SSD_D_INNER), F32),
               pltpu.VMEM((R, SSD_D_INNER), F32)]
    if 2 * C != LANES:
        scratch += [pltpu.VMEM((LANES, SSD_N_PAIRS * 2 * C), BF16),
                    pltpu.VMEM((R, SSD_N_PAIRS * 2 * C), F32)]
    return pl.pallas_call(
        functools.partial(_ssd_kernel, has_init=has_init, NB=NB, TL=TL, C=C),
        grid=grid, in_specs=in_specs, out_specs=out_specs, out_shape=out_shape,
        scratch_shapes=scratch,
        compiler_params=pltpu.CompilerParams(
            dimension_semantics=("arbitrary", "arbitrary"), vmem_limit_bytes=VMEM_LIMIT),
        name="ssd_mixer_init" if has_init else "ssd_mixer_zero",
    )(*args)


def _hgrn_kernel(*refs, has_init, NB, TL, C):
    it = iter(refs)
    x_ref = next(it)
    s0_ref = next(it) if has_init else None
    nw_ref, win_ref, lb_ref, gnw_ref, wo_ref = [next(it) for _ in range(5)]
    xo_ref = next(it)
    so_ref = next(it)
    st_scr, tri_scr, qe_scr, ke_scr, kw_scr, v_scr, g_scr, el_scr, o_scr = [next(it) for _ in range(9)]

    R = NB * TL
    n_ch = R // C
    ch_per_seq = TL // C
    HD = HGRN_HEAD_DIM
    i = pl.program_id(1)
    n_i = pl.num_programs(1)

    @pl.when(i == 0)
    def _init():
        tr = lax.broadcasted_iota(jnp.int32, (R, R), 0)
        tc = lax.broadcasted_iota(jnp.int32, (R, R), 1)
        tri_scr[...] = ((tc <= tr) & ((tr & -C) == (tc & -C))).astype(BF16)
        if has_init:
            for nb in range(NB):
                for hh in range(HGRN_N_HEADS):
                    st_scr[nb, hh] = s0_ref[nb, hh].T
        else:
            st_scr[...] = jnp.zeros_like(st_scr)

    x = x_ref[...].reshape(R, D_MODEL)
    h16 = _rms(x, nw_ref[...]).astype(BF16)
    lb_soft = jax.nn.softmax(lb_ref[...], axis=0)
    lb = (lb_soft[0:1, :] + lb_soft[1:2, :]) - lb_soft[0:1, :]

    for c0 in range(0, D_MODEL, HGRN_COLS):
        cols = slice(c0, c0 + HGRN_COLS)
        q = _silu(_dot(h16, win_ref[:, c0:c0 + HGRN_COLS]))
        f = _dot(h16, win_ref[:, D_MODEL + c0:D_MODEL + c0 + HGRN_COLS])
        v = _dot(h16, win_ref[:, 2 * D_MODEL + c0:2 * D_MODEL + c0 + HGRN_COLS])
        g = _dot(h16, win_ref[:, 3 * D_MODEL + c0:3 * D_MODEL + c0 + HGRN_COLS])
        lbc = lb[:, cols]
        forget = lbc + (1.0 - lbc) * jax.nn.sigmoid(f)
        k = 1.0 - forget
        b = sum(_dot(tri_scr[...], p) for p in _split3(jnp.log(forget)))
        ends = [b[(c + 1) * C - 1:(c + 1) * C, :] for c in range(n_ch)]
        last = jnp.concatenate([jnp.broadcast_to(e, (C, HGRN_COLS)) for e in ends], axis=0)
        qe_scr[:, cols] = (q * jnp.exp(b)).astype(BF16)
        ke_scr[:, cols] = (k * jnp.exp(-b)).astype(BF16)
        kw_scr[:, cols] = (k * jnp.exp(last - b)).astype(BF16)
        v_scr[:, cols] = v.astype(BF16)
        g_scr[:, cols] = _silu(g)
        el_scr[:, cols] = jnp.exp(jnp.concatenate(ends, axis=0))

    rr = lax.broadcasted_iota(jnp.int32, (C, C), 0)
    cc = lax.broadcasted_iota(jnp.int32, (C, C), 1)
    tril = cc <= rr

    for c in range(n_ch):
        nb = c // ch_per_seq
        rows = slice(c * C, (c + 1) * C)
        for hh in range(HGRN_N_HEADS):
            sl = slice(hh * HD, (hh + 1) * HD)
            qe, v = qe_scr[rows, sl], v_scr[rows, sl]
            sc = jnp.where(tril, _dot_nt(qe, ke_scr[rows, sl]), 0.0)
            s_h = st_scr[nb, hh]
            o_scr[rows, sl] = _dot(sc.astype(BF16), v) + _dot_nt(qe, s_h.astype(BF16))
            st_scr[nb, hh] = el_scr[c:c + 1, sl] * s_h + _dot_tn(v, kw_scr[rows, sl])

    o = o_scr[...]
    parts = []
    for hh in range(HGRN_N_HEADS):
        blk = o[:, hh * HD:(hh + 1) * HD]
        parts.append(blk * lax.rsqrt(jnp.mean(blk * blk, axis=-1, keepdims=True) + NORM_EPS))
    on = jnp.concatenate(parts, axis=1) * gnw_ref[...] * g_scr[...]
    out = _dot(on.astype(BF16), wo_ref[...])
    xo_ref[...] = (x + out).reshape(NB, TL, D_MODEL)

    @pl.when(i == n_i - 1)
    def _fin():
        for nb in range(NB):
            for hh in range(HGRN_N_HEADS):
                so_ref[nb, hh] = st_scr[nb, hh].T


def _hgrn_mixer(x, s0, p, *, NB, TL, C):
    n_seq, L, _ = x.shape
    has_init = s0 is not None
    R = NB * TL
    grid = (n_seq // NB, L // TL)
    st_block = (NB, HGRN_N_HEADS, HGRN_HEAD_DIM, HGRN_HEAD_DIM)
    in_specs = [pl.BlockSpec((NB, TL, D_MODEL), lambda b, i: (b, i, 0))]
    args = [x]
    if has_init:
        in_specs.append(pl.BlockSpec(st_block, lambda b, i: (b, 0, 0, 0)))
        args.append(s0)
    consts = [p["nw"], p["win"], p["lb"], p["gnw"], p["wo"]]
    in_specs += [_const_spec(c.shape) for c in consts]
    args += consts
    out_shape = (jax.ShapeDtypeStruct((n_seq, L, D_MODEL), F32),
                 jax.ShapeDtypeStruct((n_seq,) + st_block[1:], F32))
    out_specs = (pl.BlockSpec((NB, TL, D_MODEL), lambda b, i: (b, i, 0)),
                 pl.BlockSpec(st_block, lambda b, i: (b, 0, 0, 0)))
    scratch = [pltpu.VMEM(st_block, F32),
               pltpu.VMEM((R, R), BF16),
               pltpu.VMEM((R, D_MODEL), BF16),
               pltpu.VMEM((R, D_MODEL), BF16),
               pltpu.VMEM((R, D_MODEL), BF16),
               pltpu.VMEM((R, D_MODEL), BF16),
               pltpu.VMEM((R, D_MODEL), F32),
               pltpu.VMEM((R // C, D_MODEL), F32),
               pltpu.VMEM((R, D_MODEL), F32)]
    return pl.pallas_call(
        functools.partial(_hgrn_kernel, has_init=has_init, NB=NB, TL=TL, C=C),
        grid=grid, in_specs=in_specs, out_specs=out_specs, out_shape=out_shape,
        scratch_shapes=scratch,
        compiler_params=pltpu.CompilerParams(
            dimension_semantics=("arbitrary", "arbitrary"), vmem_limit_bytes=VMEM_LIMIT),
        name="hgrn_mixer_init" if has_init else "hgrn_mixer_zero",
    )(*args)


def _ffn_kernel(*refs, final):
    if final:
        x_ref, nw_ref, wg_ref, wu_ref, wd_ref, fw_ref, o_ref = refs
    else:
        x_ref, nw_ref, wg_ref, wu_ref, wd_ref, o_ref = refs
    x = x_ref[...]
    h16 = _rms(x, nw_ref[...]).astype(BF16)
    act = _silu(_dot(h16, wg_ref[...])) * _dot(h16, wu_ref[...])
    y = x + _dot(act.astype(BF16), wd_ref[...])
    if final:
        y = _rms(y, fw_ref[...])
    o_ref[...] = y


def _ffn(x2d, p, final_w, *, TM):
    rows = x2d.shape[0]
    final = final_w is not None
    consts = [p["nw"], p["wg"], p["wu"], p["wd"]] + ([final_w] if final else [])
    return pl.pallas_call(
        functools.partial(_ffn_kernel, final=final),
        grid=(rows // TM,),
        in_specs=[pl.BlockSpec((TM, D_MODEL), lambda i: (i, 0))] + [_const_spec(c.shape) for c in consts],
        out_specs=pl.BlockSpec((TM, D_MODEL), lambda i: (i, 0)),
        out_shape=jax.ShapeDtypeStruct((rows, D_MODEL), F32),
        compiler_params=pltpu.CompilerParams(
            dimension_semantics=("arbitrary",), vmem_limit_bytes=VMEM_LIMIT),
        name="swiglu_final" if final else "swiglu",
    )(x2d, *consts)


def _trunk(x, s_ssd, cv, s_hgrn, ssd_p, hgrn_p, ffn_p, final_w, *, NB, TL, C, TM):
    n_seq, L, _ = x.shape
    s0 = None if s_ssd is None else s_ssd.reshape(n_seq, SSD_D_INNER, SSD_D_STATE)
    x, s_new, cv_new = _ssd_mixer(x, s0, cv, ssd_p, NB=NB, TL=TL, C=C)
    x = _ffn(x.reshape(n_seq * L, D_MODEL), ffn_p[0], None, TM=TM).reshape(n_seq, L, D_MODEL)
    x, h_new = _hgrn_mixer(x, s_hgrn, hgrn_p, NB=NB, TL=TL, C=C)
    y = _ffn(x.reshape(n_seq * L, D_MODEL), ffn_p[1], final_w, TM=TM).reshape(n_seq, L, D_MODEL)
    s_new = s_new.reshape(1, n_seq, SSD_N_HEADS, SSD_HEAD_DIM, SSD_D_STATE)
    return y, s_new, cv_new[None], h_new[None]


def kernel(x_prompt, x_sample, state_ssd, cache_conv, state_hgrn, ssd_norm_w, ssd_in_w, ssd_conv_w, ssd_conv_b, ssd_dt_bias, ssd_A_log, ssd_D, ssd_gnorm_w, ssd_out_w, hgrn_norm_w, hgrn_in_w, hgrn_lower_bounds, hgrn_gnorm_w, hgrn_out_w, ffn_norm_w, ffn_w_gate, ffn_w_up, ffn_w_down, final_norm_w):
    in_w = ssd_in_w[0]
    w_dt = in_w[:, SSD_D_INNER + SSD_CONV_DIM:]
    pair_order = jnp.concatenate([jnp.arange(0, SSD_N_HEADS, 2), jnp.arange(1, SSD_N_HEADS, 2)])
    ssd_p = {
        "nw": ssd_norm_w[0].reshape(1, D_MODEL),
        "wz": in_w[:, :SSD_D_INNER].astype(BF16),
        "wx": in_w[:, SSD_D_INNER:SSD_D_INNER + SSD_CONV_DIM].astype(BF16),
        "wdt": w_dt.astype(BF16),
        "wdtT": w_dt.T[pair_order].astype(BF16),
        "cw": ssd_conv_w[0],
        "cb": ssd_conv_b[0].reshape(1, SSD_CONV_DIM),
        "dtb": ssd_dt_bias[0].reshape(1, SSD_N_HEADS),
        "dtbT": ssd_dt_bias[0][pair_order].reshape(SSD_N_HEADS, 1),
        "alog": ssd_A_log[0],
        "dexp": jnp.repeat(ssd_D[0], SSD_HEAD_DIM).reshape(1, SSD_D_INNER),
        "gnw": ssd_gnorm_w[0].reshape(1, SSD_D_INNER),
        "wo": ssd_out_w[0].astype(BF16),
    }
    hgrn_p = {
        "nw": hgrn_norm_w[0].reshape(1, D_MODEL),
        "win": hgrn_in_w[0].astype(BF16),
        "lb": hgrn_lower_bounds,
        "gnw": jnp.tile(hgrn_gnorm_w[0], HGRN_N_HEADS).reshape(1, D_MODEL),
        "wo": hgrn_out_w[0].astype(BF16),
    }
    ffn_p = [{"nw": ffn_norm_w[l].reshape(1, D_MODEL), "wg": ffn_w_gate[l].astype(BF16),
              "wu": ffn_w_up[l].astype(BF16), "wd": ffn_w_down[l].astype(BF16)} for l in range(2)]
    final_w = final_norm_w.reshape(1, D_MODEL)

    y_p, ssd_s_p, conv_p, hgrn_s_p = _trunk(
        x_prompt, None, None, None, ssd_p, hgrn_p, ffn_p, final_w, NB=1, TL=256, C=64, TM=512)
    y_s, ssd_s_s, conv_s, hgrn_s_s = _trunk(
        x_sample, state_ssd[0], cache_conv[0], state_hgrn[0], ssd_p, hgrn_p, ffn_p, final_w,
        NB=4, TL=32, C=32, TM=512)
    return (y_p, y_s, ssd_s_p, conv_p, hgrn_s_p, ssd_s_s, conv_s, hgrn_s_s)
```

```python
import functools

import jax
import jax.numpy as jnp
from jax import lax
from jax.experimental import pallas as pl
from jax.experimental.pallas import tpu as pltpu

F32 = jnp.float32
BF16 = jnp.bfloat16

D_MODEL = 1024
NORM_EPS = 1e-6

SSD_D_INNER = 2048
SSD_HEAD_DIM = 64
SSD_N_HEADS = 32
SSD_N_GROUPS = 4
SSD_D_STATE = 128
SSD_CONV_W = 4
SSD_CONV_DIM = 3072
SSD_NORM_GROUP = 512
SSD_N_PAIRS = SSD_N_HEADS // 2
SSD_XBC_OFF = SSD_D_INNER
SSD_DT_OFF = SSD_D_INNER + SSD_CONV_DIM

HGRN_HEAD_DIM = 128
HGRN_N_HEADS = 8

FFN_HIDDEN = 2816

LANES = 128
SUBLANES = 8
CONV_COLS = 512
HGRN_COLS = 256
VMEM_LIMIT = 56 * 1024 * 1024


def _dot(a, b):
    return jnp.dot(a, b, preferred_element_type=F32)


def _dot_nt(a, b):
    return lax.dot_general(a, b, (((1,), (1,)), ((), ())), preferred_element_type=F32)


def _dot_tn(a, b):
    return lax.dot_general(a, b, (((0,), (0,)), ((), ())), preferred_element_type=F32)


def _split3(a):
    hi = a.astype(BF16)
    r1 = a - hi.astype(F32)
    mid = r1.astype(BF16)
    lo = (r1 - mid.astype(F32)).astype(BF16)
    return hi, mid, lo


def _stack3(a):
    hi = a.astype(BF16).astype(F32)
    r1 = a - hi
    mid = r1.astype(BF16).astype(F32)
    return jnp.concatenate([hi, mid, r1 - mid, jnp.zeros_like(a)], axis=1).astype(BF16)


def _rms(x, w):
    return x * lax.rsqrt(jnp.mean(x * x, axis=-1, keepdims=True) + NORM_EPS) * w


def _silu(x):
    return x * jax.nn.sigmoid(x)


def _shift_rows(u, first_row):
    rows, cols = u.shape
    nv = rows // SUBLANES
    r = pltpu.roll(u.reshape(nv, SUBLANES, cols), 1, axis=1)
    first = jnp.broadcast_to(first_row, (1, SUBLANES, cols))
    prev = jnp.concatenate([first, r[:nv - 1]], axis=0)
    sub = lax.broadcasted_iota(jnp.int32, (nv, SUBLANES, cols), 1)
    return jnp.where(sub == 0, prev, r).reshape(rows, cols)


def _chunk_triangle(R, C):
    rr = lax.broadcasted_iota(jnp.int32, (R, R), 0)
    cc = lax.broadcasted_iota(jnp.int32, (R, R), 1)
    return ((cc <= rr) & ((rr & -C) == (cc & -C))).astype(BF16)


def _ssd_kernel(*refs, has_init, NB, TL, C):
    it = iter(refs)
    x_ref = next(it)
    s0_ref = cv0_ref = None
    if has_init:
        s0_ref = next(it)
        cv0_ref = next(it)
    (nw_ref, win_ref, wdtT_ref, cw_ref, cb_ref, dtb_ref, dtbT_ref,
     alog_ref, alogp_ref, dexp_ref, gnw_ref, wo_ref) = [next(it) for _ in range(12)]
    xo_ref = next(it)
    so_ref = next(it)
    cvo_ref = next(it)
    (st_scr, hist_scr, z_scr, xbc_scr, y_scr, e_scr, tri_scr,
     colc_scr, we_scr) = [next(it) for _ in range(9)]
    es_scr = colcs_scr = None
    if 2 * C != LANES:
        es_scr = next(it)
        colcs_scr = next(it)

    R = NB * TL
    n_ch = R // C
    ch_per_seq = TL // C
    W2 = 2 * C
    i = pl.program_id(1)
    n_i = pl.num_programs(1)

    @pl.when(i == 0)
    def _init():
        ek = lax.broadcasted_iota(jnp.int32, e_scr.shape, 0)
        el = lax.broadcasted_iota(jnp.int32, e_scr.shape, 1)
        e_scr[...] = ((ek < 3 * SSD_N_HEADS) & ((ek & (SSD_N_HEADS - 1)) == el // SSD_HEAD_DIM)).astype(BF16)
        if es_scr is not None:
            ek = lax.broadcasted_iota(jnp.int32, es_scr.shape, 0)
            el = lax.broadcasted_iota(jnp.int32, es_scr.shape, 1)
            es_scr[...] = ((ek < 3 * SSD_N_HEADS) & ((ek & (SSD_N_HEADS - 1)) == el // C)).astype(BF16)
        tri_scr[...] = _chunk_triangle(R, C)
        hist_scr[...] = jnp.zeros_like(hist_scr)
        if has_init:
            for nb in range(NB):
                st_scr[nb] = s0_ref[nb].T
            hist_scr[:, SUBLANES - 3:SUBLANES, :] = cv0_ref[...]
        else:
            st_scr[...] = jnp.zeros_like(st_scr)

    x = x_ref[...].reshape(R, D_MODEL)
    h16 = _rms(x, nw_ref[...]).astype(BF16)

    for cb0 in range(0, SSD_CONV_DIM, CONV_COLS):
        cols = slice(cb0, cb0 + CONV_COLS)
        xr = _dot(h16, win_ref[:, SSD_XBC_OFF + cb0:SSD_XBC_OFF + cb0 + CONV_COLS])
        w0, w1, w2, w3 = (cw_ref[k:k + 1, cols] for k in range(SSD_CONV_W))
        for nb in range(NB):
            xn = xr[nb * TL:(nb + 1) * TL]
            hm = hist_scr[nb, :, cols]
            xm3, xm2, xm1 = hm[5:6], hm[6:7], hm[7:8]
            u = _shift_rows(w0 * xn, w0 * xm1)
            u = _shift_rows(w1 * xn + u, w1 * xm1 + w0 * xm2)
            u = _shift_rows(w2 * xn + u, w2 * xm1 + w1 * xm2 + w0 * xm3)
            xbc_scr[nb * TL:(nb + 1) * TL, cols] = _silu(w3 * xn + u + cb_ref[:, cols])
            hist_scr[nb, :, cols] = xn[TL - SUBLANES:TL]
    for zb0 in range(0, SSD_D_INNER, CONV_COLS):
        z_scr[:, zb0:zb0 + CONV_COLS] = _silu(_dot(h16, win_ref[:, zb0:zb0 + CONV_COLS]))

    dt = jax.nn.softplus(_dot(h16, win_ref[:, SSD_DT_OFF:]) + dtb_ref[...])
    dtT = jax.nn.softplus(_dot_nt(wdtT_ref[...], h16) + dtbT_ref[...])
    dtTp = jnp.concatenate(
        [jnp.concatenate([dtT[0:SSD_N_PAIRS, c * C:(c + 1) * C],
                          dtT[SSD_N_PAIRS:, c * C:(c + 1) * C]], axis=1) for c in range(n_ch)],
        axis=0)
    a_row = -jnp.exp(alog_ref[...])
    a_pair = -jnp.exp(alogp_ref[...])
    cum = sum(_dot(tri_scr[...], p) for p in _split3(dt * a_row))
    last = jnp.concatenate(
        [jnp.broadcast_to(cum[(c + 1) * C - 1:(c + 1) * C, :], (C, SSD_N_HEADS)) for c in range(n_ch)], axis=0)
    wdt = jnp.exp(last - cum) * dt
    cum3 = _stack3(cum)
    colc_scr[...] = _dot(cum3, e_scr[...])
    we_scr[...] = _dot(_stack3(wdt), e_scr[...])
    if colcs_scr is not None:
        colcs_scr[...] = _dot(cum3, es_scr[...])
    r2 = lax.broadcasted_iota(jnp.int32, (W2, W2), 0)
    c2 = lax.broadcasted_iota(jnp.int32, (W2, W2), 1)
    triT2_16 = ((r2 <= c2) & ((r2 < C) == (c2 < C))).astype(BF16)
    aTp = dtTp * jnp.concatenate([a_pair] * n_ch, axis=0)
    cumTp = sum(_dot(p, triT2_16) for p in _split3(aTp))

    tp = lax.broadcasted_iota(jnp.int32, (C, W2), 0)
    sp = lax.broadcasted_iota(jnp.int32, (C, W2), 1)
    tril_pair = jnp.where(sp < C, sp, sp - C) <= tp
    left_e = lax.broadcasted_iota(jnp.int32, (C, LANES), 1) < SSD_HEAD_DIM

    for c in range(n_ch):
        nb = c // ch_per_seq
        rows = slice(c * C, (c + 1) * C)
        dtTp_c = dtTp[c * SSD_N_PAIRS:(c + 1) * SSD_N_PAIRS, :]
        cumTp_c = cumTp[c * SSD_N_PAIRS:(c + 1) * SSD_N_PAIRS, :]
        for g in range(SSD_N_GROUPS):
            bo = SSD_D_INNER + g * SSD_D_STATE
            co = SSD_D_INNER + SSD_N_GROUPS * SSD_D_STATE + g * SSD_D_STATE
            Bg16 = xbc_scr[rows, bo:bo + SSD_D_STATE].astype(BF16)
            Cg16 = xbc_scr[rows, co:co + SSD_D_STATE].astype(BF16)
            cbp = _dot_nt(Cg16, jnp.concatenate([Bg16, Bg16], axis=0))
            cbm = jnp.where(tril_pair, cbp, 0.0)
            so = g * SSD_NORM_GROUP
            st_g = st_scr[nb, :, so:so + SSD_NORM_GROUP]
            y_inter = _dot(Cg16, st_g.astype(BF16))
            xw_parts = []
            el_parts = []
            for q in range(4):
                jp = g * 4 + q
                lanes = slice(jp * LANES, (jp + 1) * LANES)
                colc_e = colc_scr[rows, lanes]
                if W2 == LANES:
                    colc_l = colc_e
                else:
                    colc_l = colcs_scr[rows, jp * W2:(jp + 1) * W2]
                decay = jnp.exp(jnp.minimum(colc_l - cumTp_c[jp:jp + 1, :], 0.0))
                mp = cbm * decay * dtTp_c[jp:jp + 1, :]
                xs_p = xbc_scr[rows, lanes]
                rhs = jnp.concatenate(
                    [jnp.where(left_e, xs_p, 0.0).astype(BF16),
                     jnp.where(left_e, 0.0, xs_p).astype(BF16)], axis=0)
                e_c = jnp.exp(colc_e)
                y_scr[rows, lanes] = _dot(mp.astype(BF16), rhs) + e_c * y_inter[:, q * LANES:(q + 1) * LANES]
                xw_parts.append((xs_p * we_scr[rows, lanes]).astype(BF16))
                el_parts.append(e_c[C - 1:C, :])
            xw_g = jnp.concatenate(xw_parts, axis=1)
            el_g = jnp.concatenate(el_parts, axis=1)
            st_scr[nb, :, so:so + SSD_NORM_GROUP] = el_g * st_g + _dot_tn(Bg16, xw_g)

    out = None
    for g in range(SSD_N_GROUPS):
        cols = slice(g * SSD_NORM_GROUP, (g + 1) * SSD_NORM_GROUP)
        yg = (y_scr[:, cols] + dexp_ref[:, cols] * xbc_scr[:, cols]) * z_scr[:, cols]
        yn = yg * lax.rsqrt(jnp.mean(yg * yg, axis=-1, keepdims=True) + NORM_EPS) * gnw_ref[:, cols]
        part = _dot(yn.astype(BF16), wo_ref[cols, :])
        out = part if out is None else out + part
    xo_ref[...] = (x + out).reshape(NB, TL, D_MODEL)

    @pl.when(i == n_i - 1)
    def _fin():
        for nb in range(NB):
            so_ref[nb] = st_scr[nb].T
        cvo_ref[...] = hist_scr[:, SUBLANES - 3:SUBLANES, :]


def _const_spec(shape):
    nd = len(shape)
    return pl.BlockSpec(shape, lambda *_: (0,) * nd, pipeline_mode=pl.Buffered(1))


def _layer_spec(arr, layer):
    nd = arr.ndim - 1
    return pl.BlockSpec((None,) + arr.shape[1:], lambda *_: (layer,) + (0,) * nd,
                        pipeline_mode=pl.Buffered(1))


def _ssd_mixer(x, s0, cv0, p, *, NB, TL, C):
    n_seq, L, _ = x.shape
    has_init = s0 is not None
    R = NB * TL
    grid = (n_seq // NB, L // TL)
    in_specs = [pl.BlockSpec((NB, TL, D_MODEL), lambda b, i: (b, i, 0))]
    args = [x]
    if has_init:
        in_specs += [pl.BlockSpec((NB, SSD_D_INNER, SSD_D_STATE), lambda b, i: (b, 0, 0)),
                     pl.BlockSpec((NB, 3, SSD_CONV_DIM), lambda b, i: (b, 0, 0))]
        args += [s0, cv0]
    alog_pair = jnp.concatenate(
        [jnp.broadcast_to(p["alog"][0::2, None], (SSD_N_PAIRS, C)),
         jnp.broadcast_to(p["alog"][1::2, None], (SSD_N_PAIRS, C))], axis=1)
    consts = [p["nw"], p["win"], p["wdtT"], p["cw"], p["cb"], p["dtb"], p["dtbT"],
              p["alog"].reshape(1, SSD_N_HEADS), alog_pair, p["dexp"], p["gnw"], p["wo"]]
    for c in consts:
        in_specs.append(_layer_spec(c, 0) if c.ndim == 3 else _const_spec(c.shape))
    args += consts
    out_shape = (jax.ShapeDtypeStruct((n_seq, L, D_MODEL), F32),
                 jax.ShapeDtypeStruct((n_seq, SSD_D_INNER, SSD_D_STATE), F32),
                 jax.ShapeDtypeStruct((n_seq, 3, SSD_CONV_DIM), F32))
    out_specs = (pl.BlockSpec((NB, TL, D_MODEL), lambda b, i: (b, i, 0)),
                 pl.BlockSpec((NB, SSD_D_INNER, SSD_D_STATE), lambda b, i: (b, 0, 0)),
                 pl.BlockSpec((NB, 3, SSD_CONV_DIM), lambda b, i: (b, 0, 0)))
    scratch = [pltpu.VMEM((NB, SSD_D_STATE, SSD_D_INNER), F32),
               pltpu.VMEM((NB, SUBLANES, SSD_CONV_DIM), F32),
               pltpu.VMEM((R, SSD_D_INNER), F32),
               pltpu.VMEM((R, SSD_CONV_DIM), F32),
               pltpu.VMEM((R, SSD_D_INNER), F32),
               pltpu.VMEM((LANES, SSD_D_INNER), BF16),
               pltpu.VMEM((R, R), BF16),
               pltpu.VMEM((R, SSD_D_INNER), F32),
               pltpu.VMEM((R, SSD_D_INNER), F32)]
    if 2 * C != LANES:
        scratch += [pltpu.VMEM((LANES, SSD_N_PAIRS * 2 * C), BF16),
                    pltpu.VMEM((R, SSD_N_PAIRS * 2 * C), F32)]
    return pl.pallas_call(
        functools.partial(_ssd_kernel, has_init=has_init, NB=NB, TL=TL, C=C),
        grid=grid, in_specs=in_specs, out_specs=out_specs, out_shape=out_shape,
        scratch_shapes=scratch,
        compiler_params=pltpu.CompilerParams(
            dimension_semantics=("arbitrary", "arbitrary"), vmem_limit_bytes=VMEM_LIMIT),
        name="ssd_mixer_init" if has_init else "ssd_mixer_zero",
    )(*args)


def _hgrn_kernel(*refs, has_init, NB, TL, C):
    it = iter(refs)
    x_ref = next(it)
    s0_ref = next(it) if has_init else None
    nw_ref, win_ref, lb_ref, gnw_ref, wo_ref = [next(it) for _ in range(5)]
    xo_ref = next(it)
    so_ref = next(it)
    st_scr, tri_scr, qe_scr, ke_scr, kw_scr, v_scr, g_scr, el_scr, o_scr = [next(it) for _ in range(9)]

    R = NB * TL
    n_ch = R // C
    ch_per_seq = TL // C
    HD = HGRN_HEAD_DIM
    i = pl.program_id(1)
    n_i = pl.num_programs(1)

    @pl.when(i == 0)
    def _init():
        tri_scr[...] = _chunk_triangle(R, C)
        if has_init:
            for nb in range(NB):
                for hh in range(HGRN_N_HEADS):
                    st_scr[nb, hh] = s0_ref[nb, hh].T
        else:
            st_scr[...] = jnp.zeros_like(st_scr)

    x = x_ref[...].reshape(R, D_MODEL)
    h16 = _rms(x, nw_ref[...]).astype(BF16)
    lb_soft = jax.nn.softmax(lb_ref[...], axis=0)
    lb = (lb_soft[0:1, :] + lb_soft[1:2, :]) - lb_soft[0:1, :]

    for c0 in range(0, D_MODEL, HGRN_COLS):
        cols = slice(c0, c0 + HGRN_COLS)
        q = _silu(_dot(h16, win_ref[:, c0:c0 + HGRN_COLS]))
        f = _dot(h16, win_ref[:, D_MODEL + c0:D_MODEL + c0 + HGRN_COLS])
        v = _dot(h16, win_ref[:, 2 * D_MODEL + c0:2 * D_MODEL + c0 + HGRN_COLS])
        g = _dot(h16, win_ref[:, 3 * D_MODEL + c0:3 * D_MODEL + c0 + HGRN_COLS])
        lbc = lb[:, cols]
        forget = lbc + (1.0 - lbc) * jax.nn.sigmoid(f)
        k = 1.0 - forget
        b = sum(_dot(tri_scr[...], p) for p in _split3(jnp.log(forget)))
        ends = [b[(c + 1) * C - 1:(c + 1) * C, :] for c in range(n_ch)]
        last = jnp.concatenate([jnp.broadcast_to(e, (C, HGRN_COLS)) for e in ends], axis=0)
        qe_scr[:, cols] = (q * jnp.exp(b)).astype(BF16)
        ke_scr[:, cols] = (k * jnp.exp(-b)).astype(BF16)
        kw_scr[:, cols] = (k * jnp.exp(last - b)).astype(BF16)
        v_scr[:, cols] = v.astype(BF16)
        g_scr[:, cols] = _silu(g)
        el_scr[:, cols] = jnp.exp(jnp.concatenate(ends, axis=0))

    rr = lax.broadcasted_iota(jnp.int32, (C, C), 0)
    cc = lax.broadcasted_iota(jnp.int32, (C, C), 1)
    tril = cc <= rr

    for c in range(n_ch):
        nb = c // ch_per_seq
        rows = slice(c * C, (c + 1) * C)
        for hh in range(HGRN_N_HEADS):
            sl = slice(hh * HD, (hh + 1) * HD)
            qe, v = qe_scr[rows, sl], v_scr[rows, sl]
            sc = jnp.where(tril, _dot_nt(qe, ke_scr[rows, sl]), 0.0)
            s_h = st_scr[nb, hh]
            o_scr[rows, sl] = _dot(sc.astype(BF16), v) + _dot_nt(qe, s_h.astype(BF16))
            st_scr[nb, hh] = el_scr[c:c + 1, sl] * s_h + _dot_tn(v, kw_scr[rows, sl])

    out = None
    for c0 in range(0, D_MODEL, HGRN_COLS):
        parts = []
        for h0 in range(c0, c0 + HGRN_COLS, HD):
            blk = o_scr[:, h0:h0 + HD]
            parts.append(blk * lax.rsqrt(jnp.mean(blk * blk, axis=-1, keepdims=True) + NORM_EPS))
        cols = slice(c0, c0 + HGRN_COLS)
        on = jnp.concatenate(parts, axis=1) * gnw_ref[:, cols] * g_scr[:, cols]
        part = _dot(on.astype(BF16), wo_ref[cols, :])
        out = part if out is None else out + part
    xo_ref[...] = (x + out).reshape(NB, TL, D_MODEL)

    @pl.when(i == n_i - 1)
    def _fin():
        for nb in range(NB):
            for hh in range(HGRN_N_HEADS):
                so_ref[nb, hh] = st_scr[nb, hh].T


def _hgrn_mixer(x, s0, p, *, NB, TL, C):
    n_seq, L, _ = x.shape
    has_init = s0 is not None
    R = NB * TL
    grid = (n_seq // NB, L // TL)
    st_block = (NB, HGRN_N_HEADS, HGRN_HEAD_DIM, HGRN_HEAD_DIM)
    in_specs = [pl.BlockSpec((NB, TL, D_MODEL), lambda b, i: (b, i, 0))]
    args = [x]
    if has_init:
        in_specs.append(pl.BlockSpec(st_block, lambda b, i: (b, 0, 0, 0)))
        args.append(s0)
    consts = [p["nw"], p["win"], p["lb"], p["gnw"], p["wo"]]
    for c in consts:
        in_specs.append(_layer_spec(c, 0) if c.ndim == 3 else _const_spec(c.shape))
    args += consts
    out_shape = (jax.ShapeDtypeStruct((n_seq, L, D_MODEL), F32),
                 jax.ShapeDtypeStruct((n_seq,) + st_block[1:], F32))
    out_specs = (pl.BlockSpec((NB, TL, D_MODEL), lambda b, i: (b, i, 0)),
                 pl.BlockSpec(st_block, lambda b, i: (b, 0, 0, 0)))
    scratch = [pltpu.VMEM(st_block, F32),
               pltpu.VMEM((R, R), BF16),
               pltpu.VMEM((R, D_MODEL), BF16),
               pltpu.VMEM((R, D_MODEL), BF16),
               pltpu.VMEM((R, D_MODEL), BF16),
               pltpu.VMEM((R, D_MODEL), BF16),
               pltpu.VMEM((R, D_MODEL), F32),
               pltpu.VMEM((R // C, D_MODEL), F32),
               pltpu.VMEM((R, D_MODEL), F32)]
    return pl.pallas_call(
        functools.partial(_hgrn_kernel, has_init=has_init, NB=NB, TL=TL, C=C),
        grid=grid, in_specs=in_specs, out_specs=out_specs, out_shape=out_shape,
        scratch_shapes=scratch,
        compiler_params=pltpu.CompilerParams(
            dimension_semantics=("arbitrary", "arbitrary"), vmem_limit_bytes=VMEM_LIMIT),
        name="hgrn_mixer_init" if has_init else "hgrn_mixer_zero",
    )(*args)


def _ffn_kernel(*refs, final):
    if final:
        x_ref, nw_ref, wg_ref, wu_ref, wd_ref, fw_ref, o_ref = refs
    else:
        x_ref, nw_ref, wg_ref, wu_ref, wd_ref, o_ref = refs
    x = x_ref[...]
    h16 = _rms(x, nw_ref[...]).astype(BF16)
    act = _silu(_dot(h16, wg_ref[...])) * _dot(h16, wu_ref[...])
    y = x + _dot(act.astype(BF16), wd_ref[...])
    if final:
        y = _rms(y, fw_ref[...])
    o_ref[...] = y


def _ffn(x2d, p, layer, final_w, *, TM):
    rows = x2d.shape[0]
    final = final_w is not None
    in_specs = [pl.BlockSpec((TM, D_MODEL), lambda i: (i, 0)), _const_spec(p["nw"][layer].shape)]
    in_specs += [_layer_spec(p[k], layer) for k in ("wg", "wu", "wd")]
    args = [x2d, p["nw"][layer], p["wg"], p["wu"], p["wd"]]
    if final:
        in_specs.append(_const_spec(final_w.shape))
        args.append(final_w)
    return pl.pallas_call(
        functools.partial(_ffn_kernel, final=final),
        grid=(rows // TM,),
        in_specs=in_specs,
        out_specs=pl.BlockSpec((TM, D_MODEL), lambda i: (i, 0)),
        out_shape=jax.ShapeDtypeStruct((rows, D_MODEL), F32),
        compiler_params=pltpu.CompilerParams(
            dimension_semantics=("arbitrary",), vmem_limit_bytes=VMEM_LIMIT),
        name="swiglu_final" if final else "swiglu",
    )(*args)


def _trunk(x, s_ssd, cv, s_hgrn, ssd_p, hgrn_p, ffn_p, final_w, *, NB, TL, C, TM):
    n_seq, L, _ = x.shape
    s0 = None if s_ssd is None else s_ssd.reshape(n_seq, SSD_D_INNER, SSD_D_STATE)
    x, s_new, cv_new = _ssd_mixer(x, s0, cv, ssd_p, NB=NB, TL=TL, C=C)
    x = _ffn(x.reshape(n_seq * L, D_MODEL), ffn_p, 0, None, TM=TM).reshape(n_seq, L, D_MODEL)
    x, h_new = _hgrn_mixer(x, s_hgrn, hgrn_p, NB=NB, TL=TL, C=C)
    y = _ffn(x.reshape(n_seq * L, D_MODEL), ffn_p, 1, final_w, TM=TM).reshape(n_seq, L, D_MODEL)
    s_new = s_new.reshape(1, n_seq, SSD_N_HEADS, SSD_HEAD_DIM, SSD_D_STATE)
    return y, s_new, cv_new[None], h_new[None]


def kernel(x_prompt, x_sample, state_ssd, cache_conv, state_hgrn, ssd_norm_w, ssd_in_w, ssd_conv_w, ssd_conv_b, ssd_dt_bias, ssd_A_log, ssd_D, ssd_gnorm_w, ssd_out_w, hgrn_norm_w, hgrn_in_w, hgrn_lower_bounds, hgrn_gnorm_w, hgrn_out_w, ffn_norm_w, ffn_w_gate, ffn_w_up, ffn_w_down, final_norm_w):
    w_dt = ssd_in_w[0, :, SSD_DT_OFF:]
    pair_order = jnp.concatenate([jnp.arange(0, SSD_N_HEADS, 2), jnp.arange(1, SSD_N_HEADS, 2)])
    ssd_p = {
        "nw": ssd_norm_w[0].reshape(1, D_MODEL),
        "win": ssd_in_w.astype(BF16),
        "wdtT": w_dt.T[pair_order].astype(BF16),
        "cw": ssd_conv_w,
        "cb": ssd_conv_b[0].reshape(1, SSD_CONV_DIM),
        "dtb": ssd_dt_bias[0].reshape(1, SSD_N_HEADS),
        "dtbT": ssd_dt_bias[0][pair_order].reshape(SSD_N_HEADS, 1),
        "alog": ssd_A_log[0],
        "dexp": jnp.repeat(ssd_D[0], SSD_HEAD_DIM).reshape(1, SSD_D_INNER),
        "gnw": ssd_gnorm_w[0].reshape(1, SSD_D_INNER),
        "wo": ssd_out_w.astype(BF16),
    }
    hgrn_p = {
        "nw": hgrn_norm_w[0].reshape(1, D_MODEL),
        "win": hgrn_in_w.astype(BF16),
        "lb": hgrn_lower_bounds,
        "gnw": jnp.tile(hgrn_gnorm_w[0], HGRN_N_HEADS).reshape(1, D_MODEL),
        "wo": hgrn_out_w.astype(BF16),
    }
    ffn_p = {"nw": ffn_norm_w.reshape(2, 1, D_MODEL), "wg": ffn_w_gate.astype(BF16),
             "wu": ffn_w_up.astype(BF16), "wd": ffn_w_down.astype(BF16)}
    final_w = final_norm_w.reshape(1, D_MODEL)

    y_p, ssd_s_p, conv_p, hgrn_s_p = _trunk(
        x_prompt, None, None, None, ssd_p, hgrn_p, ffn_p, final_w, NB=1, TL=256, C=64, TM=512)
    y_s, ssd_s_s, conv_s, hgrn_s_s = _trunk(
        x_sample, state_ssd[0], cache_conv[0], state_hgrn[0], ssd_p, hgrn_p, ffn_p, final_w,
        NB=4, TL=32, C=32, TM=512)
    return (y_p, y_s, ssd_s_p, conv_p, hgrn_s_p, ssd_s_s, conv_s, hgrn_s_s)
```

```python
import functools

import jax
import jax.numpy as jnp
from jax import lax
from jax.experimental import pallas as pl
from jax.experimental.pallas import tpu as pltpu

F32 = jnp.float32
BF16 = jnp.bfloat16

D_MODEL = 1024
NORM_EPS = 1e-6

SSD_D_INNER = 2048
SSD_HEAD_DIM = 64
SSD_N_HEADS = 32
SSD_N_GROUPS = 4
SSD_D_STATE = 128
SSD_CONV_W = 4
SSD_CONV_DIM = 3072
SSD_NORM_GROUP = 512
SSD_N_PAIRS = SSD_N_HEADS // 2
SSD_XBC_OFF = SSD_D_INNER
SSD_DT_OFF = SSD_D_INNER + SSD_CONV_DIM

HGRN_HEAD_DIM = 128
HGRN_N_HEADS = 8

FFN_HIDDEN = 2816

LANES = 128
SUBLANES = 8
SCAN_CHUNK = 64
SSD_LONG_TILE = 512
HGRN_LONG_TILE = 256
SHORT_SEQ_PER_STEP = 4
FFN_ROWS = 512
CONV_COLS = 512
HGRN_COLS = 256
VMEM_LIMIT = 56 * 1024 * 1024


def _dot(a, b):
    return jnp.dot(a, b, preferred_element_type=F32)


def _dot_nt(a, b):
    return lax.dot_general(a, b, (((1,), (1,)), ((), ())), preferred_element_type=F32)


def _dot_tn(a, b):
    return lax.dot_general(a, b, (((0,), (0,)), ((), ())), preferred_element_type=F32)


def _split3(a):
    hi = a.astype(BF16)
    r1 = a - hi.astype(F32)
    mid = r1.astype(BF16)
    lo = (r1 - mid.astype(F32)).astype(BF16)
    return hi, mid, lo


def _stack3(a):
    hi = a.astype(BF16).astype(F32)
    r1 = a - hi
    mid = r1.astype(BF16).astype(F32)
    return jnp.concatenate([hi, mid, r1 - mid, jnp.zeros_like(a)], axis=1).astype(BF16)


def _rms(x, w):
    return x * lax.rsqrt(jnp.mean(x * x, axis=-1, keepdims=True) + NORM_EPS) * w


def _silu(x):
    return x * jax.nn.sigmoid(x)


def _shift_rows(u, first_rows):
    rows, cols = u.shape
    k = first_rows.shape[0]
    nv = rows // SUBLANES
    r = pltpu.roll(u.reshape(nv, SUBLANES, cols), k, axis=1)
    first = jnp.concatenate([first_rows, jnp.zeros((SUBLANES - k, cols), u.dtype)], axis=0)
    prev = jnp.concatenate([first[None], r[:nv - 1]], axis=0)
    sub = lax.broadcasted_iota(jnp.int32, (nv, SUBLANES, cols), 1)
    return jnp.where(sub < k, prev, r).reshape(rows, cols)


def _chunk_triangle(R, C):
    rr = lax.broadcasted_iota(jnp.int32, (R, R), 0)
    cc = lax.broadcasted_iota(jnp.int32, (R, R), 1)
    return ((cc <= rr) & ((rr & -C) == (cc & -C))).astype(BF16)


def _ssd_kernel(*refs, has_init, NB, TL, C):
    it = iter(refs)
    x_ref = next(it)
    s0_ref = cv0_ref = None
    if has_init:
        s0_ref = next(it)
        cv0_ref = next(it)
    (nw_ref, win_ref, cw_ref, cb_ref, dtb_ref, dtbT_ref,
     alog_ref, alogp_ref, dexp_ref, gnw_ref, wo_ref) = [next(it) for _ in range(11)]
    xo_ref = next(it)
    so_ref = next(it)
    cvo_ref = next(it)
    (st_scr, hist_scr, z_scr, xbc_scr, y_scr, e_scr, tri_scr,
     colc_scr, we_scr) = [next(it) for _ in range(9)]
    es_scr = colcs_scr = None
    if 2 * C != LANES:
        es_scr = next(it)
        colcs_scr = next(it)

    R = NB * TL
    n_ch = R // C
    ch_per_seq = TL // C
    W2 = 2 * C
    i = pl.program_id(1)
    n_i = pl.num_programs(1)

    @pl.when(i == 0)
    def _init():
        ek = lax.broadcasted_iota(jnp.int32, e_scr.shape, 0)
        el = lax.broadcasted_iota(jnp.int32, e_scr.shape, 1)
        e_scr[...] = ((ek < 3 * SSD_N_HEADS) & ((ek & (SSD_N_HEADS - 1)) == el // SSD_HEAD_DIM)).astype(BF16)
        if es_scr is not None:
            ek = lax.broadcasted_iota(jnp.int32, es_scr.shape, 0)
            el = lax.broadcasted_iota(jnp.int32, es_scr.shape, 1)
            es_scr[...] = ((ek < 3 * SSD_N_HEADS) & ((ek & (SSD_N_HEADS - 1)) == el // C)).astype(BF16)
        tri_scr[...] = _chunk_triangle(R, C)
        hist_scr[...] = jnp.zeros_like(hist_scr)
        if has_init:
            for nb in range(NB):
                for g in range(SSD_N_GROUPS):
                    st_scr[nb, g] = s0_ref[nb, g * SSD_NORM_GROUP:(g + 1) * SSD_NORM_GROUP, :].T
                for k in range(SSD_CONV_DIM // CONV_COLS):
                    hist_scr[nb, k, SUBLANES - 3:SUBLANES, :] = cv0_ref[nb, :, k * CONV_COLS:(k + 1) * CONV_COLS]
        else:
            st_scr[...] = jnp.zeros_like(st_scr)

    x = x_ref[...].reshape(R, D_MODEL)
    h16 = _rms(x, nw_ref[...]).astype(BF16)

    for cb0 in range(0, SSD_CONV_DIM, CONV_COLS):
        cols = slice(cb0, cb0 + CONV_COLS)
        xr = _dot(h16, win_ref[:, SSD_XBC_OFF + cb0:SSD_XBC_OFF + cb0 + CONV_COLS])
        w0, w1, w2, w3 = (cw_ref[k:k + 1, cols] for k in range(SSD_CONV_W))
        for nb in range(NB):
            xn = xr[nb * TL:(nb + 1) * TL]
            hm = hist_scr[nb, cb0 // CONV_COLS]
            sx = _shift_rows(xn, hm[7:8])
            a2 = _shift_rows(w1 * xn + w0 * sx, w1 * hm[6:8] + w0 * hm[5:7])
            xbc_scr[nb * TL:(nb + 1) * TL, cols] = _silu(w3 * xn + w2 * sx + a2 + cb_ref[:, cols])
            hist_scr[nb, cb0 // CONV_COLS] = xn[TL - SUBLANES:TL]
    for zb0 in range(0, SSD_D_INNER, CONV_COLS):
        z_scr[:, zb0:zb0 + CONV_COLS] = _silu(_dot(h16, win_ref[:, zb0:zb0 + CONV_COLS]))

    dt_raw = _dot(h16, win_ref[:, SSD_DT_OFF:])
    dt = jax.nn.softplus(dt_raw + dtb_ref[...])
    pi = lax.broadcasted_iota(jnp.int32, (SSD_N_HEADS, SSD_N_HEADS), 0)
    pj = lax.broadcasted_iota(jnp.int32, (SSD_N_HEADS, SSD_N_HEADS), 1)
    pick = (pj == jnp.where(pi < SSD_N_PAIRS, 2 * pi, 2 * (pi - SSD_N_PAIRS) + 1)).astype(BF16)
    dtT = jax.nn.softplus(sum(_dot_nt(pick, p) for p in _split3(dt_raw)) + dtbT_ref[...])
    dtTp = jnp.concatenate(
        [jnp.concatenate([dtT[0:SSD_N_PAIRS, c * C:(c + 1) * C],
                          dtT[SSD_N_PAIRS:, c * C:(c + 1) * C]], axis=1) for c in range(n_ch)],
        axis=0)
    a_row = -jnp.exp(alog_ref[...])
    a_pair = -jnp.exp(alogp_ref[...])
    cum = sum(_dot(tri_scr[...], p) for p in _split3(dt * a_row))
    last = jnp.concatenate(
        [jnp.broadcast_to(cum[(c + 1) * C - 1:(c + 1) * C, :], (C, SSD_N_HEADS)) for c in range(n_ch)], axis=0)
    wdt = jnp.exp(last - cum) * dt
    cum3 = _stack3(cum)
    colc_scr[...] = _dot(cum3, e_scr[...])
    we_scr[...] = _dot(_stack3(wdt), e_scr[...])
    if colcs_scr is not None:
        colcs_scr[...] = _dot(cum3, es_scr[...])
    r2 = lax.broadcasted_iota(jnp.int32, (W2, W2), 0)
    c2 = lax.broadcasted_iota(jnp.int32, (W2, W2), 1)
    triT2_16 = ((r2 <= c2) & ((r2 < C) == (c2 < C))).astype(BF16)
    aTp = dtTp * jnp.concatenate([a_pair] * n_ch, axis=0)
    cumTp = sum(_dot(p, triT2_16) for p in _split3(aTp))

    tp = lax.broadcasted_iota(jnp.int32, (C, W2), 0)
    sp = lax.broadcasted_iota(jnp.int32, (C, W2), 1)
    tril_pair = jnp.where(sp < C, sp, sp - C) <= tp
    left_e = lax.broadcasted_iota(jnp.int32, (C, LANES), 1) < SSD_HEAD_DIM

    for c in range(n_ch):
        nb = c // ch_per_seq
        rows = slice(c * C, (c + 1) * C)
        dtTp_c = dtTp[c * SSD_N_PAIRS:(c + 1) * SSD_N_PAIRS, :]
        cumTp_c = cumTp[c * SSD_N_PAIRS:(c + 1) * SSD_N_PAIRS, :]
        for g in range(SSD_N_GROUPS):
            bo = SSD_D_INNER + g * SSD_D_STATE
            co = SSD_D_INNER + SSD_N_GROUPS * SSD_D_STATE + g * SSD_D_STATE
            Bg16 = xbc_scr[rows, bo:bo + SSD_D_STATE].astype(BF16)
            Cg16 = xbc_scr[rows, co:co + SSD_D_STATE].astype(BF16)
            cbp = _dot_nt(Cg16, jnp.concatenate([Bg16, Bg16], axis=0))
            cbm = jnp.where(tril_pair, cbp, 0.0)
            st_g = st_scr[nb, g]
            y_inter = _dot(Cg16, st_g.astype(BF16))
            xw_parts = []
            el_parts = []
            for q in range(4):
                jp = g * 4 + q
                lanes = slice(jp * LANES, (jp + 1) * LANES)
                colc_e = colc_scr[rows, lanes]
                if W2 == LANES:
                    colc_l = colc_e
                else:
                    colc_l = colcs_scr[rows, jp * W2:(jp + 1) * W2]
                decay = jnp.exp(jnp.minimum(colc_l - cumTp_c[jp:jp + 1, :], 0.0))
                mp = cbm * decay * dtTp_c[jp:jp + 1, :]
                xs_p = xbc_scr[rows, lanes]
                rhs = jnp.concatenate(
                    [jnp.where(left_e, xs_p, 0.0).astype(BF16),
                     jnp.where(left_e, 0.0, xs_p).astype(BF16)], axis=0)
                e_c = jnp.exp(colc_e)
                y_scr[rows, lanes] = _dot(mp.astype(BF16), rhs) + e_c * y_inter[:, q * LANES:(q + 1) * LANES]
                xw_parts.append((xs_p * we_scr[rows, lanes]).astype(BF16))
                el_parts.append(e_c[C - 1:C, :])
            xw_g = jnp.concatenate(xw_parts, axis=1)
            el_g = jnp.concatenate(el_parts, axis=1)
            st_scr[nb, g] = el_g * st_g + _dot_tn(Bg16, xw_g)

    out = None
    for g in range(SSD_N_GROUPS):
        cols = slice(g * SSD_NORM_GROUP, (g + 1) * SSD_NORM_GROUP)
        yg = (y_scr[:, cols] + dexp_ref[:, cols] * xbc_scr[:, cols]) * z_scr[:, cols]
        yn = yg * lax.rsqrt(jnp.mean(yg * yg, axis=-1, keepdims=True) + NORM_EPS) * gnw_ref[:, cols]
        part = _dot(yn.astype(BF16), wo_ref[cols, :])
        out = part if out is None else out + part
    xo_ref[...] = (x + out).reshape(NB, TL, D_MODEL)

    @pl.when(i == n_i - 1)
    def _fin():
        for nb in range(NB):
            for g in range(SSD_N_GROUPS):
                so_ref[nb, g * SSD_NORM_GROUP:(g + 1) * SSD_NORM_GROUP, :] = st_scr[nb, g].T
            for k in range(SSD_CONV_DIM // CONV_COLS):
                cvo_ref[nb, :, k * CONV_COLS:(k + 1) * CONV_COLS] = hist_scr[nb, k, SUBLANES - 3:SUBLANES, :]


def _const_spec(shape):
    nd = len(shape)
    return pl.BlockSpec(shape, lambda *_: (0,) * nd, pipeline_mode=pl.Buffered(1))


def _layer_spec(arr, layer):
    nd = arr.ndim - 1
    return pl.BlockSpec((None,) + arr.shape[1:], lambda *_: (layer,) + (0,) * nd,
                        pipeline_mode=pl.Buffered(1))


def _ssd_mixer(x, s0, cv0, p, *, NB, TL, C):
    n_seq, L, _ = x.shape
    has_init = s0 is not None
    R = NB * TL
    grid = (n_seq // NB, L // TL)
    in_specs = [pl.BlockSpec((NB, TL, D_MODEL), lambda b, i: (b, i, 0))]
    args = [x]
    if has_init:
        in_specs += [pl.BlockSpec((NB, SSD_D_INNER, SSD_D_STATE), lambda b, i: (b, 0, 0)),
                     pl.BlockSpec((NB, 3, SSD_CONV_DIM), lambda b, i: (b, 0, 0))]
        args += [s0, cv0]
    alog_pair = jnp.concatenate(
        [jnp.broadcast_to(p["alog"][0::2, None], (SSD_N_PAIRS, C)),
         jnp.broadcast_to(p["alog"][1::2, None], (SSD_N_PAIRS, C))], axis=1)
    consts = [p["nw"], p["win"], p["cw"], p["cb"], p["dtb"], p["dtbT"],
              p["alog"].reshape(1, SSD_N_HEADS), alog_pair, p["dexp"], p["gnw"], p["wo"]]
    for c in consts:
        in_specs.append(_layer_spec(c, 0) if c.ndim == 3 else _const_spec(c.shape))
    args += consts
    out_shape = (jax.ShapeDtypeStruct((n_seq, L, D_MODEL), F32),
                 jax.ShapeDtypeStruct((n_seq, SSD_D_INNER, SSD_D_STATE), F32),
                 jax.ShapeDtypeStruct((n_seq, 3, SSD_CONV_DIM), F32))
    out_specs = (pl.BlockSpec((NB, TL, D_MODEL), lambda b, i: (b, i, 0)),
                 pl.BlockSpec((NB, SSD_D_INNER, SSD_D_STATE), lambda b, i: (b, 0, 0)),
                 pl.BlockSpec((NB, 3, SSD_CONV_DIM), lambda b, i: (b, 0, 0)))
    scratch = [pltpu.VMEM((NB, SSD_N_GROUPS, SSD_D_STATE, SSD_NORM_GROUP), F32),
               pltpu.VMEM((NB, SSD_CONV_DIM // CONV_COLS, SUBLANES, CONV_COLS), F32),
               pltpu.VMEM((R, SSD_D_INNER), F32),
               pltpu.VMEM((R, SSD_CONV_DIM), F32),
               pltpu.VMEM((R, SSD_D_INNER), F32),
               pltpu.VMEM((LANES, SSD_D_INNER), BF16),
               pltpu.VMEM((R, R), BF16),
               pltpu.VMEM((R, SSD_D_INNER), F32),
               pltpu.VMEM((R, SSD_D_INNER), F32)]
    if 2 * C != LANES:
        scratch += [pltpu.VMEM((LANES, SSD_N_PAIRS * 2 * C), BF16),
                    pltpu.VMEM((R, SSD_N_PAIRS * 2 * C), F32)]
    return pl.pallas_call(
        functools.partial(_ssd_kernel, has_init=has_init, NB=NB, TL=TL, C=C),
        grid=grid, in_specs=in_specs, out_specs=out_specs, out_shape=out_shape,
        scratch_shapes=scratch,
        compiler_params=pltpu.CompilerParams(
            dimension_semantics=("arbitrary", "arbitrary"), vmem_limit_bytes=VMEM_LIMIT),
        name="ssd_mixer_init" if has_init else "ssd_mixer_zero",
    )(*args)


def _hgrn_kernel(*refs, has_init, NB, TL, C):
    it = iter(refs)
    x_ref = next(it)
    s0_ref = next(it) if has_init else None
    nw_ref, win_ref, lb_ref, gnw_ref, wo_ref = [next(it) for _ in range(5)]
    xo_ref = next(it)
    so_ref = next(it)
    st_scr, tri_scr, qe_scr, ke_scr, kw_scr, v_scr, g_scr, el_scr, o_scr = [next(it) for _ in range(9)]

    R = NB * TL
    n_ch = R // C
    ch_per_seq = TL // C
    HD = HGRN_HEAD_DIM
    i = pl.program_id(1)
    n_i = pl.num_programs(1)

    @pl.when(i == 0)
    def _init():
        tri_scr[...] = _chunk_triangle(R, C)
        if has_init:
            for nb in range(NB):
                for hh in range(HGRN_N_HEADS):
                    st_scr[nb, hh] = s0_ref[nb, hh].T
        else:
            st_scr[...] = jnp.zeros_like(st_scr)

    x = x_ref[...].reshape(R, D_MODEL)
    h16 = _rms(x, nw_ref[...]).astype(BF16)
    lb_soft = jax.nn.softmax(lb_ref[...], axis=0)
    lb = (lb_soft[0:1, :] + lb_soft[1:2, :]) - lb_soft[0:1, :]

    for c0 in range(0, D_MODEL, HGRN_COLS):
        cols = slice(c0, c0 + HGRN_COLS)
        q = _silu(_dot(h16, win_ref[:, c0:c0 + HGRN_COLS]))
        f = _dot(h16, win_ref[:, D_MODEL + c0:D_MODEL + c0 + HGRN_COLS])
        v = _dot(h16, win_ref[:, 2 * D_MODEL + c0:2 * D_MODEL + c0 + HGRN_COLS])
        g = _dot(h16, win_ref[:, 3 * D_MODEL + c0:3 * D_MODEL + c0 + HGRN_COLS])
        lbc = lb[:, cols]
        forget = lbc + (1.0 - lbc) * jax.nn.sigmoid(f)
        k = 1.0 - forget
        b = sum(_dot(tri_scr[...], p) for p in _split3(jnp.log(forget)))
        ends = [b[(c + 1) * C - 1:(c + 1) * C, :] for c in range(n_ch)]
        last = jnp.concatenate([jnp.broadcast_to(e, (C, HGRN_COLS)) for e in ends], axis=0)
        qe_scr[:, cols] = (q * jnp.exp(b)).astype(BF16)
        ke_scr[:, cols] = (k * jnp.exp(-b)).astype(BF16)
        kw_scr[:, cols] = (k * jnp.exp(last - b)).astype(BF16)
        v_scr[:, cols] = v.astype(BF16)
        g_scr[:, cols] = _silu(g)
        el_scr[:, cols] = jnp.exp(jnp.concatenate(ends, axis=0))

    rr = lax.broadcasted_iota(jnp.int32, (C, C), 0)
    cc = lax.broadcasted_iota(jnp.int32, (C, C), 1)
    tril = cc <= rr

    for c in range(n_ch):
        nb = c // ch_per_seq
        rows = slice(c * C, (c + 1) * C)
        for hh in range(HGRN_N_HEADS):
            sl = slice(hh * HD, (hh + 1) * HD)
            qe, v = qe_scr[rows, sl], v_scr[rows, sl]
            sc = jnp.where(tril, _dot_nt(qe, ke_scr[rows, sl]), 0.0)
            s_h = st_scr[nb, hh]
            o_scr[rows, sl] = _dot(sc.astype(BF16), v) + _dot_nt(qe, s_h.astype(BF16))
            st_scr[nb, hh] = el_scr[c:c + 1, sl] * s_h + _dot_tn(v, kw_scr[rows, sl])

    out = None
    for c0 in range(0, D_MODEL, HGRN_COLS):
        parts = []
        for h0 in range(c0, c0 + HGRN_COLS, HD):
            blk = o_scr[:, h0:h0 + HD]
            parts.append(blk * lax.rsqrt(jnp.mean(blk * blk, axis=-1, keepdims=True) + NORM_EPS))
        cols = slice(c0, c0 + HGRN_COLS)
        on = jnp.concatenate(parts, axis=1) * gnw_ref[:, cols] * g_scr[:, cols]
        part = _dot(on.astype(BF16), wo_ref[cols, :])
        out = part if out is None else out + part
    xo_ref[...] = (x + out).reshape(NB, TL, D_MODEL)

    @pl.when(i == n_i - 1)
    def _fin():
        for nb in range(NB):
            for hh in range(HGRN_N_HEADS):
                so_ref[nb, hh] = st_scr[nb, hh].T


def _hgrn_mixer(x, s0, p, *, NB, TL, C):
    n_seq, L, _ = x.shape
    has_init = s0 is not None
    R = NB * TL
    grid = (n_seq // NB, L // TL)
    st_block = (NB, HGRN_N_HEADS, HGRN_HEAD_DIM, HGRN_HEAD_DIM)
    in_specs = [pl.BlockSpec((NB, TL, D_MODEL), lambda b, i: (b, i, 0))]
    args = [x]
    if has_init:
        in_specs.append(pl.BlockSpec(st_block, lambda b, i: (b, 0, 0, 0)))
        args.append(s0)
    consts = [p["nw"], p["win"], p["lb"], p["gnw"], p["wo"]]
    for c in consts:
        in_specs.append(_layer_spec(c, 0) if c.ndim == 3 else _const_spec(c.shape))
    args += consts
    out_shape = (jax.ShapeDtypeStruct((n_seq, L, D_MODEL), F32),
                 jax.ShapeDtypeStruct((n_seq,) + st_block[1:], F32))
    out_specs = (pl.BlockSpec((NB, TL, D_MODEL), lambda b, i: (b, i, 0)),
                 pl.BlockSpec(st_block, lambda b, i: (b, 0, 0, 0)))
    scratch = [pltpu.VMEM(st_block, F32),
               pltpu.VMEM((R, R), BF16),
               pltpu.VMEM((R, D_MODEL), BF16),
               pltpu.VMEM((R, D_MODEL), BF16),
               pltpu.VMEM((R, D_MODEL), BF16),
               pltpu.VMEM((R, D_MODEL), BF16),
               pltpu.VMEM((R, D_MODEL), F32),
               pltpu.VMEM((R // C, D_MODEL), F32),
               pltpu.VMEM((R, D_MODEL), F32)]
    return pl.pallas_call(
        functools.partial(_hgrn_kernel, has_init=has_init, NB=NB, TL=TL, C=C),
        grid=grid, in_specs=in_specs, out_specs=out_specs, out_shape=out_shape,
        scratch_shapes=scratch,
        compiler_params=pltpu.CompilerParams(
            dimension_semantics=("arbitrary", "arbitrary"), vmem_limit_bytes=VMEM_LIMIT),
        name="hgrn_mixer_init" if has_init else "hgrn_mixer_zero",
    )(*args)


def _ffn_kernel(*refs, final):
    if final:
        x_ref, nw_ref, wg_ref, wu_ref, wd_ref, fw_ref, o_ref = refs
    else:
        x_ref, nw_ref, wg_ref, wu_ref, wd_ref, o_ref = refs
    x = x_ref[...]
    h16 = _rms(x, nw_ref[...]).astype(BF16)
    act = _silu(_dot(h16, wg_ref[...])) * _dot(h16, wu_ref[...])
    y = x + _dot(act.astype(BF16), wd_ref[...])
    if final:
        y = _rms(y, fw_ref[...])
    o_ref[...] = y


def _ffn(x2d, p, layer, final_w, *, TM):
    rows = x2d.shape[0]
    final = final_w is not None
    in_specs = [pl.BlockSpec((TM, D_MODEL), lambda i: (i, 0)), _const_spec(p["nw"][layer].shape)]
    in_specs += [_layer_spec(p[k], layer) for k in ("wg", "wu", "wd")]
    args = [x2d, p["nw"][layer], p["wg"], p["wu"], p["wd"]]
    if final:
        in_specs.append(_const_spec(final_w.shape))
        args.append(final_w)
    return pl.pallas_call(
        functools.partial(_ffn_kernel, final=final),
        grid=(rows // TM,),
        in_specs=in_specs,
        out_specs=pl.BlockSpec((TM, D_MODEL), lambda i: (i, 0)),
        out_shape=jax.ShapeDtypeStruct((rows, D_MODEL), F32),
        compiler_params=pltpu.CompilerParams(
            dimension_semantics=("arbitrary",), vmem_limit_bytes=VMEM_LIMIT),
        name="swiglu_final" if final else "swiglu",
    )(*args)


def _tiles(n_seq, L):
    if L >= SSD_LONG_TILE:
        return dict(NB=1, TL_SSD=SSD_LONG_TILE, TL_HGRN=HGRN_LONG_TILE, C=SCAN_CHUNK, TM=FFN_ROWS)
    return dict(NB=SHORT_SEQ_PER_STEP, TL_SSD=L, TL_HGRN=L, C=min(L, SCAN_CHUNK), TM=min(FFN_ROWS, n_seq * L))


def _trunk(x, s_ssd, cv, s_hgrn, ssd_p, hgrn_p, ffn_p, final_w):
    n_seq, L, _ = x.shape
    t = _tiles(n_seq, L)
    s0 = None if s_ssd is None else s_ssd.reshape(n_seq, SSD_D_INNER, SSD_D_STATE)
    x, s_new, cv_new = _ssd_mixer(x, s0, cv, ssd_p, NB=t["NB"], TL=t["TL_SSD"], C=t["C"])
    x = _ffn(x.reshape(n_seq * L, D_MODEL), ffn_p, 0, None, TM=t["TM"]).reshape(n_seq, L, D_MODEL)
    x, h_new = _hgrn_mixer(x, s_hgrn, hgrn_p, NB=t["NB"], TL=t["TL_HGRN"], C=t["C"])
    y = _ffn(x.reshape(n_seq * L, D_MODEL), ffn_p, 1, final_w, TM=t["TM"]).reshape(n_seq, L, D_MODEL)
    s_new = s_new.reshape(1, n_seq, SSD_N_HEADS, SSD_HEAD_DIM, SSD_D_STATE)
    return y, s_new, cv_new[None], h_new[None]


def kernel(x_prompt, x_sample, state_ssd, cache_conv, state_hgrn, ssd_norm_w, ssd_in_w, ssd_conv_w, ssd_conv_b, ssd_dt_bias, ssd_A_log, ssd_D, ssd_gnorm_w, ssd_out_w, hgrn_norm_w, hgrn_in_w, hgrn_lower_bounds, hgrn_gnorm_w, hgrn_out_w, ffn_norm_w, ffn_w_gate, ffn_w_up, ffn_w_down, final_norm_w):
    pair_order = jnp.concatenate([jnp.arange(0, SSD_N_HEADS, 2), jnp.arange(1, SSD_N_HEADS, 2)])
    ssd_p = {
        "nw": ssd_norm_w[0].reshape(1, D_MODEL),
        "win": ssd_in_w.astype(BF16),
        "cw": ssd_conv_w,
        "cb": ssd_conv_b[0].reshape(1, SSD_CONV_DIM),
        "dtb": ssd_dt_bias[0].reshape(1, SSD_N_HEADS),
        "dtbT": ssd_dt_bias[0][pair_order].reshape(SSD_N_HEADS, 1),
        "alog": ssd_A_log[0],
        "dexp": jnp.repeat(ssd_D[0], SSD_HEAD_DIM).reshape(1, SSD_D_INNER),
        "gnw": ssd_gnorm_w[0].reshape(1, SSD_D_INNER),
        "wo": ssd_out_w.astype(BF16),
    }
    hgrn_p = {
        "nw": hgrn_norm_w[0].reshape(1, D_MODEL),
        "win": hgrn_in_w.astype(BF16),
        "lb": hgrn_lower_bounds,
        "gnw": jnp.tile(hgrn_gnorm_w[0], HGRN_N_HEADS).reshape(1, D_MODEL),
        "wo": hgrn_out_w.astype(BF16),
    }
    ffn_p = {"nw": ffn_norm_w.reshape(2, 1, D_MODEL), "wg": ffn_w_gate.astype(BF16),
             "wu": ffn_w_up.astype(BF16), "wd": ffn_w_down.astype(BF16)}
    final_w = final_norm_w.reshape(1, D_MODEL)

    y_p, ssd_s_p, conv_p, hgrn_s_p = _trunk(x_prompt, None, None, None, ssd_p, hgrn_p, ffn_p, final_w)
    y_s, ssd_s_s, conv_s, hgrn_s_s = _trunk(
        x_sample, state_ssd[0], cache_conv[0], state_hgrn[0], ssd_p, hgrn_p, ffn_p, final_w)
    return (y_p, y_s, ssd_s_p, conv_p, hgrn_s_p, ssd_s_s, conv_s, hgrn_s_s)
```

```python
import functools

import jax
import jax.numpy as jnp
from jax import lax
from jax.experimental import pallas as pl
from jax.experimental.pallas import tpu as pltpu

F32 = jnp.float32
BF16 = jnp.bfloat16

D_MODEL = 1024
NORM_EPS = 1e-6

SSD_D_INNER = 2048
SSD_HEAD_DIM = 64
SSD_N_HEADS = 32
SSD_N_GROUPS = 4
SSD_D_STATE = 128
SSD_CONV_W = 4
SSD_CONV_DIM = 3072
SSD_NORM_GROUP = 512
SSD_N_PAIRS = SSD_N_HEADS // 2
SSD_XBC_OFF = SSD_D_INNER
SSD_DT_OFF = SSD_D_INNER + SSD_CONV_DIM

HGRN_HEAD_DIM = 128
HGRN_N_HEADS = 8

FFN_HIDDEN = 2816

LANES = 128
SUBLANES = 8
SCAN_CHUNK = 64
SSD_LONG_TILE = 256
HGRN_LONG_TILE = 256
SHORT_SEQ_PER_STEP = 4
FFN_ROWS = 512
CONV_COLS = 512
HGRN_COLS = 256
VMEM_LIMIT = 56 * 1024 * 1024


def _dot(a, b):
    return jnp.dot(a, b, preferred_element_type=F32)


def _dot_nt(a, b):
    return lax.dot_general(a, b, (((1,), (1,)), ((), ())), preferred_element_type=F32)


def _dot_tn(a, b):
    return lax.dot_general(a, b, (((0,), (0,)), ((), ())), preferred_element_type=F32)


def _split3(a):
    hi = a.astype(BF16)
    r1 = a - hi.astype(F32)
    mid = r1.astype(BF16)
    lo = (r1 - mid.astype(F32)).astype(BF16)
    return hi, mid, lo


def _stack3(a):
    hi = a.astype(BF16).astype(F32)
    r1 = a - hi
    mid = r1.astype(BF16).astype(F32)
    return jnp.concatenate([hi, mid, r1 - mid, jnp.zeros_like(a)], axis=1).astype(BF16)


def _rms(x, w):
    return x * lax.rsqrt(jnp.mean(x * x, axis=-1, keepdims=True) + NORM_EPS) * w


def _silu(x):
    return x * jax.nn.sigmoid(x)


def _shift_rows(u, first_rows):
    rows, cols = u.shape
    k = first_rows.shape[0]
    nv = rows // SUBLANES
    r = pltpu.roll(u.reshape(nv, SUBLANES, cols), k, axis=1)
    first = jnp.concatenate([first_rows, jnp.zeros((SUBLANES - k, cols), u.dtype)], axis=0)
    prev = jnp.concatenate([first[None], r[:nv - 1]], axis=0)
    sub = lax.broadcasted_iota(jnp.int32, (nv, SUBLANES, cols), 1)
    return jnp.where(sub < k, prev, r).reshape(rows, cols)


def _chunk_cumsum(a, C):
    R, cols = a.shape
    nv = R // SUBLANES
    x = a.reshape(nv, SUBLANES, cols)
    sub = lax.broadcasted_iota(jnp.int32, (nv, SUBLANES, cols), 1)
    for k in (1, 2, 4):
        x = x + jnp.where(sub >= k, pltpu.roll(x, k, axis=1), 0.0)
    gpc = C // SUBLANES
    x = x.reshape(R // C, gpc, SUBLANES, cols)
    groups = [x[:, 0]]
    for v in range(1, gpc):
        groups.append(x[:, v] + groups[-1][:, SUBLANES - 1:SUBLANES, :])
    return jnp.stack(groups, axis=1).reshape(R, cols)


def _chunk_triangle(R, C):
    rr = lax.broadcasted_iota(jnp.int32, (R, R), 0)
    cc = lax.broadcasted_iota(jnp.int32, (R, R), 1)
    return ((cc <= rr) & ((rr & -C) == (cc & -C))).astype(BF16)


def _ssd_kernel(*refs, has_init, NB, TL, C):
    it = iter(refs)
    x_ref = next(it)
    s0_ref = cv0_ref = None
    if has_init:
        s0_ref = next(it)
        cv0_ref = next(it)
    (nw_ref, win_ref, cw_ref, cb_ref, dtb_ref, dtbT_ref,
     alog_ref, alogp_ref, dexp_ref, gnw_ref, wo_ref) = [next(it) for _ in range(11)]
    xo_ref = next(it)
    so_ref = next(it)
    cvo_ref = next(it)
    (st_scr, hist_scr, z_scr, xbc_scr, y_scr, e_scr, tri_scr,
     colc_scr, we_scr) = [next(it) for _ in range(9)]
    es_scr = colcs_scr = None
    if 2 * C != LANES:
        es_scr = next(it)
        colcs_scr = next(it)

    R = NB * TL
    n_ch = R // C
    ch_per_seq = TL // C
    W2 = 2 * C
    i = pl.program_id(1)
    n_i = pl.num_programs(1)

    @pl.when(i == 0)
    def _init():
        ek = lax.broadcasted_iota(jnp.int32, e_scr.shape, 0)
        el = lax.broadcasted_iota(jnp.int32, e_scr.shape, 1)
        e_scr[...] = ((ek < 3 * SSD_N_HEADS) & ((ek & (SSD_N_HEADS - 1)) == el // SSD_HEAD_DIM)).astype(BF16)
        if es_scr is not None:
            ek = lax.broadcasted_iota(jnp.int32, es_scr.shape, 0)
            el = lax.broadcasted_iota(jnp.int32, es_scr.shape, 1)
            es_scr[...] = ((ek < 3 * SSD_N_HEADS) & ((ek & (SSD_N_HEADS - 1)) == el // C)).astype(BF16)
        tri_scr[...] = _chunk_triangle(R, C)
        hist_scr[...] = jnp.zeros_like(hist_scr)
        if has_init:
            for nb in range(NB):
                for g in range(SSD_N_GROUPS):
                    st_scr[nb, g] = s0_ref[nb, g * SSD_NORM_GROUP:(g + 1) * SSD_NORM_GROUP, :].T
                for k in range(SSD_CONV_DIM // CONV_COLS):
                    hist_scr[nb, k, SUBLANES - 3:SUBLANES, :] = cv0_ref[nb, :, k * CONV_COLS:(k + 1) * CONV_COLS]
        else:
            st_scr[...] = jnp.zeros_like(st_scr)

    x = x_ref[...].reshape(R, D_MODEL)
    h16 = _rms(x, nw_ref[...]).astype(BF16)

    for cb0 in range(0, SSD_CONV_DIM, CONV_COLS):
        cols = slice(cb0, cb0 + CONV_COLS)
        xr = _dot(h16, win_ref[:, SSD_XBC_OFF + cb0:SSD_XBC_OFF + cb0 + CONV_COLS])
        w0, w1, w2, w3 = (cw_ref[k:k + 1, cols] for k in range(SSD_CONV_W))
        for nb in range(NB):
            xn = xr[nb * TL:(nb + 1) * TL]
            hm = hist_scr[nb, cb0 // CONV_COLS]
            sx = _shift_rows(xn, hm[7:8])
            a2 = _shift_rows(w1 * xn + w0 * sx, w1 * hm[6:8] + w0 * hm[5:7])
            xbc_scr[nb * TL:(nb + 1) * TL, cols] = _silu(w3 * xn + w2 * sx + a2 + cb_ref[:, cols])
            hist_scr[nb, cb0 // CONV_COLS] = xn[TL - SUBLANES:TL]
    for zb0 in range(0, SSD_D_INNER, CONV_COLS):
        z_scr[:, zb0:zb0 + CONV_COLS] = _silu(_dot(h16, win_ref[:, zb0:zb0 + CONV_COLS]))

    dt_raw = _dot(h16, win_ref[:, SSD_DT_OFF:])
    dt = jax.nn.softplus(dt_raw + dtb_ref[...])
    pi = lax.broadcasted_iota(jnp.int32, (SSD_N_HEADS, SSD_N_HEADS), 0)
    pj = lax.broadcasted_iota(jnp.int32, (SSD_N_HEADS, SSD_N_HEADS), 1)
    pick = (pj == jnp.where(pi < SSD_N_PAIRS, 2 * pi, 2 * (pi - SSD_N_PAIRS) + 1)).astype(BF16)
    dtT = jax.nn.softplus(sum(_dot_nt(pick, p) for p in _split3(dt_raw)) + dtbT_ref[...])
    dtTp = jnp.concatenate(
        [jnp.concatenate([dtT[0:SSD_N_PAIRS, c * C:(c + 1) * C],
                          dtT[SSD_N_PAIRS:, c * C:(c + 1) * C]], axis=1) for c in range(n_ch)],
        axis=0)
    a_row = -jnp.exp(alog_ref[...])
    a_pair = -jnp.exp(alogp_ref[...])
    cum = sum(_dot(tri_scr[...], p) for p in _split3(dt * a_row))
    last = jnp.concatenate(
        [jnp.broadcast_to(cum[(c + 1) * C - 1:(c + 1) * C, :], (C, SSD_N_HEADS)) for c in range(n_ch)], axis=0)
    wdt = jnp.exp(last - cum) * dt
    cum3 = _stack3(cum)
    colc_scr[...] = _dot(cum3, e_scr[...])
    we_scr[...] = _dot(_stack3(wdt), e_scr[...])
    if colcs_scr is not None:
        colcs_scr[...] = _dot(cum3, es_scr[...])
    r2 = lax.broadcasted_iota(jnp.int32, (W2, W2), 0)
    c2 = lax.broadcasted_iota(jnp.int32, (W2, W2), 1)
    triT2_16 = ((r2 <= c2) & ((r2 < C) == (c2 < C))).astype(BF16)
    aTp = dtTp * jnp.concatenate([a_pair] * n_ch, axis=0)
    cumTp = sum(_dot(p, triT2_16) for p in _split3(aTp))

    tp = lax.broadcasted_iota(jnp.int32, (C, W2), 0)
    sp = lax.broadcasted_iota(jnp.int32, (C, W2), 1)
    tril_pair = jnp.where(sp < C, sp, sp - C) <= tp
    left_e = lax.broadcasted_iota(jnp.int32, (C, LANES), 1) < SSD_HEAD_DIM

    for c in range(n_ch):
        nb = c // ch_per_seq
        rows = slice(c * C, (c + 1) * C)
        dtTp_c = dtTp[c * SSD_N_PAIRS:(c + 1) * SSD_N_PAIRS, :]
        cumTp_c = cumTp[c * SSD_N_PAIRS:(c + 1) * SSD_N_PAIRS, :]
        for g in range(SSD_N_GROUPS):
            bo = SSD_D_INNER + g * SSD_D_STATE
            co = SSD_D_INNER + SSD_N_GROUPS * SSD_D_STATE + g * SSD_D_STATE
            Bg16 = xbc_scr[rows, bo:bo + SSD_D_STATE].astype(BF16)
            Cg16 = xbc_scr[rows, co:co + SSD_D_STATE].astype(BF16)
            cbp = _dot_nt(Cg16, jnp.concatenate([Bg16, Bg16], axis=0))
            cbm = jnp.where(tril_pair, cbp, 0.0)
            st_g = st_scr[nb, g]
            y_inter = _dot(Cg16, st_g.astype(BF16))
            xw_parts = []
            el_parts = []
            for q in range(4):
                jp = g * 4 + q
                lanes = slice(jp * LANES, (jp + 1) * LANES)
                colc_e = colc_scr[rows, lanes]
                if W2 == LANES:
                    colc_l = colc_e
                else:
                    colc_l = colcs_scr[rows, jp * W2:(jp + 1) * W2]
                decay = jnp.exp(jnp.minimum(colc_l - cumTp_c[jp:jp + 1, :], 0.0))
                mp = cbm * decay * dtTp_c[jp:jp + 1, :]
                xs_p = xbc_scr[rows, lanes]
                rhs = jnp.concatenate(
                    [jnp.where(left_e, xs_p, 0.0).astype(BF16),
                     jnp.where(left_e, 0.0, xs_p).astype(BF16)], axis=0)
                e_c = jnp.exp(colc_e)
                y_scr[rows, lanes] = _dot(mp.astype(BF16), rhs) + e_c * y_inter[:, q * LANES:(q + 1) * LANES]
                xw_parts.append((xs_p * we_scr[rows, lanes]).astype(BF16))
                el_parts.append(e_c[C - 1:C, :])
            xw_g = jnp.concatenate(xw_parts, axis=1)
            el_g = jnp.concatenate(el_parts, axis=1)
            st_scr[nb, g] = el_g * st_g + _dot_tn(Bg16, xw_g)

    out = None
    for g in range(SSD_N_GROUPS):
        cols = slice(g * SSD_NORM_GROUP, (g + 1) * SSD_NORM_GROUP)
        yg = (y_scr[:, cols] + dexp_ref[:, cols] * xbc_scr[:, cols]) * z_scr[:, cols]
        yn = yg * lax.rsqrt(jnp.mean(yg * yg, axis=-1, keepdims=True) + NORM_EPS) * gnw_ref[:, cols]
        part = _dot(yn.astype(BF16), wo_ref[cols, :])
        out = part if out is None else out + part
    xo_ref[...] = (x + out).reshape(NB, TL, D_MODEL)

    @pl.when(i == n_i - 1)
    def _fin():
        for nb in range(NB):
            for g in range(SSD_N_GROUPS):
                so_ref[nb, g * SSD_NORM_GROUP:(g + 1) * SSD_NORM_GROUP, :] = st_scr[nb, g].T
            for k in range(SSD_CONV_DIM // CONV_COLS):
                cvo_ref[nb, :, k * CONV_COLS:(k + 1) * CONV_COLS] = hist_scr[nb, k, SUBLANES - 3:SUBLANES, :]


def _const_spec(shape):
    nd = len(shape)
    return pl.BlockSpec(shape, lambda *_: (0,) * nd, pipeline_mode=pl.Buffered(1))


def _layer_spec(arr, layer):
    nd = arr.ndim - 1
    return pl.BlockSpec((None,) + arr.shape[1:], lambda *_: (layer,) + (0,) * nd,
                        pipeline_mode=pl.Buffered(1))


def _ssd_mixer(x, s0, cv0, p, *, NB, TL, C):
    n_seq, L, _ = x.shape
    has_init = s0 is not None
    R = NB * TL
    grid = (n_seq // NB, L // TL)
    in_specs = [pl.BlockSpec((NB, TL, D_MODEL), lambda b, i: (b, i, 0))]
    args = [x]
    if has_init:
        in_specs += [pl.BlockSpec((NB, SSD_D_INNER, SSD_D_STATE), lambda b, i: (b, 0, 0)),
                     pl.BlockSpec((NB, 3, SSD_CONV_DIM), lambda b, i: (b, 0, 0))]
        args += [s0, cv0]
    alog_pair = jnp.concatenate(
        [jnp.broadcast_to(p["alog"][0::2, None], (SSD_N_PAIRS, C)),
         jnp.broadcast_to(p["alog"][1::2, None], (SSD_N_PAIRS, C))], axis=1)
    consts = [p["nw"], p["win"], p["cw"], p["cb"], p["dtb"], p["dtbT"],
              p["alog"].reshape(1, SSD_N_HEADS), alog_pair, p["dexp"], p["gnw"], p["wo"]]
    for c in consts:
        in_specs.append(_layer_spec(c, 0) if c.ndim == 3 else _const_spec(c.shape))
    args += consts
    out_shape = (jax.ShapeDtypeStruct((n_seq, L, D_MODEL), F32),
                 jax.ShapeDtypeStruct((n_seq, SSD_D_INNER, SSD_D_STATE), F32),
                 jax.ShapeDtypeStruct((n_seq, 3, SSD_CONV_DIM), F32))
    out_specs = (pl.BlockSpec((NB, TL, D_MODEL), lambda b, i: (b, i, 0)),
                 pl.BlockSpec((NB, SSD_D_INNER, SSD_D_STATE), lambda b, i: (b, 0, 0)),
                 pl.BlockSpec((NB, 3, SSD_CONV_DIM), lambda b, i: (b, 0, 0)))
    scratch = [pltpu.VMEM((NB, SSD_N_GROUPS, SSD_D_STATE, SSD_NORM_GROUP), F32),
               pltpu.VMEM((NB, SSD_CONV_DIM // CONV_COLS, SUBLANES, CONV_COLS), F32),
               pltpu.VMEM((R, SSD_D_INNER), F32),
               pltpu.VMEM((R, SSD_CONV_DIM), F32),
               pltpu.VMEM((R, SSD_D_INNER), F32),
               pltpu.VMEM((LANES, SSD_D_INNER), BF16),
               pltpu.VMEM((R, R), BF16),
               pltpu.VMEM((R, SSD_D_INNER), F32),
               pltpu.VMEM((R, SSD_D_INNER), F32)]
    if 2 * C != LANES:
        scratch += [pltpu.VMEM((LANES, SSD_N_PAIRS * 2 * C), BF16),
                    pltpu.VMEM((R, SSD_N_PAIRS * 2 * C), F32)]
    return pl.pallas_call(
        functools.partial(_ssd_kernel, has_init=has_init, NB=NB, TL=TL, C=C),
        grid=grid, in_specs=in_specs, out_specs=out_specs, out_shape=out_shape,
        scratch_shapes=scratch,
        compiler_params=pltpu.CompilerParams(
            dimension_semantics=("arbitrary", "arbitrary"), vmem_limit_bytes=VMEM_LIMIT),
        name="ssd_mixer_init" if has_init else "ssd_mixer_zero",
    )(*args)


def _hgrn_kernel(*refs, has_init, NB, TL, C):
    it = iter(refs)
    x_ref = next(it)
    s0_ref = next(it) if has_init else None
    nw_ref, win_ref, lb_ref, gnw_ref, wo_ref = [next(it) for _ in range(5)]
    xo_ref = next(it)
    so_ref = next(it)
    st_scr, qe_scr, ke_scr, kw_scr, v_scr, g_scr, el_scr, o_scr = [next(it) for _ in range(8)]

    R = NB * TL
    n_ch = R // C
    ch_per_seq = TL // C
    HD = HGRN_HEAD_DIM
    i = pl.program_id(1)
    n_i = pl.num_programs(1)

    @pl.when(i == 0)
    def _init():
        if has_init:
            for nb in range(NB):
                for hh in range(HGRN_N_HEADS):
                    st_scr[nb, hh] = s0_ref[nb, hh].T
        else:
            st_scr[...] = jnp.zeros_like(st_scr)

    x = x_ref[...].reshape(R, D_MODEL)
    h16 = _rms(x, nw_ref[...]).astype(BF16)
    lb_soft = jax.nn.softmax(lb_ref[...], axis=0)
    lb = (lb_soft[0:1, :] + lb_soft[1:2, :]) - lb_soft[0:1, :]

    for c0 in range(0, D_MODEL, HGRN_COLS):
        cols = slice(c0, c0 + HGRN_COLS)
        q = _silu(_dot(h16, win_ref[:, c0:c0 + HGRN_COLS]))
        f = _dot(h16, win_ref[:, D_MODEL + c0:D_MODEL + c0 + HGRN_COLS])
        v = _dot(h16, win_ref[:, 2 * D_MODEL + c0:2 * D_MODEL + c0 + HGRN_COLS])
        g = _dot(h16, win_ref[:, 3 * D_MODEL + c0:3 * D_MODEL + c0 + HGRN_COLS])
        lbc = lb[:, cols]
        forget = lbc + (1.0 - lbc) * jax.nn.sigmoid(f)
        k = 1.0 - forget
        b = _chunk_cumsum(jnp.log(forget), C)
        ends = [b[(c + 1) * C - 1:(c + 1) * C, :] for c in range(n_ch)]
        last = jnp.concatenate([jnp.broadcast_to(e, (C, HGRN_COLS)) for e in ends], axis=0)
        qe_scr[:, cols] = (q * jnp.exp(b)).astype(BF16)
        ke_scr[:, cols] = (k * jnp.exp(-b)).astype(BF16)
        kw_scr[:, cols] = (k * jnp.exp(last - b)).astype(BF16)
        v_scr[:, cols] = v.astype(BF16)
        g_scr[:, cols] = _silu(g)
        el_scr[:, cols] = jnp.exp(jnp.concatenate(ends, axis=0))

    rr = lax.broadcasted_iota(jnp.int32, (C, C), 0)
    cc = lax.broadcasted_iota(jnp.int32, (C, C), 1)
    tril = cc <= rr

    for c in range(n_ch):
        nb = c // ch_per_seq
        rows = slice(c * C, (c + 1) * C)
        for hh in range(HGRN_N_HEADS):
            sl = slice(hh * HD, (hh + 1) * HD)
            qe, v = qe_scr[rows, sl], v_scr[rows, sl]
            sc = jnp.where(tril, _dot_nt(qe, ke_scr[rows, sl]), 0.0)
            s_h = st_scr[nb, hh]
            o_scr[rows, sl] = _dot(sc.astype(BF16), v) + _dot_nt(qe, s_h.astype(BF16))
            st_scr[nb, hh] = el_scr[c:c + 1, sl] * s_h + _dot_tn(v, kw_scr[rows, sl])

    out = None
    for c0 in range(0, D_MODEL, HGRN_COLS):
        parts = []
        for h0 in range(c0, c0 + HGRN_COLS, HD):
            blk = o_scr[:, h0:h0 + HD]
            parts.append(blk * lax.rsqrt(jnp.mean(blk * blk, axis=-1, keepdims=True) + NORM_EPS))
        cols = slice(c0, c0 + HGRN_COLS)
        on = jnp.concatenate(parts, axis=1) * gnw_ref[:, cols] * g_scr[:, cols]
        part = _dot(on.astype(BF16), wo_ref[cols, :])
        out = part if out is None else out + part
    xo_ref[...] = (x + out).reshape(NB, TL, D_MODEL)

    @pl.when(i == n_i - 1)
    def _fin():
        for nb in range(NB):
            for hh in range(HGRN_N_HEADS):
                so_ref[nb, hh] = st_scr[nb, hh].T


def _hgrn_mixer(x, s0, p, *, NB, TL, C):
    n_seq, L, _ = x.shape
    has_init = s0 is not None
    R = NB * TL
    grid = (n_seq // NB, L // TL)
    st_block = (NB, HGRN_N_HEADS, HGRN_HEAD_DIM, HGRN_HEAD_DIM)
    in_specs = [pl.BlockSpec((NB, TL, D_MODEL), lambda b, i: (b, i, 0))]
    args = [x]
    if has_init:
        in_specs.append(pl.BlockSpec(st_block, lambda b, i: (b, 0, 0, 0)))
        args.append(s0)
    consts = [p["nw"], p["win"], p["lb"], p["gnw"], p["wo"]]
    for c in consts:
        in_specs.append(_layer_spec(c, 0) if c.ndim == 3 else _const_spec(c.shape))
    args += consts
    out_shape = (jax.ShapeDtypeStruct((n_seq, L, D_MODEL), F32),
                 jax.ShapeDtypeStruct((n_seq,) + st_block[1:], F32))
    out_specs = (pl.BlockSpec((NB, TL, D_MODEL), lambda b, i: (b, i, 0)),
                 pl.BlockSpec(st_block, lambda b, i: (b, 0, 0, 0)))
    scratch = [pltpu.VMEM(st_block, F32),
               pltpu.VMEM((R, D_MODEL), BF16),
               pltpu.VMEM((R, D_MODEL), BF16),
               pltpu.VMEM((R, D_MODEL), BF16),
               pltpu.VMEM((R, D_MODEL), BF16),
               pltpu.VMEM((R, D_MODEL), F32),
               pltpu.VMEM((R // C, D_MODEL), F32),
               pltpu.VMEM((R, D_MODEL), F32)]
    return pl.pallas_call(
        functools.partial(_hgrn_kernel, has_init=has_init, NB=NB, TL=TL, C=C),
        grid=grid, in_specs=in_specs, out_specs=out_specs, out_shape=out_shape,
        scratch_shapes=scratch,
        compiler_params=pltpu.CompilerParams(
            dimension_semantics=("arbitrary", "arbitrary"), vmem_limit_bytes=VMEM_LIMIT),
        name="hgrn_mixer_init" if has_init else "hgrn_mixer_zero",
    )(*args)


def _ffn_kernel(*refs, final, starts):
    n_grp = len(starts) - 1
    x_refs = refs[:n_grp]
    nw_ref, wg_ref, wu_ref, wd_ref = refs[n_grp:n_grp + 4]
    fw_ref = refs[n_grp + 4] if final else None
    o_refs = refs[len(refs) - n_grp:]
    i = pl.program_id(0)
    x = x_refs[0][...]
    for k in range(1, n_grp):
        x = jnp.where(i >= starts[k], x_refs[k][...], x)
    h16 = _rms(x, nw_ref[...]).astype(BF16)
    act = _silu(_dot(h16, wg_ref[...])) * _dot(h16, wu_ref[...])
    y = x + _dot(act.astype(BF16), wd_ref[...])
    if final:
        y = _rms(y, fw_ref[...])
    o_refs[n_grp - 1][...] = y
    for k in range(n_grp - 1):
        @pl.when((i >= starts[k]) & (i < starts[k + 1]))
        def _store(k=k):
            o_refs[k][...] = o_refs[n_grp - 1][...]


def _ffn(xs, p, layer, final_w, *, TM):
    final = final_w is not None
    starts = [0]
    for x2d in xs:
        starts.append(starts[-1] + x2d.shape[0] // TM)

    def group_spec(k):
        lo, n = starts[k], starts[k + 1] - starts[k]
        return pl.BlockSpec((TM, D_MODEL), lambda i: (jnp.clip(i - lo, 0, n - 1), 0))

    in_specs = [group_spec(k) for k in range(len(xs))] + [_const_spec(p["nw"][layer].shape)]
    in_specs += [_layer_spec(p[k], layer) for k in ("wg", "wu", "wd")]
    args = list(xs) + [p["nw"][layer], p["wg"], p["wu"], p["wd"]]
    if final:
        in_specs.append(_const_spec(final_w.shape))
        args.append(final_w)
    return pl.pallas_call(
        functools.partial(_ffn_kernel, final=final, starts=tuple(starts)),
        grid=(starts[-1],),
        in_specs=in_specs,
        out_specs=[group_spec(k) for k in range(len(xs))],
        out_shape=[jax.ShapeDtypeStruct(x2d.shape, F32) for x2d in xs],
        compiler_params=pltpu.CompilerParams(
            dimension_semantics=("arbitrary",), vmem_limit_bytes=VMEM_LIMIT),
        name="swiglu_final" if final else "swiglu",
    )(*args)


def _tiles(n_seq, L):
    if L >= SSD_LONG_TILE:
        return dict(NB=1, TL_SSD=SSD_LONG_TILE, TL_HGRN=HGRN_LONG_TILE, C=SCAN_CHUNK, TM=FFN_ROWS)
    return dict(NB=SHORT_SEQ_PER_STEP, TL_SSD=L, TL_HGRN=L, C=min(L, SCAN_CHUNK), TM=min(FFN_ROWS, n_seq * L))


def _trunk(groups, ssd_p, hgrn_p, ffn_p, final_w):
    shapes = [g[0].shape for g in groups]
    tiles = [_tiles(n_seq, L) for n_seq, L, _ in shapes]
    tm = min(t["TM"] for t in tiles)

    order = sorted(range(len(groups)), key=lambda k: shapes[k][0] * shapes[k][1])

    def ffn_all(xs, layer, fw):
        ys = _ffn([xs[k].reshape(-1, D_MODEL) for k in order], ffn_p, layer, fw, TM=tm)
        out = [None] * len(xs)
        for k, y in zip(order, ys):
            out[k] = y.reshape(shapes[k])
        return out

    xs, ssd_states, conv_caches, hgrn_states = [], [], [], []
    for (x, s_ssd, cv, _), t in zip(groups, tiles):
        n_seq = x.shape[0]
        s0 = None if s_ssd is None else s_ssd.reshape(n_seq, SSD_D_INNER, SSD_D_STATE)
        x, s_new, cv_new = _ssd_mixer(x, s0, cv, ssd_p, NB=t["NB"], TL=t["TL_SSD"], C=t["C"])
        xs.append(x)
        ssd_states.append(s_new.reshape(1, n_seq, SSD_N_HEADS, SSD_HEAD_DIM, SSD_D_STATE))
        conv_caches.append(cv_new[None])
    xs = ffn_all(xs, 0, None)
    for k, ((_, _, _, s_hgrn), t) in enumerate(zip(groups, tiles)):
        xs[k], h_new = _hgrn_mixer(xs[k], s_hgrn, hgrn_p, NB=t["NB"], TL=t["TL_HGRN"], C=t["C"])
        hgrn_states.append(h_new[None])
    ys = ffn_all(xs, 1, final_w)
    return ys, ssd_states, conv_caches, hgrn_states


def kernel(x_prompt, x_sample, state_ssd, cache_conv, state_hgrn, ssd_norm_w, ssd_in_w, ssd_conv_w, ssd_conv_b, ssd_dt_bias, ssd_A_log, ssd_D, ssd_gnorm_w, ssd_out_w, hgrn_norm_w, hgrn_in_w, hgrn_lower_bounds, hgrn_gnorm_w, hgrn_out_w, ffn_norm_w, ffn_w_gate, ffn_w_up, ffn_w_down, final_norm_w):
    pair_order = jnp.concatenate([jnp.arange(0, SSD_N_HEADS, 2), jnp.arange(1, SSD_N_HEADS, 2)])
    ssd_p = {
        "nw": ssd_norm_w[0].reshape(1, D_MODEL),
        "win": ssd_in_w.astype(BF16),
        "cw": ssd_conv_w,
        "cb": ssd_conv_b[0].reshape(1, SSD_CONV_DIM),
        "dtb": ssd_dt_bias[0].reshape(1, SSD_N_HEADS),
        "dtbT": ssd_dt_bias[0][pair_order].reshape(SSD_N_HEADS, 1),
        "alog": ssd_A_log[0],
        "dexp": jnp.repeat(ssd_D[0], SSD_HEAD_DIM).reshape(1, SSD_D_INNER),
        "gnw": ssd_gnorm_w[0].reshape(1, SSD_D_INNER),
        "wo": ssd_out_w.astype(BF16),
    }
    hgrn_p = {
        "nw": hgrn_norm_w[0].reshape(1, D_MODEL),
        "win": hgrn_in_w.astype(BF16),
        "lb": hgrn_lower_bounds,
        "gnw": jnp.tile(hgrn_gnorm_w[0], HGRN_N_HEADS).reshape(1, D_MODEL),
        "wo": hgrn_out_w.astype(BF16),
    }
    ffn_p = {"nw": ffn_norm_w.reshape(2, 1, D_MODEL), "wg": ffn_w_gate.astype(BF16),
             "wu": ffn_w_up.astype(BF16), "wd": ffn_w_down.astype(BF16)}
    final_w = final_norm_w.reshape(1, D_MODEL)

    groups = [(x_prompt, None, None, None),
              (x_sample, state_ssd[0], cache_conv[0], state_hgrn[0])]
    ys, ssd_s, conv_c, hgrn_s = _trunk(groups, ssd_p, hgrn_p, ffn_p, final_w)
    return (ys[0], ys[1], ssd_s[0], conv_c[0], hgrn_s[0], ssd_s[1], conv_c[1], hgrn_s[1])
```

```python
import functools

import jax
import jax.numpy as jnp
from jax import lax
from jax.experimental import pallas as pl
from jax.experimental.pallas import tpu as pltpu

F32 = jnp.float32
BF16 = jnp.bfloat16

D_MODEL = 1024
NORM_EPS = 1e-6

SSD_D_INNER = 2048
SSD_HEAD_DIM = 64
SSD_N_HEADS = 32
SSD_N_GROUPS = 4
SSD_D_STATE = 128
SSD_CONV_W = 4
SSD_CONV_DIM = 3072
SSD_NORM_GROUP = 512
SSD_N_PAIRS = SSD_N_HEADS // 2
SSD_XBC_OFF = SSD_D_INNER
SSD_DT_OFF = SSD_D_INNER + SSD_CONV_DIM

HGRN_HEAD_DIM = 128
HGRN_N_HEADS = 8

FFN_HIDDEN = 2816

LANES = 128
SUBLANES = 8
SCAN_CHUNK = 64
SSD_LONG_TILE = 256
HGRN_LONG_TILE = 256
SHORT_SEQ_PER_STEP = 4
FFN_ROWS = 512
CONV_COLS = 512
HGRN_COLS = 256
VMEM_LIMIT = 56 * 1024 * 1024


def _dot(a, b):
    return jnp.dot(a, b, preferred_element_type=F32)


def _dot_nt(a, b):
    return lax.dot_general(a, b, (((1,), (1,)), ((), ())), preferred_element_type=F32)


def _dot_tn(a, b):
    return lax.dot_general(a, b, (((0,), (0,)), ((), ())), preferred_element_type=F32)


def _split3(a):
    hi = a.astype(BF16)
    r1 = a - hi.astype(F32)
    mid = r1.astype(BF16)
    lo = (r1 - mid.astype(F32)).astype(BF16)
    return hi, mid, lo


def _stack3(a):
    hi = a.astype(BF16).astype(F32)
    r1 = a - hi
    mid = r1.astype(BF16).astype(F32)
    return jnp.concatenate([hi, mid, r1 - mid, jnp.zeros_like(a)], axis=1).astype(BF16)


def _rms(x, w):
    return x * lax.rsqrt(jnp.mean(x * x, axis=-1, keepdims=True) + NORM_EPS) * w


def _silu(x):
    return x * jax.nn.sigmoid(x)


def _shift_rows(u, first_rows):
    rows, cols = u.shape
    k = first_rows.shape[0]
    nv = rows // SUBLANES
    r = pltpu.roll(u.reshape(nv, SUBLANES, cols), k, axis=1)
    first = jnp.concatenate([first_rows, jnp.zeros((SUBLANES - k, cols), u.dtype)], axis=0)
    prev = jnp.concatenate([first[None], r[:nv - 1]], axis=0)
    sub = lax.broadcasted_iota(jnp.int32, (nv, SUBLANES, cols), 1)
    return jnp.where(sub < k, prev, r).reshape(rows, cols)


def _chunk_triangle(R, C):
    rr = lax.broadcasted_iota(jnp.int32, (R, R), 0)
    cc = lax.broadcasted_iota(jnp.int32, (R, R), 1)
    return ((cc <= rr) & ((rr & -C) == (cc & -C))).astype(BF16)


def _ssd_kernel(*refs, has_init, NB, TL, C):
    it = iter(refs)
    x_ref = next(it)
    s0_ref = cv0_ref = None
    if has_init:
        s0_ref = next(it)
        cv0_ref = next(it)
    (nw_ref, win_ref, cw_ref, cb_ref, dtb_ref, dtbT_ref,
     alog_ref, alogp_ref, dexp_ref, gnw_ref, wo_ref) = [next(it) for _ in range(11)]
    xo_ref = next(it)
    so_ref = next(it)
    cvo_ref = next(it)
    (st_scr, hist_scr, z_scr, xbc_scr, y_scr, e_scr, tri_scr,
     colc_scr, we_scr) = [next(it) for _ in range(9)]
    es_scr = colcs_scr = None
    if 2 * C != LANES:
        es_scr = next(it)
        colcs_scr = next(it)

    R = NB * TL
    n_ch = R // C
    ch_per_seq = TL // C
    W2 = 2 * C
    i = pl.program_id(1)
    n_i = pl.num_programs(1)

    @pl.when(i == 0)
    def _init():
        ek = lax.broadcasted_iota(jnp.int32, e_scr.shape, 0)
        el = lax.broadcasted_iota(jnp.int32, e_scr.shape, 1)
        e_scr[...] = ((ek < 3 * SSD_N_HEADS) & ((ek & (SSD_N_HEADS - 1)) == el // SSD_HEAD_DIM)).astype(BF16)
        if es_scr is not None:
            ek = lax.broadcasted_iota(jnp.int32, es_scr.shape, 0)
            el = lax.broadcasted_iota(jnp.int32, es_scr.shape, 1)
            es_scr[...] = ((ek < 3 * SSD_N_HEADS) & ((ek & (SSD_N_HEADS - 1)) == el // C)).astype(BF16)
        tri_scr[...] = _chunk_triangle(R, C)
        hist_scr[...] = jnp.zeros_like(hist_scr)
        if has_init:
            for nb in range(NB):
                for g in range(SSD_N_GROUPS):
                    st_scr[nb, g] = s0_ref[nb, g * SSD_NORM_GROUP:(g + 1) * SSD_NORM_GROUP, :].T
                for k in range(SSD_CONV_DIM // CONV_COLS):
                    hist_scr[nb, k, SUBLANES - 3:SUBLANES, :] = cv0_ref[nb, :, k * CONV_COLS:(k + 1) * CONV_COLS]
        else:
            st_scr[...] = jnp.zeros_like(st_scr)

    x = x_ref[...].reshape(R, D_MODEL)
    h16 = _rms(x, nw_ref[...]).astype(BF16)

    for cb0 in range(0, SSD_CONV_DIM, CONV_COLS):
        cols = slice(cb0, cb0 + CONV_COLS)
        xr = _dot(h16, win_ref[:, SSD_XBC_OFF + cb0:SSD_XBC_OFF + cb0 + CONV_COLS])
        w0, w1, w2, w3 = (cw_ref[k:k + 1, cols] for k in range(SSD_CONV_W))
        for nb in range(NB):
            xn = xr[nb * TL:(nb + 1) * TL]
            hm = hist_scr[nb, cb0 // CONV_COLS]
            sx = _shift_rows(xn, hm[7:8])
            a2 = _shift_rows(w1 * xn + w0 * sx, w1 * hm[6:8] + w0 * hm[5:7])
            xbc_scr[nb * TL:(nb + 1) * TL, cols] = _silu(w3 * xn + w2 * sx + a2 + cb_ref[:, cols])
            hist_scr[nb, cb0 // CONV_COLS] = xn[TL - SUBLANES:TL]
    for zb0 in range(0, SSD_D_INNER, CONV_COLS):
        z_scr[:, zb0:zb0 + CONV_COLS] = _silu(_dot(h16, win_ref[:, zb0:zb0 + CONV_COLS]))

    dt_raw = _dot(h16, win_ref[:, SSD_DT_OFF:])
    dt = jax.nn.softplus(dt_raw + dtb_ref[...])
    pi = lax.broadcasted_iota(jnp.int32, (SSD_N_HEADS, SSD_N_HEADS), 0)
    pj = lax.broadcasted_iota(jnp.int32, (SSD_N_HEADS, SSD_N_HEADS), 1)
    pick = (pj == jnp.where(pi < SSD_N_PAIRS, 2 * pi, 2 * (pi - SSD_N_PAIRS) + 1)).astype(BF16)
    dtT = jax.nn.softplus(sum(_dot_nt(pick, p) for p in _split3(dt_raw)) + dtbT_ref[...])
    dtTp = jnp.concatenate(
        [jnp.concatenate([dtT[0:SSD_N_PAIRS, c * C:(c + 1) * C],
                          dtT[SSD_N_PAIRS:, c * C:(c + 1) * C]], axis=1) for c in range(n_ch)],
        axis=0)
    a_row = -jnp.exp(alog_ref[...])
    a_pair = -jnp.exp(alogp_ref[...])
    cum = sum(_dot(tri_scr[...], p) for p in _split3(dt * a_row))
    last = jnp.concatenate(
        [jnp.broadcast_to(cum[(c + 1) * C - 1:(c + 1) * C, :], (C, SSD_N_HEADS)) for c in range(n_ch)], axis=0)
    wdt = jnp.exp(last - cum) * dt
    cum3 = _stack3(cum)
    colc_scr[...] = _dot(cum3, e_scr[...])
    we_scr[...] = _dot(_stack3(wdt), e_scr[...])
    if colcs_scr is not None:
        colcs_scr[...] = _dot(cum3, es_scr[...])
    r2 = lax.broadcasted_iota(jnp.int32, (W2, W2), 0)
    c2 = lax.broadcasted_iota(jnp.int32, (W2, W2), 1)
    triT2_16 = ((r2 <= c2) & ((r2 < C) == (c2 < C))).astype(BF16)
    aTp = dtTp * jnp.concatenate([a_pair] * n_ch, axis=0)
    cumTp = sum(_dot(p, triT2_16) for p in _split3(aTp))

    tp = lax.broadcasted_iota(jnp.int32, (C, W2), 0)
    sp = lax.broadcasted_iota(jnp.int32, (C, W2), 1)
    tril_pair = jnp.where(sp < C, sp, sp - C) <= tp
    left_e = lax.broadcasted_iota(jnp.int32, (C, LANES), 1) < SSD_HEAD_DIM

    for c in range(n_ch):
        nb = c // ch_per_seq
        rows = slice(c * C, (c + 1) * C)
        dtTp_c = dtTp[c * SSD_N_PAIRS:(c + 1) * SSD_N_PAIRS, :]
        cumTp_c = cumTp[c * SSD_N_PAIRS:(c + 1) * SSD_N_PAIRS, :]
        for g in range(SSD_N_GROUPS):
            bo = SSD_D_INNER + g * SSD_D_STATE
            co = SSD_D_INNER + SSD_N_GROUPS * SSD_D_STATE + g * SSD_D_STATE
            Bg16 = xbc_scr[rows, bo:bo + SSD_D_STATE].astype(BF16)
            Cg16 = xbc_scr[rows, co:co + SSD_D_STATE].astype(BF16)
            cbp = _dot_nt(Cg16, jnp.concatenate([Bg16, Bg16], axis=0))
            cbm = jnp.where(tril_pair, cbp, 0.0)
            st_g = st_scr[nb, g]
            y_inter = _dot(Cg16, st_g.astype(BF16))
            xw_parts = []
            el_parts = []
            for q in range(4):
                jp = g * 4 + q
                lanes = slice(jp * LANES, (jp + 1) * LANES)
                colc_e = colc_scr[rows, lanes]
                if W2 == LANES:
                    colc_l = colc_e
                else:
                    colc_l = colcs_scr[rows, jp * W2:(jp + 1) * W2]
                decay = jnp.exp(jnp.minimum(colc_l - cumTp_c[jp:jp + 1, :], 0.0))
                mp = cbm * decay * dtTp_c[jp:jp + 1, :]
                xs_p = xbc_scr[rows, lanes]
                rhs = jnp.concatenate(
                    [jnp.where(left_e, xs_p, 0.0).astype(BF16),
                     jnp.where(left_e, 0.0, xs_p).astype(BF16)], axis=0)
                e_c = jnp.exp(colc_e)
                y_scr[rows, lanes] = _dot(mp.astype(BF16), rhs) + e_c * y_inter[:, q * LANES:(q + 1) * LANES]
                xw_parts.append((xs_p * we_scr[rows, lanes]).astype(BF16))
                el_parts.append(e_c[C - 1:C, :])
            xw_g = jnp.concatenate(xw_parts, axis=1)
            el_g = jnp.concatenate(el_parts, axis=1)
            st_scr[nb, g] = el_g * st_g + _dot_tn(Bg16, xw_g)

    out = None
    for g in range(SSD_N_GROUPS):
        cols = slice(g * SSD_NORM_GROUP, (g + 1) * SSD_NORM_GROUP)
        yg = (y_scr[:, cols] + dexp_ref[:, cols] * xbc_scr[:, cols]) * z_scr[:, cols]
        yn = yg * lax.rsqrt(jnp.mean(yg * yg, axis=-1, keepdims=True) + NORM_EPS) * gnw_ref[:, cols]
        part = _dot(yn.astype(BF16), wo_ref[cols, :])
        out = part if out is None else out + part
    xo_ref[...] = (x + out).reshape(NB, TL, D_MODEL)

    @pl.when(i == n_i - 1)
    def _fin():
        for nb in range(NB):
            for g in range(SSD_N_GROUPS):
                so_ref[nb, g * SSD_NORM_GROUP:(g + 1) * SSD_NORM_GROUP, :] = st_scr[nb, g].T
            for k in range(SSD_CONV_DIM // CONV_COLS):
                cvo_ref[nb, :, k * CONV_COLS:(k + 1) * CONV_COLS] = hist_scr[nb, k, SUBLANES - 3:SUBLANES, :]


def _const_spec(shape):
    nd = len(shape)
    return pl.BlockSpec(shape, lambda *_: (0,) * nd, pipeline_mode=pl.Buffered(1))


def _layer_spec(arr, layer):
    nd = arr.ndim - 1
    return pl.BlockSpec((None,) + arr.shape[1:], lambda *_: (layer,) + (0,) * nd,
                        pipeline_mode=pl.Buffered(1))


def _ssd_mixer(x, s0, cv0, p, *, NB, TL, C):
    n_seq, L, _ = x.shape
    has_init = s0 is not None
    R = NB * TL
    grid = (n_seq // NB, L // TL)
    in_specs = [pl.BlockSpec((NB, TL, D_MODEL), lambda b, i: (b, i, 0))]
    args = [x]
    if has_init:
        in_specs += [pl.BlockSpec((NB, SSD_D_INNER, SSD_D_STATE), lambda b, i: (b, 0, 0)),
                     pl.BlockSpec((NB, 3, SSD_CONV_DIM), lambda b, i: (b, 0, 0))]
        args += [s0, cv0]
    alog_pair = jnp.concatenate(
        [jnp.broadcast_to(p["alog"][0::2, None], (SSD_N_PAIRS, C)),
         jnp.broadcast_to(p["alog"][1::2, None], (SSD_N_PAIRS, C))], axis=1)
    consts = [p["nw"], p["win"], p["cw"], p["cb"], p["dtb"], p["dtbT"],
              p["alog"].reshape(1, SSD_N_HEADS), alog_pair, p["dexp"], p["gnw"], p["wo"]]
    for c in consts:
        in_specs.append(_layer_spec(c, 0) if c.ndim == 3 else _const_spec(c.shape))
    args += consts
    out_shape = (jax.ShapeDtypeStruct((n_seq, L, D_MODEL), F32),
                 jax.ShapeDtypeStruct((n_seq, SSD_D_INNER, SSD_D_STATE), F32),
                 jax.ShapeDtypeStruct((n_seq, 3, SSD_CONV_DIM), F32))
    out_specs = (pl.BlockSpec((NB, TL, D_MODEL), lambda b, i: (b, i, 0)),
                 pl.BlockSpec((NB, SSD_D_INNER, SSD_D_STATE), lambda b, i: (b, 0, 0)),
                 pl.BlockSpec((NB, 3, SSD_CONV_DIM), lambda b, i: (b, 0, 0)))
    scratch = [pltpu.VMEM((NB, SSD_N_GROUPS, SSD_D_STATE, SSD_NORM_GROUP), F32),
               pltpu.VMEM((NB, SSD_CONV_DIM // CONV_COLS, SUBLANES, CONV_COLS), F32),
               pltpu.VMEM((R, SSD_D_INNER), F32),
               pltpu.VMEM((R, SSD_CONV_DIM), F32),
               pltpu.VMEM((R, SSD_D_INNER), F32),
               pltpu.VMEM((LANES, SSD_D_INNER), BF16),
               pltpu.VMEM((R, R), BF16),
               pltpu.VMEM((R, SSD_D_INNER), F32),
               pltpu.VMEM((R, SSD_D_INNER), F32)]
    if 2 * C != LANES:
        scratch += [pltpu.VMEM((LANES, SSD_N_PAIRS * 2 * C), BF16),
                    pltpu.VMEM((R, SSD_N_PAIRS * 2 * C), F32)]
    return pl.pallas_call(
        functools.partial(_ssd_kernel, has_init=has_init, NB=NB, TL=TL, C=C),
        grid=grid, in_specs=in_specs, out_specs=out_specs, out_shape=out_shape,
        scratch_shapes=scratch,
        compiler_params=pltpu.CompilerParams(
            dimension_semantics=("arbitrary", "arbitrary"), vmem_limit_bytes=VMEM_LIMIT),
        name="ssd_mixer_init" if has_init else "ssd_mixer_zero",
    )(*args)


def _hgrn_kernel(*refs, has_init, NB, TL, C):
    it = iter(refs)
    x_ref = next(it)
    s0_ref = next(it) if has_init else None
    nw_ref, win_ref, lb_ref, gnw_ref, wo_ref = [next(it) for _ in range(5)]
    xo_ref = next(it)
    so_ref = next(it)
    st_scr, tri_scr, qe_scr, ke_scr, kw_scr, v_scr, g_scr, el_scr, o_scr = [next(it) for _ in range(9)]

    R = NB * TL
    n_ch = R // C
    ch_per_seq = TL // C
    HD = HGRN_HEAD_DIM
    i = pl.program_id(1)
    n_i = pl.num_programs(1)

    @pl.when(i == 0)
    def _init():
        tri_scr[...] = _chunk_triangle(R, C)
        if has_init:
            for nb in range(NB):
                for hh in range(HGRN_N_HEADS):
                    st_scr[nb, hh] = s0_ref[nb, hh].T
        else:
            st_scr[...] = jnp.zeros_like(st_scr)

    x = x_ref[...].reshape(R, D_MODEL)
    h16 = _rms(x, nw_ref[...]).astype(BF16)
    lb_soft = jax.nn.softmax(lb_ref[...], axis=0)
    lb = (lb_soft[0:1, :] + lb_soft[1:2, :]) - lb_soft[0:1, :]

    for c0 in range(0, D_MODEL, HGRN_COLS):
        cols = slice(c0, c0 + HGRN_COLS)
        q = _silu(_dot(h16, win_ref[:, c0:c0 + HGRN_COLS]))
        f = _dot(h16, win_ref[:, D_MODEL + c0:D_MODEL + c0 + HGRN_COLS])
        v = _dot(h16, win_ref[:, 2 * D_MODEL + c0:2 * D_MODEL + c0 + HGRN_COLS])
        g = _dot(h16, win_ref[:, 3 * D_MODEL + c0:3 * D_MODEL + c0 + HGRN_COLS])
        lbc = lb[:, cols]
        forget = lbc + (1.0 - lbc) * jax.nn.sigmoid(f)
        k = 1.0 - forget
        b = sum(_dot(tri_scr[...], p) for p in _split3(jnp.log(forget)))
        ends = [b[(c + 1) * C - 1:(c + 1) * C, :] for c in range(n_ch)]
        last = jnp.concatenate([jnp.broadcast_to(e, (C, HGRN_COLS)) for e in ends], axis=0)
        qe_scr[:, cols] = (q * jnp.exp(b)).astype(BF16)
        ke_scr[:, cols] = (k * jnp.exp(-b)).astype(BF16)
        kw_scr[:, cols] = (k * jnp.exp(last - b)).astype(BF16)
        v_scr[:, cols] = v.astype(BF16)
        g_scr[:, cols] = _silu(g)
        el_scr[:, cols] = jnp.exp(jnp.concatenate(ends, axis=0))

    rr = lax.broadcasted_iota(jnp.int32, (C, C), 0)
    cc = lax.broadcasted_iota(jnp.int32, (C, C), 1)
    tril = cc <= rr

    for c in range(n_ch):
        nb = c // ch_per_seq
        rows = slice(c * C, (c + 1) * C)
        for hh in range(HGRN_N_HEADS):
            sl = slice(hh * HD, (hh + 1) * HD)
            qe, v = qe_scr[rows, sl], v_scr[rows, sl]
            sc = jnp.where(tril, _dot_nt(qe, ke_scr[rows, sl]), 0.0)
            s_h = st_scr[nb, hh]
            o_scr[rows, sl] = _dot(sc.astype(BF16), v) + _dot_nt(qe, s_h.astype(BF16))
            st_scr[nb, hh] = el_scr[c:c + 1, sl] * s_h + _dot_tn(v, kw_scr[rows, sl])

    out = None
    for c0 in range(0, D_MODEL, HGRN_COLS):
        parts = []
        for h0 in range(c0, c0 + HGRN_COLS, HD):
            blk = o_scr[:, h0:h0 + HD]
            parts.append(blk * lax.rsqrt(jnp.mean(blk * blk, axis=-1, keepdims=True) + NORM_EPS))
        cols = slice(c0, c0 + HGRN_COLS)
        on = jnp.concatenate(parts, axis=1) * gnw_ref[:, cols] * g_scr[:, cols]
        part = _dot(on.astype(BF16), wo_ref[cols, :])
        out = part if out is None else out + part
    xo_ref[...] = (x + out).reshape(NB, TL, D_MODEL)

    @pl.when(i == n_i - 1)
    def _fin():
        for nb in range(NB):
            for hh in range(HGRN_N_HEADS):
                so_ref[nb, hh] = st_scr[nb, hh].T


def _hgrn_mixer(x, s0, p, *, NB, TL, C):
    n_seq, L, _ = x.shape
    has_init = s0 is not None
    R = NB * TL
    grid = (n_seq // NB, L // TL)
    st_block = (NB, HGRN_N_HEADS, HGRN_HEAD_DIM, HGRN_HEAD_DIM)
    in_specs = [pl.BlockSpec((NB, TL, D_MODEL), lambda b, i: (b, i, 0))]
    args = [x]
    if has_init:
        in_specs.append(pl.BlockSpec(st_block, lambda b, i: (b, 0, 0, 0)))
        args.append(s0)
    consts = [p["nw"], p["win"], p["lb"], p["gnw"], p["wo"]]
    for c in consts:
        in_specs.append(_layer_spec(c, 0) if c.ndim == 3 else _const_spec(c.shape))
    args += consts
    out_shape = (jax.ShapeDtypeStruct((n_seq, L, D_MODEL), F32),
                 jax.ShapeDtypeStruct((n_seq,) + st_block[1:], F32))
    out_specs = (pl.BlockSpec((NB, TL, D_MODEL), lambda b, i: (b, i, 0)),
                 pl.BlockSpec(st_block, lambda b, i: (b, 0, 0, 0)))
    scratch = [pltpu.VMEM(st_block, F32),
               pltpu.VMEM((R, R), BF16),
               pltpu.VMEM((R, D_MODEL), BF16),
               pltpu.VMEM((R, D_MODEL), BF16),
               pltpu.VMEM((R, D_MODEL), BF16),
               pltpu.VMEM((R, D_MODEL), BF16),
               pltpu.VMEM((R, D_MODEL), F32),
               pltpu.VMEM((R // C, D_MODEL), F32),
               pltpu.VMEM((R, D_MODEL), F32)]
    return pl.pallas_call(
        functools.partial(_hgrn_kernel, has_init=has_init, NB=NB, TL=TL, C=C),
        grid=grid, in_specs=in_specs, out_specs=out_specs, out_shape=out_shape,
        scratch_shapes=scratch,
        compiler_params=pltpu.CompilerParams(
            dimension_semantics=("arbitrary", "arbitrary"), vmem_limit_bytes=VMEM_LIMIT),
        name="hgrn_mixer_init" if has_init else "hgrn_mixer_zero",
    )(*args)


def _ffn_kernel(*refs, final):
    if final:
        x_ref, nw_ref, wg_ref, wu_ref, wd_ref, fw_ref, o_ref = refs
    else:
        x_ref, nw_ref, wg_ref, wu_ref, wd_ref, o_ref = refs
    x = x_ref[...]
    h16 = _rms(x, nw_ref[...]).astype(BF16)
    act = _silu(_dot(h16, wg_ref[...])) * _dot(h16, wu_ref[...])
    y = x + _dot(act.astype(BF16), wd_ref[...])
    if final:
        y = _rms(y, fw_ref[...])
    o_ref[...] = y


def _ffn(x2d, p, layer, final_w, *, TM):
    rows = x2d.shape[0]
    final = final_w is not None
    in_specs = [pl.BlockSpec((TM, D_MODEL), lambda i: (i, 0)), _const_spec(p["nw"][layer].shape)]
    in_specs += [_layer_spec(p[k], layer) for k in ("wg", "wu", "wd")]
    args = [x2d, p["nw"][layer], p["wg"], p["wu"], p["wd"]]
    if final:
        in_specs.append(_const_spec(final_w.shape))
        args.append(final_w)
    return pl.pallas_call(
        functools.partial(_ffn_kernel, final=final),
        grid=(rows // TM,),
        in_specs=in_specs,
        out_specs=pl.BlockSpec((TM, D_MODEL), lambda i: (i, 0)),
        out_shape=jax.ShapeDtypeStruct((rows, D_MODEL), F32),
        compiler_params=pltpu.CompilerParams(
            dimension_semantics=("arbitrary",), vmem_limit_bytes=VMEM_LIMIT),
        name="swiglu_final" if final else "swiglu",
    )(*args)


def _tiles(n_seq, L):
    if L >= SSD_LONG_TILE:
        return dict(NB=1, TL_SSD=SSD_LONG_TILE, TL_HGRN=HGRN_LONG_TILE, C=SCAN_CHUNK, TM=FFN_ROWS)
    return dict(NB=SHORT_SEQ_PER_STEP, TL_SSD=L, TL_HGRN=L, C=min(L, SCAN_CHUNK), TM=min(FFN_ROWS, n_seq * L))


def _trunk(x, s_ssd, cv, s_hgrn, ssd_p, hgrn_p, ffn_p, final_w):
    n_seq, L, _ = x.shape
    t = _tiles(n_seq, L)
    s0 = None if s_ssd is None else s_ssd.reshape(n_seq, SSD_D_INNER, SSD_D_STATE)
    x, s_new, cv_new = _ssd_mixer(x, s0, cv, ssd_p, NB=t["NB"], TL=t["TL_SSD"], C=t["C"])
    x = _ffn(x.reshape(n_seq * L, D_MODEL), ffn_p, 0, None, TM=t["TM"]).reshape(n_seq, L, D_MODEL)
    x, h_new = _hgrn_mixer(x, s_hgrn, hgrn_p, NB=t["NB"], TL=t["TL_HGRN"], C=t["C"])
    y = _ffn(x.reshape(n_seq * L, D_MODEL), ffn_p, 1, final_w, TM=t["TM"]).reshape(n_seq, L, D_MODEL)
    s_new = s_new.reshape(1, n_seq, SSD_N_HEADS, SSD_HEAD_DIM, SSD_D_STATE)
    return y, s_new, cv_new[None], h_new[None]


def kernel(x_prompt, x_sample, state_ssd, cache_conv, state_hgrn, ssd_norm_w, ssd_in_w, ssd_conv_w, ssd_conv_b, ssd_dt_bias, ssd_A_log, ssd_D, ssd_gnorm_w, ssd_out_w, hgrn_norm_w, hgrn_in_w, hgrn_lower_bounds, hgrn_gnorm_w, hgrn_out_w, ffn_norm_w, ffn_w_gate, ffn_w_up, ffn_w_down, final_norm_w):
    pair_order = jnp.concatenate([jnp.arange(0, SSD_N_HEADS, 2), jnp.arange(1, SSD_N_HEADS, 2)])
    ssd_p = {
        "nw": ssd_norm_w[0].reshape(1, D_MODEL),
        "win": ssd_in_w.astype(BF16),
        "cw": ssd_conv_w,
        "cb": ssd_conv_b[0].reshape(1, SSD_CONV_DIM),
        "dtb": ssd_dt_bias[0].reshape(1, SSD_N_HEADS),
        "dtbT": ssd_dt_bias[0][pair_order].reshape(SSD_N_HEADS, 1),
        "alog": ssd_A_log[0],
        "dexp": jnp.repeat(ssd_D[0], SSD_HEAD_DIM).reshape(1, SSD_D_INNER),
        "gnw": ssd_gnorm_w[0].reshape(1, SSD_D_INNER),
        "wo": ssd_out_w.astype(BF16),
    }
    hgrn_p = {
        "nw": hgrn_norm_w[0].reshape(1, D_MODEL),
        "win": hgrn_in_w.astype(BF16),
        "lb": hgrn_lower_bounds,
        "gnw": jnp.tile(hgrn_gnorm_w[0], HGRN_N_HEADS).reshape(1, D_MODEL),
        "wo": hgrn_out_w.astype(BF16),
    }
    ffn_p = {"nw": ffn_norm_w.reshape(2, 1, D_MODEL), "wg": ffn_w_gate.astype(BF16),
             "wu": ffn_w_up.astype(BF16), "wd": ffn_w_down.astype(BF16)}
    final_w = final_norm_w.reshape(1, D_MODEL)

    y_p, ssd_s_p, conv_p, hgrn_s_p = _trunk(x_prompt, None, None, None, ssd_p, hgrn_p, ffn_p, final_w)
    y_s, ssd_s_s, conv_s, hgrn_s_s = _trunk(
        x_sample, state_ssd[0], cache_conv[0], state_hgrn[0], ssd_p, hgrn_p, ffn_p, final_w)
    return (y_p, y_s, ssd_s_p, conv_p, hgrn_s_p, ssd_s_s, conv_s, hgrn_s_s)
```

```python
import functools

import jax
import jax.numpy as jnp
from jax import lax
from jax.experimental import pallas as pl
from jax.experimental.pallas import tpu as pltpu

F32 = jnp.float32
BF16 = jnp.bfloat16

D_MODEL = 1024
NORM_EPS = 1e-6

SSD_D_INNER = 2048
SSD_HEAD_DIM = 64
SSD_N_HEADS = 32
SSD_N_GROUPS = 4
SSD_D_STATE = 128
SSD_CONV_W = 4
SSD_CONV_DIM = 3072
SSD_NORM_GROUP = 512
SSD_N_PAIRS = SSD_N_HEADS // 2
SSD_XBC_OFF = SSD_D_INNER
SSD_DT_OFF = SSD_D_INNER + SSD_CONV_DIM

HGRN_HEAD_DIM = 128
HGRN_N_HEADS = 8

FFN_HIDDEN = 2816

LANES = 128
SUBLANES = 8
SCAN_CHUNK = 64
SSD_LONG_TILE = 256
HGRN_LONG_TILE = 256
SHORT_SEQ_PER_STEP = 4
FFN_ROWS = 512
CONV_COLS = 512
HGRN_COLS = 256
VMEM_LIMIT = 56 * 1024 * 1024


def _dot(a, b):
    return jnp.dot(a, b, preferred_element_type=F32)


def _dot_nt(a, b):
    return lax.dot_general(a, b, (((1,), (1,)), ((), ())), preferred_element_type=F32)


def _dot_tn(a, b):
    return lax.dot_general(a, b, (((0,), (0,)), ((), ())), preferred_element_type=F32)


def _split3(a):
    hi = a.astype(BF16)
    r1 = a - hi.astype(F32)
    mid = r1.astype(BF16)
    lo = (r1 - mid.astype(F32)).astype(BF16)
    return hi, mid, lo


def _stack3(a):
    hi = a.astype(BF16).astype(F32)
    r1 = a - hi
    mid = r1.astype(BF16).astype(F32)
    return jnp.concatenate([hi, mid, r1 - mid, jnp.zeros_like(a)], axis=1).astype(BF16)


def _rms(x, w):
    return x * lax.rsqrt(jnp.mean(x * x, axis=-1, keepdims=True) + NORM_EPS) * w


def _silu(x):
    return x * jax.nn.sigmoid(x)


def _shift_rows(u, first_rows):
    rows, cols = u.shape
    k = first_rows.shape[0]
    nv = rows // SUBLANES
    r = pltpu.roll(u.reshape(nv, SUBLANES, cols), k, axis=1)
    first = jnp.concatenate([first_rows, jnp.zeros((SUBLANES - k, cols), u.dtype)], axis=0)
    prev = jnp.concatenate([first[None], r[:nv - 1]], axis=0)
    sub = lax.broadcasted_iota(jnp.int32, (nv, SUBLANES, cols), 1)
    return jnp.where(sub < k, prev, r).reshape(rows, cols)


def _chunk_triangle(R, C):
    rr = lax.broadcasted_iota(jnp.int32, (R, R), 0)
    cc = lax.broadcasted_iota(jnp.int32, (R, R), 1)
    return ((cc <= rr) & ((rr & -C) == (cc & -C))).astype(BF16)


def _ssd_kernel(*refs, has_init, NB, TL, C):
    it = iter(refs)
    x_ref = next(it)
    s0_ref = cv0_ref = None
    if has_init:
        s0_ref = next(it)
        cv0_ref = next(it)
    (nw_ref, win_ref, cw_ref, cb_ref, dtb_ref, dtbT_ref,
     alog_ref, alogp_ref, dexp_ref, gnw_ref, wo_ref) = [next(it) for _ in range(11)]
    xo_ref = next(it)
    so_ref = next(it)
    cvo_ref = next(it)
    (st_scr, hist_scr, z_scr, xbc_scr, y_scr, e_scr, tri_scr,
     colc_scr, we_scr) = [next(it) for _ in range(9)]
    es_scr = colcs_scr = None
    if 2 * C != LANES:
        es_scr = next(it)
        colcs_scr = next(it)

    R = NB * TL
    n_ch = R // C
    ch_per_seq = TL // C
    W2 = 2 * C
    i = pl.program_id(1)
    n_i = pl.num_programs(1)

    @pl.when(i == 0)
    def _init():
        ek = lax.broadcasted_iota(jnp.int32, e_scr.shape, 0)
        el = lax.broadcasted_iota(jnp.int32, e_scr.shape, 1)
        e_scr[...] = ((ek < 3 * SSD_N_HEADS) & ((ek & (SSD_N_HEADS - 1)) == el // SSD_HEAD_DIM)).astype(BF16)
        if es_scr is not None:
            ek = lax.broadcasted_iota(jnp.int32, es_scr.shape, 0)
            el = lax.broadcasted_iota(jnp.int32, es_scr.shape, 1)
            es_scr[...] = ((ek < 3 * SSD_N_HEADS) & ((ek & (SSD_N_HEADS - 1)) == el // C)).astype(BF16)
        tri_scr[...] = _chunk_triangle(R, C)
        hist_scr[...] = jnp.zeros_like(hist_scr)
        if has_init:
            for nb in range(NB):
                for g in range(SSD_N_GROUPS):
                    st_scr[nb, g] = s0_ref[nb, g * SSD_NORM_GROUP:(g + 1) * SSD_NORM_GROUP, :].T
                for k in range(SSD_CONV_DIM // CONV_COLS):
                    hist_scr[nb, k, SUBLANES - 3:SUBLANES, :] = cv0_ref[nb, :, k * CONV_COLS:(k + 1) * CONV_COLS]
        else:
            st_scr[...] = jnp.zeros_like(st_scr)

    x = x_ref[...].reshape(R, D_MODEL)
    h16 = _rms(x, nw_ref[...]).astype(BF16)

    def xbc_block(cb0):
        cols = slice(cb0, cb0 + CONV_COLS)
        xr = _dot(h16, win_ref[:, SSD_XBC_OFF + cb0:SSD_XBC_OFF + cb0 + CONV_COLS])
        w0, w1, w2, w3 = (cw_ref[k:k + 1, cols] for k in range(SSD_CONV_W))
        for nb in range(NB):
            xn = xr[nb * TL:(nb + 1) * TL]
            hm = hist_scr[nb, cb0 // CONV_COLS]
            sx = _shift_rows(xn, hm[7:8])
            a2 = _shift_rows(w1 * xn + w0 * sx, w1 * hm[6:8] + w0 * hm[5:7])
            xbc_scr[nb * TL:(nb + 1) * TL, cols] = _silu(w3 * xn + w2 * sx + a2 + cb_ref[:, cols])
            hist_scr[nb, cb0 // CONV_COLS] = xn[TL - SUBLANES:TL]

    pending = list(range(0, SSD_CONV_DIM, CONV_COLS))

    def emit_blocks(n):
        for _ in range(min(n, len(pending))):
            xbc_block(pending.pop(0))

    if NB > 1:
        emit_blocks(len(pending))

    dt_raw = _dot(h16, win_ref[:, SSD_DT_OFF:])
    emit_blocks(1)
    dt =jax.nn.softplus(dt_raw + dtb_ref[...])
    pi = lax.broadcasted_iota(jnp.int32, (SSD_N_HEADS, SSD_N_HEADS), 0)
    pj = lax.broadcasted_iota(jnp.int32, (SSD_N_HEADS, SSD_N_HEADS), 1)
    pick = (pj == jnp.where(pi < SSD_N_PAIRS, 2 * pi, 2 * (pi - SSD_N_PAIRS) + 1)).astype(BF16)
    dtT = jax.nn.softplus(sum(_dot_nt(pick, p) for p in _split3(dt_raw)) + dtbT_ref[...])
    a_row = -jnp.exp(alog_ref[...])
    a_pair = -jnp.exp(alogp_ref[...])
    cum = sum(_dot(tri_scr[...], p) for p in _split3(dt * a_row))
    emit_blocks(1)
    dtTp = jnp.concatenate(
        [jnp.concatenate([dtT[0:SSD_N_PAIRS, c * C:(c + 1) * C],
                          dtT[SSD_N_PAIRS:, c * C:(c + 1) * C]], axis=1) for c in range(n_ch)],
        axis=0)
    r2 = lax.broadcasted_iota(jnp.int32, (W2, W2), 0)
    c2 = lax.broadcasted_iota(jnp.int32, (W2, W2), 1)
    triT2_16 = ((r2 <= c2) & ((r2 < C) == (c2 < C))).astype(BF16)
    aTp = dtTp * jnp.concatenate([a_pair] * n_ch, axis=0)
    cumTp = sum(_dot(p, triT2_16) for p in _split3(aTp))
    last = jnp.concatenate(
        [jnp.broadcast_to(cum[(c + 1) * C - 1:(c + 1) * C, :], (C, SSD_N_HEADS)) for c in range(n_ch)], axis=0)
    wdt = jnp.exp(last - cum) * dt
    emit_blocks(1)
    cum3 = _stack3(cum)
    colc_scr[...] = _dot(cum3, e_scr[...])
    emit_blocks(1)
    we_scr[...] = _dot(_stack3(wdt), e_scr[...])
    if colcs_scr is not None:
        colcs_scr[...] = _dot(cum3, es_scr[...])
    emit_blocks(len(pending))
    for zb0 in range(0, SSD_D_INNER, CONV_COLS):
        z_scr[:, zb0:zb0 + CONV_COLS] = _silu(_dot(h16, win_ref[:, zb0:zb0 + CONV_COLS]))

    tp = lax.broadcasted_iota(jnp.int32, (C, W2), 0)
    sp = lax.broadcasted_iota(jnp.int32, (C, W2), 1)
    tril_pair = jnp.where(sp < C, sp, sp - C) <= tp
    left_e = lax.broadcasted_iota(jnp.int32, (C, LANES), 1) < SSD_HEAD_DIM

    for c in range(n_ch):
        nb = c // ch_per_seq
        rows = slice(c * C, (c + 1) * C)
        dtTp_c = dtTp[c * SSD_N_PAIRS:(c + 1) * SSD_N_PAIRS, :]
        cumTp_c = cumTp[c * SSD_N_PAIRS:(c + 1) * SSD_N_PAIRS, :]
        for g in range(SSD_N_GROUPS):
            bo = SSD_D_INNER + g * SSD_D_STATE
            co = SSD_D_INNER + SSD_N_GROUPS * SSD_D_STATE + g * SSD_D_STATE
            Bg16 = xbc_scr[rows, bo:bo + SSD_D_STATE].astype(BF16)
            Cg16 = xbc_scr[rows, co:co + SSD_D_STATE].astype(BF16)
            cbp = _dot_nt(Cg16, jnp.concatenate([Bg16, Bg16], axis=0))
            cbm = jnp.where(tril_pair, cbp, 0.0)
            st_g = st_scr[nb, g]
            y_inter = _dot(Cg16, st_g.astype(BF16))
            xw_parts = []
            el_parts = []
            for q in range(4):
                jp = g * 4 + q
                lanes = slice(jp * LANES, (jp + 1) * LANES)
                colc_e = colc_scr[rows, lanes]
                if W2 == LANES:
                    colc_l = colc_e
                else:
                    colc_l = colcs_scr[rows, jp * W2:(jp + 1) * W2]
                decay = jnp.exp(jnp.minimum(colc_l - cumTp_c[jp:jp + 1, :], 0.0))
                mp = cbm * decay * dtTp_c[jp:jp + 1, :]
                xs_p = xbc_scr[rows, lanes]
                rhs = jnp.concatenate(
                    [jnp.where(left_e, xs_p, 0.0).astype(BF16),
                     jnp.where(left_e, 0.0, xs_p).astype(BF16)], axis=0)
                e_c = jnp.exp(colc_e)
                y_scr[rows, lanes] = _dot(mp.astype(BF16), rhs) + e_c * y_inter[:, q * LANES:(q + 1) * LANES]
                xw_parts.append((xs_p * we_scr[rows, lanes]).astype(BF16))
                el_parts.append(e_c[C - 1:C, :])
            xw_g = jnp.concatenate(xw_parts, axis=1)
            el_g = jnp.concatenate(el_parts, axis=1)
            st_scr[nb, g] = el_g * st_g + _dot_tn(Bg16, xw_g)

    out = None
    for g in range(SSD_N_GROUPS):
        cols = slice(g * SSD_NORM_GROUP, (g + 1) * SSD_NORM_GROUP)
        yg = (y_scr[:, cols] + dexp_ref[:, cols] * xbc_scr[:, cols]) * z_scr[:, cols]
        yn = yg * lax.rsqrt(jnp.mean(yg * yg, axis=-1, keepdims=True) + NORM_EPS) * gnw_ref[:, cols]
        part = _dot(yn.astype(BF16), wo_ref[cols, :])
        out = part if out is None else out + part
    xo_ref[...] = (x + out).reshape(NB, TL, D_MODEL)

    @pl.when(i == n_i - 1)
    def _fin():
        for nb in range(NB):
            for g in range(SSD_N_GROUPS):
                so_ref[nb, g * SSD_NORM_GROUP:(g + 1) * SSD_NORM_GROUP, :] = st_scr[nb, g].T
            for k in range(SSD_CONV_DIM // CONV_COLS):
                cvo_ref[nb, :, k * CONV_COLS:(k + 1) * CONV_COLS] = hist_scr[nb, k, SUBLANES - 3:SUBLANES, :]


def _const_spec(shape):
    nd = len(shape)
    return pl.BlockSpec(shape, lambda *_: (0,) * nd, pipeline_mode=pl.Buffered(1))


def _layer_spec(arr, layer):
    nd = arr.ndim - 1
    return pl.BlockSpec((None,) + arr.shape[1:], lambda *_: (layer,) + (0,) * nd,
                        pipeline_mode=pl.Buffered(1))


def _ssd_mixer(x, s0, cv0, p, *, NB, TL, C):
    n_seq, L, _ = x.shape
    has_init = s0 is not None
    R = NB * TL
    grid = (n_seq // NB, L // TL)
    in_specs = [pl.BlockSpec((NB, TL, D_MODEL), lambda b, i: (b, i, 0))]
    args = [x]
    if has_init:
        in_specs += [pl.BlockSpec((NB, SSD_D_INNER, SSD_D_STATE), lambda b, i: (b, 0, 0)),
                     pl.BlockSpec((NB, 3, SSD_CONV_DIM), lambda b, i: (b, 0, 0))]
        args += [s0, cv0]
    alog_pair = jnp.concatenate(
        [jnp.broadcast_to(p["alog"][0::2, None], (SSD_N_PAIRS, C)),
         jnp.broadcast_to(p["alog"][1::2, None], (SSD_N_PAIRS, C))], axis=1)
    consts = [p["nw"], p["win"], p["cw"], p["cb"], p["dtb"], p["dtbT"],
              p["alog"].reshape(1, SSD_N_HEADS), alog_pair, p["dexp"], p["gnw"], p["wo"]]
    for c in consts:
        in_specs.append(_layer_spec(c, 0) if c.ndim == 3 else _const_spec(c.shape))
    args += consts
    out_shape = (jax.ShapeDtypeStruct((n_seq, L, D_MODEL), F32),
                 jax.ShapeDtypeStruct((n_seq, SSD_D_INNER, SSD_D_STATE), F32),
                 jax.ShapeDtypeStruct((n_seq, 3, SSD_CONV_DIM), F32))
    out_specs = (pl.BlockSpec((NB, TL, D_MODEL), lambda b, i: (b, i, 0)),
                 pl.BlockSpec((NB, SSD_D_INNER, SSD_D_STATE), lambda b, i: (b, 0, 0)),
                 pl.BlockSpec((NB, 3, SSD_CONV_DIM), lambda b, i: (b, 0, 0)))
    scratch = [pltpu.VMEM((NB, SSD_N_GROUPS, SSD_D_STATE, SSD_NORM_GROUP), F32),
               pltpu.VMEM((NB, SSD_CONV_DIM // CONV_COLS, SUBLANES, CONV_COLS), F32),
               pltpu.VMEM((R, SSD_D_INNER), F32),
               pltpu.VMEM((R, SSD_CONV_DIM), F32),
               pltpu.VMEM((R, SSD_D_INNER), F32),
               pltpu.VMEM((LANES, SSD_D_INNER), BF16),
               pltpu.VMEM((R, R), BF16),
               pltpu.VMEM((R, SSD_D_INNER), F32),
               pltpu.VMEM((R, SSD_D_INNER), F32)]
    if 2 * C != LANES:
        scratch += [pltpu.VMEM((LANES, SSD_N_PAIRS * 2 * C), BF16),
                    pltpu.VMEM((R, SSD_N_PAIRS * 2 * C), F32)]
    return pl.pallas_call(
        functools.partial(_ssd_kernel, has_init=has_init, NB=NB, TL=TL, C=C),
        grid=grid, in_specs=in_specs, out_specs=out_specs, out_shape=out_shape,
        scratch_shapes=scratch,
        compiler_params=pltpu.CompilerParams(
            dimension_semantics=("arbitrary", "arbitrary"), vmem_limit_bytes=VMEM_LIMIT),
        name="ssd_mixer_init" if has_init else "ssd_mixer_zero",
    )(*args)


def _hgrn_kernel(*refs, has_init, NB, TL, C):
    it = iter(refs)
    x_ref = next(it)
    s0_ref = next(it) if has_init else None
    nw_ref, win_ref, lb_ref, gnw_ref, wo_ref = [next(it) for _ in range(5)]
    xo_ref = next(it)
    so_ref = next(it)
    st_scr, tri_scr, qe_scr, ke_scr, kw_scr, v_scr, g_scr, el_scr, o_scr = [next(it) for _ in range(9)]

    R = NB * TL
    n_ch = R // C
    ch_per_seq = TL // C
    HD = HGRN_HEAD_DIM
    i = pl.program_id(1)
    n_i = pl.num_programs(1)

    @pl.when(i == 0)
    def _init():
        tri_scr[...] = _chunk_triangle(R, C)
        if has_init:
            for nb in range(NB):
                for hh in range(HGRN_N_HEADS):
                    st_scr[nb, hh] = s0_ref[nb, hh].T
        else:
            st_scr[...] = jnp.zeros_like(st_scr)

    x = x_ref[...].reshape(R, D_MODEL)
    h16 = _rms(x, nw_ref[...]).astype(BF16)
    lb_soft = jax.nn.softmax(lb_ref[...], axis=0)
    lb = (lb_soft[0:1, :] + lb_soft[1:2, :]) - lb_soft[0:1, :]

    for c0 in range(0, D_MODEL, HGRN_COLS):
        cols = slice(c0, c0 + HGRN_COLS)
        q = _silu(_dot(h16, win_ref[:, c0:c0 + HGRN_COLS]))
        f = _dot(h16, win_ref[:, D_MODEL + c0:D_MODEL + c0 + HGRN_COLS])
        v = _dot(h16, win_ref[:, 2 * D_MODEL + c0:2 * D_MODEL + c0 + HGRN_COLS])
        g = _dot(h16, win_ref[:, 3 * D_MODEL + c0:3 * D_MODEL + c0 + HGRN_COLS])
        lbc = lb[:, cols]
        forget = lbc + (1.0 - lbc) * jax.nn.sigmoid(f)
        k = 1.0 - forget
        b = sum(_dot(tri_scr[...], p) for p in _split3(jnp.log(forget)))
        ends = [b[(c + 1) * C - 1:(c + 1) * C, :] for c in range(n_ch)]
        last = jnp.concatenate([jnp.broadcast_to(e, (C, HGRN_COLS)) for e in ends], axis=0)
        qe_scr[:, cols] = (q * jnp.exp(b)).astype(BF16)
        ke_scr[:, cols] = (k * jnp.exp(-b)).astype(BF16)
        kw_scr[:, cols] = (k * jnp.exp(last - b)).astype(BF16)
        v_scr[:, cols] = v.astype(BF16)
        g_scr[:, cols] = _silu(g)
        el_scr[:, cols] = jnp.exp(jnp.concatenate(ends, axis=0))

    rr = lax.broadcasted_iota(jnp.int32, (C, C), 0)
    cc = lax.broadcasted_iota(jnp.int32, (C, C), 1)
    tril = cc <= rr

    for c in range(n_ch):
        nb = c // ch_per_seq
        rows = slice(c * C, (c + 1) * C)
        for hh in range(HGRN_N_HEADS):
            sl = slice(hh * HD, (hh + 1) * HD)
            qe, v = qe_scr[rows, sl], v_scr[rows, sl]
            sc = jnp.where(tril, _dot_nt(qe, ke_scr[rows, sl]), 0.0)
            s_h = st_scr[nb, hh]
            o_scr[rows, sl] = _dot(sc.astype(BF16), v) + _dot_nt(qe, s_h.astype(BF16))
            st_scr[nb, hh] = el_scr[c:c + 1, sl] * s_h + _dot_tn(v, kw_scr[rows, sl])

    out = None
    for c0 in range(0, D_MODEL, HGRN_COLS):
        parts = []
        for h0 in range(c0, c0 + HGRN_COLS, HD):
            blk = o_scr[:, h0:h0 + HD]
            parts.append(blk * lax.rsqrt(jnp.mean(blk * blk, axis=-1, keepdims=True) + NORM_EPS))
        cols = slice(c0, c0 + HGRN_COLS)
        on = jnp.concatenate(parts, axis=1) * gnw_ref[:, cols] * g_scr[:, cols]
        part = _dot(on.astype(BF16), wo_ref[cols, :])
        out = part if out is None else out + part
    xo_ref[...] = (x + out).reshape(NB, TL, D_MODEL)

    @pl.when(i == n_i - 1)
    def _fin():
        for nb in range(NB):
            for hh in range(HGRN_N_HEADS):
                so_ref[nb, hh] = st_scr[nb, hh].T


def _hgrn_mixer(x, s0, p, *, NB, TL, C):
    n_seq, L, _ = x.shape
    has_init = s0 is not None
    R = NB * TL
    grid = (n_seq // NB, L // TL)
    st_block = (NB, HGRN_N_HEADS, HGRN_HEAD_DIM, HGRN_HEAD_DIM)
    in_specs = [pl.BlockSpec((NB, TL, D_MODEL), lambda b, i: (b, i, 0))]
    args = [x]
    if has_init:
        in_specs.append(pl.BlockSpec(st_block, lambda b, i: (b, 0, 0, 0)))
        args.append(s0)
    consts = [p["nw"], p["win"], p["lb"], p["gnw"], p["wo"]]
    for c in consts:
        in_specs.append(_layer_spec(c, 0) if c.ndim == 3 else _const_spec(c.shape))
    args += consts
    out_shape = (jax.ShapeDtypeStruct((n_seq, L, D_MODEL), F32),
                 jax.ShapeDtypeStruct((n_seq,) + st_block[1:], F32))
    out_specs = (pl.BlockSpec((NB, TL, D_MODEL), lambda b, i: (b, i, 0)),
                 pl.BlockSpec(st_block, lambda b, i: (b, 0, 0, 0)))
    scratch = [pltpu.VMEM(st_block, F32),
               pltpu.VMEM((R, R), BF16),
               pltpu.VMEM((R, D_MODEL), BF16),
               pltpu.VMEM((R, D_MODEL), BF16),
               pltpu.VMEM((R, D_MODEL), BF16),
               pltpu.VMEM((R, D_MODEL), BF16),
               pltpu.VMEM((R, D_MODEL), F32),
               pltpu.VMEM((R // C, D_MODEL), F32),
               pltpu.VMEM((R, D_MODEL), F32)]
    return pl.pallas_call(
        functools.partial(_hgrn_kernel, has_init=has_init, NB=NB, TL=TL, C=C),
        grid=grid, in_specs=in_specs, out_specs=out_specs, out_shape=out_shape,
        scratch_shapes=scratch,
        compiler_params=pltpu.CompilerParams(
            dimension_semantics=("arbitrary", "arbitrary"), vmem_limit_bytes=VMEM_LIMIT),
        name="hgrn_mixer_init" if has_init else "hgrn_mixer_zero",
    )(*args)


def _ffn_kernel(*refs, final):
    if final:
        x_ref, nw_ref, wg_ref, wu_ref, wd_ref, fw_ref, o_ref = refs
    else:
        x_ref, nw_ref, wg_ref, wu_ref, wd_ref, o_ref = refs
    x = x_ref[...]
    h16 = _rms(x, nw_ref[...]).astype(BF16)
    act = _silu(_dot(h16, wg_ref[...])) * _dot(h16, wu_ref[...])
    y = x + _dot(act.astype(BF16), wd_ref[...])
    if final:
        y = _rms(y, fw_ref[...])
    o_ref[...] = y


def _ffn(x2d, p, layer, final_w, *, TM):
    rows = x2d.shape[0]
    final = final_w is not None
    in_specs = [pl.BlockSpec((TM, D_MODEL), lambda i: (i, 0)), _const_spec(p["nw"][layer].shape)]
    in_specs += [_layer_spec(p[k], layer) for k in ("wg", "wu", "wd")]
    args = [x2d, p["nw"][layer], p["wg"], p["wu"], p["wd"]]
    if final:
        in_specs.append(_const_spec(final_w.shape))
        args.append(final_w)
    return pl.pallas_call(
        functools.partial(_ffn_kernel, final=final),
        grid=(rows // TM,),
        in_specs=in_specs,
        out_specs=pl.BlockSpec((TM, D_MODEL), lambda i: (i, 0)),
        out_shape=jax.ShapeDtypeStruct((rows, D_MODEL), F32),
        compiler_params=pltpu.CompilerParams(
            dimension_semantics=("arbitrary",), vmem_limit_bytes=VMEM_LIMIT),
        name="swiglu_final" if final else "swiglu",
    )(*args)


def _tiles(n_seq, L):
    if L >= SSD_LONG_TILE:
        return dict(NB=1, TL_SSD=SSD_LONG_TILE, TL_HGRN=HGRN_LONG_TILE, C=SCAN_CHUNK, TM=FFN_ROWS)
    return dict(NB=SHORT_SEQ_PER_STEP, TL_SSD=L, TL_HGRN=L, C=min(L, SCAN_CHUNK), TM=min(FFN_ROWS, n_seq * L))


def _trunk(x, s_ssd, cv, s_hgrn, ssd_p, hgrn_p, ffn_p, final_w):
    n_seq, L, _ = x.shape
    t = _tiles(n_seq, L)
    s0 = None if s_ssd is None else s_ssd.reshape(n_seq, SSD_D_INNER, SSD_D_STATE)
    x, s_new, cv_new = _ssd_mixer(x, s0, cv, ssd_p, NB=t["NB"], TL=t["TL_SSD"], C=t["C"])
    x = _ffn(x.reshape(n_seq * L, D_MODEL), ffn_p, 0, None, TM=t["TM"]).reshape(n_seq, L, D_MODEL)
    x, h_new = _hgrn_mixer(x, s_hgrn, hgrn_p, NB=t["NB"], TL=t["TL_HGRN"], C=t["C"])
    y = _ffn(x.reshape(n_seq * L, D_MODEL), ffn_p, 1, final_w, TM=t["TM"]).reshape(n_seq, L, D_MODEL)
    s_new = s_new.reshape(1, n_seq, SSD_N_HEADS, SSD_HEAD_DIM, SSD_D_STATE)
    return y, s_new, cv_new[None], h_new[None]


def kernel(x_prompt, x_sample, state_ssd, cache_conv, state_hgrn, ssd_norm_w, ssd_in_w, ssd_conv_w, ssd_conv_b, ssd_dt_bias, ssd_A_log, ssd_D, ssd_gnorm_w, ssd_out_w, hgrn_norm_w, hgrn_in_w, hgrn_lower_bounds, hgrn_gnorm_w, hgrn_out_w, ffn_norm_w, ffn_w_gate, ffn_w_up, ffn_w_down, final_norm_w):
    pair_order = jnp.concatenate([jnp.arange(0, SSD_N_HEADS, 2), jnp.arange(1, SSD_N_HEADS, 2)])
    ssd_p = {
        "nw": ssd_norm_w[0].reshape(1, D_MODEL),
        "win": ssd_in_w.astype(BF16),
        "cw": ssd_conv_w,
        "cb": ssd_conv_b[0].reshape(1, SSD_CONV_DIM),
        "dtb": ssd_dt_bias[0].reshape(1, SSD_N_HEADS),
        "dtbT": ssd_dt_bias[0][pair_order].reshape(SSD_N_HEADS, 1),
        "alog": ssd_A_log[0],
        "dexp": jnp.repeat(ssd_D[0], SSD_HEAD_DIM).reshape(1, SSD_D_INNER),
        "gnw": ssd_gnorm_w[0].reshape(1, SSD_D_INNER),
        "wo": ssd_out_w.astype(BF16),
    }
    hgrn_p = {
        "nw": hgrn_norm_w[0].reshape(1, D_MODEL),
        "win": hgrn_in_w.astype(BF16),
        "lb": hgrn_lower_bounds,
        "gnw": jnp.tile(hgrn_gnorm_w[0], HGRN_N_HEADS).reshape(1, D_MODEL),
        "wo": hgrn_out_w.astype(BF16),
    }
    ffn_p = {"nw": ffn_norm_w.reshape(2, 1, D_MODEL), "wg": ffn_w_gate.astype(BF16),
             "wu": ffn_w_up.astype(BF16), "wd": ffn_w_down.astype(BF16)}
    final_w = final_norm_w.reshape(1, D_MODEL)

    y_p, ssd_s_p, conv_p, hgrn_s_p = _trunk(x_prompt, None, None, None, ssd_p, hgrn_p, ffn_p, final_w)
    y_s, ssd_s_s, conv_s, hgrn_s_s = _trunk(
        x_sample, state_ssd[0], cache_conv[0], state_hgrn[0], ssd_p, hgrn_p, ffn_p, final_w)
    return (y_p, y_s, ssd_s_p, conv_p, hgrn_s_p, ssd_s_s, conv_s, hgrn_s_s)
```

```python
import functools

import jax
import jax.numpy as jnp
from jax import lax
from jax.experimental import pallas as pl
from jax.experimental.pallas import tpu as pltpu

F32 = jnp.float32
BF16 = jnp.bfloat16

D_MODEL = 1024
NORM_EPS = 1e-6

SSD_D_INNER = 2048
SSD_HEAD_DIM = 64
SSD_N_HEADS = 32
SSD_N_GROUPS = 4
SSD_D_STATE = 128
SSD_CONV_W = 4
SSD_CONV_DIM = 3072
SSD_NORM_GROUP = 512
SSD_N_PAIRS = SSD_N_HEADS // 2
SSD_XBC_OFF = SSD_D_INNER
SSD_DT_OFF = SSD_D_INNER + SSD_CONV_DIM

HGRN_HEAD_DIM = 128
HGRN_N_HEADS = 8

FFN_HIDDEN = 2816

LANES = 128
SUBLANES = 8
BF16_TILE_ROWS = 16
SCAN_CHUNK = 64
SSD_LONG_TILE = 256
HGRN_LONG_TILE = 256
SHORT_SEQ_PER_STEP = 4
FFN_ROWS = 512
CONV_COLS = 512
HGRN_COLS = 256
VMEM_LIMIT = 56 * 1024 * 1024


def _dot(a, b):
    return jnp.dot(a, b, preferred_element_type=F32)


def _dot_nt(a, b):
    return lax.dot_general(a, b, (((1,), (1,)), ((), ())), preferred_element_type=F32)


def _dot_tn(a, b):
    return lax.dot_general(a, b, (((0,), (0,)), ((), ())), preferred_element_type=F32)


def _split3(a):
    hi = a.astype(BF16)
    r1 = a - hi.astype(F32)
    mid = r1.astype(BF16)
    lo = (r1 - mid.astype(F32)).astype(BF16)
    return hi, mid, lo


def _stack3(a):
    hi = a.astype(BF16).astype(F32)
    r1 = a - hi
    mid = r1.astype(BF16).astype(F32)
    return jnp.concatenate([hi, mid, r1 - mid, jnp.zeros_like(a)], axis=1).astype(BF16)


def _rms(x, w):
    return x * lax.rsqrt(jnp.mean(x * x, axis=-1, keepdims=True) + NORM_EPS) * w


def _silu(x):
    return x * jax.nn.sigmoid(x)


def _shift_rows(u, first_rows):
    rows, cols = u.shape
    k = first_rows.shape[0]
    nv = rows // SUBLANES
    r = pltpu.roll(u.reshape(nv, SUBLANES, cols), k, axis=1)
    first = jnp.concatenate([first_rows, jnp.zeros((SUBLANES - k, cols), u.dtype)], axis=0)
    prev = jnp.concatenate([first[None], r[:nv - 1]], axis=0)
    sub = lax.broadcasted_iota(jnp.int32, (nv, SUBLANES, cols), 1)
    return jnp.where(sub < k, prev, r).reshape(rows, cols)


def _chunk_triangle(R, C):
    rr = lax.broadcasted_iota(jnp.int32, (R, R), 0)
    cc = lax.broadcasted_iota(jnp.int32, (R, R), 1)
    return ((cc <= rr) & ((rr & -C) == (cc & -C))).astype(BF16)


def _ssd_kernel(*refs, has_init, NB, TL, C, n_side):
    it = iter(refs)
    x_ref = next(it)
    s0_ref = cv0_ref = None
    if has_init:
        s0_ref = next(it)
        cv0_ref = next(it)
    (nw_ref, win_ref, cw_ref, cb_ref, dtb_ref, dtbT_ref,
     alog_ref, alogp_ref, dexp_ref, gnw_ref, wo_ref) = [next(it) for _ in range(11)]
    side_in = [next(it) for _ in range(n_side)]
    xo_ref = next(it)
    so_ref = next(it)
    cvo_ref = next(it)
    for w_ref in side_in:
        next(it)[...] = w_ref[...].astype(BF16)
    (st_scr, hist_scr, z_scr, xbc_scr, y_scr, e_scr, tri_scr,
     colc_scr, we_scr) = [next(it) for _ in range(9)]
    es_scr = colcs_scr = None
    if 2 * C != LANES:
        es_scr = next(it)
        colcs_scr = next(it)

    R = NB * TL
    n_ch = R // C
    ch_per_seq = TL // C
    W2 = 2 * C
    i = pl.program_id(1)
    n_i = pl.num_programs(1)

    @pl.when(i == 0)
    def _init():
        ek = lax.broadcasted_iota(jnp.int32, e_scr.shape, 0)
        el = lax.broadcasted_iota(jnp.int32, e_scr.shape, 1)
        e_scr[...] = ((ek < 3 * SSD_N_HEADS) & ((ek & (SSD_N_HEADS - 1)) == el // SSD_HEAD_DIM)).astype(BF16)
        if es_scr is not None:
            ek = lax.broadcasted_iota(jnp.int32, es_scr.shape, 0)
            el = lax.broadcasted_iota(jnp.int32, es_scr.shape, 1)
            es_scr[...] = ((ek < 3 * SSD_N_HEADS) & ((ek & (SSD_N_HEADS - 1)) == el // C)).astype(BF16)
        tri_scr[...] = _chunk_triangle(R, C)
        hist_scr[...] = jnp.zeros_like(hist_scr)
        if has_init:
            for nb in range(NB):
                for g in range(SSD_N_GROUPS):
                    st_scr[nb, g] = s0_ref[nb, g * SSD_NORM_GROUP:(g + 1) * SSD_NORM_GROUP, :].T
                for k in range(SSD_CONV_DIM // CONV_COLS):
                    hist_scr[nb, k, SUBLANES - 3:SUBLANES, :] = cv0_ref[nb, :, k * CONV_COLS:(k + 1) * CONV_COLS]
        else:
            st_scr[...] = jnp.zeros_like(st_scr)

    x = x_ref[...].reshape(R, D_MODEL)
    h16 = _rms(x, nw_ref[...]).astype(BF16)

    for cb0 in range(0, SSD_CONV_DIM, CONV_COLS):
        cols = slice(cb0, cb0 + CONV_COLS)
        xr = _dot(h16, win_ref[:, SSD_XBC_OFF + cb0:SSD_XBC_OFF + cb0 + CONV_COLS])
        w0, w1, w2, w3 = (cw_ref[k:k + 1, cols] for k in range(SSD_CONV_W))
        for nb in range(NB):
            xn = xr[nb * TL:(nb + 1) * TL]
            hm = hist_scr[nb, cb0 // CONV_COLS]
            sx = _shift_rows(xn, hm[7:8])
            a2 = _shift_rows(w1 * xn + w0 * sx, w1 * hm[6:8] + w0 * hm[5:7])
            xbc_scr[nb * TL:(nb + 1) * TL, cols] = _silu(w3 * xn + w2 * sx + a2 + cb_ref[:, cols])
            hist_scr[nb, cb0 // CONV_COLS] = xn[TL - SUBLANES:TL]
    for zb0 in range(0, SSD_D_INNER, CONV_COLS):
        z_scr[:, zb0:zb0 + CONV_COLS] = _silu(_dot(h16, win_ref[:, zb0:zb0 + CONV_COLS]))

    dt_raw = _dot(h16, win_ref[:, SSD_DT_OFF:])
    dt = jax.nn.softplus(dt_raw + dtb_ref[...])
    pi = lax.broadcasted_iota(jnp.int32, (SSD_N_HEADS, SSD_N_HEADS), 0)
    pj = lax.broadcasted_iota(jnp.int32, (SSD_N_HEADS, SSD_N_HEADS), 1)
    pick = (pj == jnp.where(pi < SSD_N_PAIRS, 2 * pi, 2 * (pi - SSD_N_PAIRS) + 1)).astype(BF16)
    dtT = jax.nn.softplus(sum(_dot_nt(pick, p) for p in _split3(dt_raw)) + dtbT_ref[...])
    dtTp = jnp.concatenate(
        [jnp.concatenate([dtT[0:SSD_N_PAIRS, c * C:(c + 1) * C],
                          dtT[SSD_N_PAIRS:, c * C:(c + 1) * C]], axis=1) for c in range(n_ch)],
        axis=0)
    a_row = -jnp.exp(alog_ref[...])
    a_pair = -jnp.exp(alogp_ref[...])
    cum = sum(_dot(tri_scr[...], p) for p in _split3(dt * a_row))
    last = jnp.concatenate(
        [jnp.broadcast_to(cum[(c + 1) * C - 1:(c + 1) * C, :], (C, SSD_N_HEADS)) for c in range(n_ch)], axis=0)
    wdt = jnp.exp(last - cum) * dt
    cum3 = _stack3(cum)
    colc_scr[...] = _dot(cum3, e_scr[...])
    we_scr[...] = _dot(_stack3(wdt), e_scr[...])
    if colcs_scr is not None:
        colcs_scr[...] = _dot(cum3, es_scr[...])
    r2 = lax.broadcasted_iota(jnp.int32, (W2, W2), 0)
    c2 = lax.broadcasted_iota(jnp.int32, (W2, W2), 1)
    triT2_16 = ((r2 <= c2) & ((r2 < C) == (c2 < C))).astype(BF16)
    aTp = dtTp * jnp.concatenate([a_pair] * n_ch, axis=0)
    cumTp = sum(_dot(p, triT2_16) for p in _split3(aTp))

    tp = lax.broadcasted_iota(jnp.int32, (C, W2), 0)
    sp = lax.broadcasted_iota(jnp.int32, (C, W2), 1)
    tril_pair = jnp.where(sp < C, sp, sp - C) <= tp
    left_e = lax.broadcasted_iota(jnp.int32, (C, LANES), 1) < SSD_HEAD_DIM

    for c in range(n_ch):
        nb = c // ch_per_seq
        rows = slice(c * C, (c + 1) * C)
        dtTp_c = dtTp[c * SSD_N_PAIRS:(c + 1) * SSD_N_PAIRS, :]
        cumTp_c = cumTp[c * SSD_N_PAIRS:(c + 1) * SSD_N_PAIRS, :]
        for g in range(SSD_N_GROUPS):
            bo = SSD_D_INNER + g * SSD_D_STATE
            co = SSD_D_INNER + SSD_N_GROUPS * SSD_D_STATE + g * SSD_D_STATE
            Bg16 = xbc_scr[rows, bo:bo + SSD_D_STATE].astype(BF16)
            Cg16 = xbc_scr[rows, co:co + SSD_D_STATE].astype(BF16)
            cbp = _dot_nt(Cg16, jnp.concatenate([Bg16, Bg16], axis=0))
            cbm = jnp.where(tril_pair, cbp, 0.0)
            st_g = st_scr[nb, g]
            y_inter = _dot(Cg16, st_g.astype(BF16))
            xw_parts = []
            el_parts = []
            for q in range(4):
                jp = g * 4 + q
                lanes = slice(jp * LANES, (jp + 1) * LANES)
                colc_e = colc_scr[rows, lanes]
                if W2 == LANES:
                    colc_l = colc_e
                else:
                    colc_l = colcs_scr[rows, jp * W2:(jp + 1) * W2]
                decay = jnp.exp(jnp.minimum(colc_l - cumTp_c[jp:jp + 1, :], 0.0))
                mp = cbm * decay * dtTp_c[jp:jp + 1, :]
                xs_p = xbc_scr[rows, lanes]
                rhs = jnp.concatenate(
                    [jnp.where(left_e, xs_p, 0.0).astype(BF16),
                     jnp.where(left_e, 0.0, xs_p).astype(BF16)], axis=0)
                e_c = jnp.exp(colc_e)
                y_scr[rows, lanes] = _dot(mp.astype(BF16), rhs) + e_c * y_inter[:, q * LANES:(q + 1) * LANES]
                xw_parts.append((xs_p * we_scr[rows, lanes]).astype(BF16))
                el_parts.append(e_c[C - 1:C, :])
            xw_g = jnp.concatenate(xw_parts, axis=1)
            el_g = jnp.concatenate(el_parts, axis=1)
            st_scr[nb, g] = el_g * st_g + _dot_tn(Bg16, xw_g)

    out = None
    for g in range(SSD_N_GROUPS):
        cols = slice(g * SSD_NORM_GROUP, (g + 1) * SSD_NORM_GROUP)
        yg = (y_scr[:, cols] + dexp_ref[:, cols] * xbc_scr[:, cols]) * z_scr[:, cols]
        yn = yg * lax.rsqrt(jnp.mean(yg * yg, axis=-1, keepdims=True) + NORM_EPS) * gnw_ref[:, cols]
        part = _dot(yn.astype(BF16), wo_ref[cols, :])
        out = part if out is None else out + part
    xo_ref[...] = (x + out).reshape(NB, TL, D_MODEL)

    @pl.when(i == n_i - 1)
    def _fin():
        for nb in range(NB):
            for g in range(SSD_N_GROUPS):
                so_ref[nb, g * SSD_NORM_GROUP:(g + 1) * SSD_NORM_GROUP, :] = st_scr[nb, g].T
            for k in range(SSD_CONV_DIM // CONV_COLS):
                cvo_ref[nb, :, k * CONV_COLS:(k + 1) * CONV_COLS] = hist_scr[nb, k, SUBLANES - 3:SUBLANES, :]


def _const_spec(shape):
    nd = len(shape)
    return pl.BlockSpec(shape, lambda *_: (0,) * nd, pipeline_mode=pl.Buffered(1))


def _layer_spec(arr, layer):
    nd = arr.ndim - 1
    return pl.BlockSpec((None,) + arr.shape[1:], lambda *_: (layer,) + (0,) * nd,
                        pipeline_mode=pl.Buffered(1))


def _side_cast(side, n_steps, step_of):
    in_specs, out_specs, out_shapes = [], [], []
    for w, layer in side:
        rows, cols = w.shape[1:]
        n_blk = max(n for n in range(1, n_steps + 1) if rows % n == 0 and (rows // n) % BF16_TILE_ROWS == 0)
        br = rows // n_blk
        in_specs.append(pl.BlockSpec(
            (None, br, cols), lambda *g, layer=layer, n_blk=n_blk: (layer, jnp.minimum(step_of(*g), n_blk - 1), 0)))
        out_specs.append(pl.BlockSpec(
            (br, cols), lambda *g, n_blk=n_blk: (jnp.minimum(step_of(*g), n_blk - 1), 0)))
        out_shapes.append(jax.ShapeDtypeStruct((rows, cols), BF16))
    return in_specs, out_specs, out_shapes


def _ssd_mixer(x, s0, cv0, p, side, *, NB, TL, C):
    n_seq, L, _ = x.shape
    has_init = s0 is not None
    R = NB * TL
    grid = (n_seq // NB, L // TL)
    side_in, side_out, side_shapes = _side_cast(side, grid[0] * grid[1], lambda b, i: b * grid[1] + i)
    in_specs = [pl.BlockSpec((NB, TL, D_MODEL), lambda b, i: (b, i, 0))]
    args = [x]
    if has_init:
        in_specs += [pl.BlockSpec((NB, SSD_D_INNER, SSD_D_STATE), lambda b, i: (b, 0, 0)),
                     pl.BlockSpec((NB, 3, SSD_CONV_DIM), lambda b, i: (b, 0, 0))]
        args += [s0, cv0]
    alog_pair = jnp.concatenate(
        [jnp.broadcast_to(p["alog"][0::2, None], (SSD_N_PAIRS, C)),
         jnp.broadcast_to(p["alog"][1::2, None], (SSD_N_PAIRS, C))], axis=1)
    consts = [p["nw"], p["win"], p["cw"], p["cb"], p["dtb"], p["dtbT"],
              p["alog"].reshape(1, SSD_N_HEADS), alog_pair, p["dexp"], p["gnw"], p["wo"]]
    for c in consts:
        in_specs.append(_layer_spec(c, 0) if c.ndim == 3 else _const_spec(c.shape))
    args += consts
    in_specs += side_in
    args += [w for w, _ in side]
    out_shape = [jax.ShapeDtypeStruct((n_seq, L, D_MODEL), F32),
                 jax.ShapeDtypeStruct((n_seq, SSD_D_INNER, SSD_D_STATE), F32),
                 jax.ShapeDtypeStruct((n_seq, 3, SSD_CONV_DIM), F32)] + side_shapes
    out_specs = [pl.BlockSpec((NB, TL, D_MODEL), lambda b, i: (b, i, 0)),
                 pl.BlockSpec((NB, SSD_D_INNER, SSD_D_STATE), lambda b, i: (b, 0, 0)),
                 pl.BlockSpec((NB, 3, SSD_CONV_DIM), lambda b, i: (b, 0, 0))] + side_out
    scratch = [pltpu.VMEM((NB, SSD_N_GROUPS, SSD_D_STATE, SSD_NORM_GROUP), F32),
               pltpu.VMEM((NB, SSD_CONV_DIM // CONV_COLS, SUBLANES, CONV_COLS), F32),
               pltpu.VMEM((R, SSD_D_INNER), F32),
               pltpu.VMEM((R, SSD_CONV_DIM), F32),
               pltpu.VMEM((R, SSD_D_INNER), F32),
               pltpu.VMEM((LANES, SSD_D_INNER), BF16),
               pltpu.VMEM((R, R), BF16),
               pltpu.VMEM((R, SSD_D_INNER), F32),
               pltpu.VMEM((R, SSD_D_INNER), F32)]
    if 2 * C != LANES:
        scratch += [pltpu.VMEM((LANES, SSD_N_PAIRS * 2 * C), BF16),
                    pltpu.VMEM((R, SSD_N_PAIRS * 2 * C), F32)]
    outs = pl.pallas_call(
        functools.partial(_ssd_kernel, has_init=has_init, NB=NB, TL=TL, C=C, n_side=len(side)),
        grid=grid, in_specs=in_specs, out_specs=out_specs, out_shape=out_shape,
        scratch_shapes=scratch,
        compiler_params=pltpu.CompilerParams(
            dimension_semantics=("arbitrary", "arbitrary"), vmem_limit_bytes=VMEM_LIMIT),
        name="ssd_mixer_init" if has_init else "ssd_mixer_zero",
    )(*args)
    return outs[0], outs[1], outs[2], list(outs[3:])


def _hgrn_kernel(*refs, has_init, NB, TL, C):
    it = iter(refs)
    x_ref = next(it)
    s0_ref = next(it) if has_init else None
    nw_ref, win_ref, lb_ref, gnw_ref, wo_ref = [next(it) for _ in range(5)]
    xo_ref = next(it)
    so_ref = next(it)
    st_scr, tri_scr, qe_scr, ke_scr, kw_scr, v_scr, g_scr, el_scr, o_scr = [next(it) for _ in range(9)]

    R = NB * TL
    n_ch = R // C
    ch_per_seq = TL // C
    HD = HGRN_HEAD_DIM
    i = pl.program_id(1)
    n_i = pl.num_programs(1)

    @pl.when(i == 0)
    def _init():
        tri_scr[...] = _chunk_triangle(R, C)
        if has_init:
            for nb in range(NB):
                for hh in range(HGRN_N_HEADS):
                    st_scr[nb, hh] = s0_ref[nb, hh].T
        else:
            st_scr[...] = jnp.zeros_like(st_scr)

    x = x_ref[...].reshape(R, D_MODEL)
    h16 = _rms(x, nw_ref[...]).astype(BF16)
    lb_soft = jax.nn.softmax(lb_ref[...], axis=0)
    lb = (lb_soft[0:1, :] + lb_soft[1:2, :]) - lb_soft[0:1, :]

    for c0 in range(0, D_MODEL, HGRN_COLS):
        cols = slice(c0, c0 + HGRN_COLS)
        q = _silu(_dot(h16, win_ref[:, c0:c0 + HGRN_COLS]))
        f = _dot(h16, win_ref[:, D_MODEL + c0:D_MODEL + c0 + HGRN_COLS])
        v = _dot(h16, win_ref[:, 2 * D_MODEL + c0:2 * D_MODEL + c0 + HGRN_COLS])
        g = _dot(h16, win_ref[:, 3 * D_MODEL + c0:3 * D_MODEL + c0 + HGRN_COLS])
        lbc = lb[:, cols]
        forget = lbc + (1.0 - lbc) * jax.nn.sigmoid(f)
        k = 1.0 - forget
        b = sum(_dot(tri_scr[...], p) for p in _split3(jnp.log(forget)))
        ends = [b[(c + 1) * C - 1:(c + 1) * C, :] for c in range(n_ch)]
        last = jnp.concatenate([jnp.broadcast_to(e, (C, HGRN_COLS)) for e in ends], axis=0)
        qe_scr[:, cols] = (q * jnp.exp(b)).astype(BF16)
        ke_scr[:, cols] = (k * jnp.exp(-b)).astype(BF16)
        kw_scr[:, cols] = (k * jnp.exp(last - b)).astype(BF16)
        v_scr[:, cols] = v.astype(BF16)
        g_scr[:, cols] = _silu(g)
        el_scr[:, cols] = jnp.exp(jnp.concatenate(ends, axis=0))

    rr = lax.broadcasted_iota(jnp.int32, (C, C), 0)
    cc = lax.broadcasted_iota(jnp.int32, (C, C), 1)
    tril = cc <= rr

    for c in range(n_ch):
        nb = c // ch_per_seq
        rows = slice(c * C, (c + 1) * C)
        for hh in range(HGRN_N_HEADS):
            sl = slice(hh * HD, (hh + 1) * HD)
            qe, v = qe_scr[rows, sl], v_scr[rows, sl]
            sc = jnp.where(tril, _dot_nt(qe, ke_scr[rows, sl]), 0.0)
            s_h = st_scr[nb, hh]
            o_scr[rows, sl] = _dot(sc.astype(BF16), v) + _dot_nt(qe, s_h.astype(BF16))
            st_scr[nb, hh] = el_scr[c:c + 1, sl] * s_h + _dot_tn(v, kw_scr[rows, sl])

    out = None
    for c0 in range(0, D_MODEL, HGRN_COLS):
        parts = []
        for h0 in range(c0, c0 + HGRN_COLS, HD):
            blk = o_scr[:, h0:h0 + HD]
            parts.append(blk * lax.rsqrt(jnp.mean(blk * blk, axis=-1, keepdims=True) + NORM_EPS))
        cols = slice(c0, c0 + HGRN_COLS)
        on = jnp.concatenate(parts, axis=1) * gnw_ref[:, cols] * g_scr[:, cols]
        part = _dot(on.astype(BF16), wo_ref[cols, :])
        out = part if out is None else out + part
    xo_ref[...] = (x + out).reshape(NB, TL, D_MODEL)

    @pl.when(i == n_i - 1)
    def _fin():
        for nb in range(NB):
            for hh in range(HGRN_N_HEADS):
                so_ref[nb, hh] = st_scr[nb, hh].T


def _hgrn_mixer(x, s0, p, *, NB, TL, C):
    n_seq, L, _ = x.shape
    has_init = s0 is not None
    R = NB * TL
    grid = (n_seq // NB, L // TL)
    st_block = (NB, HGRN_N_HEADS, HGRN_HEAD_DIM, HGRN_HEAD_DIM)
    in_specs = [pl.BlockSpec((NB, TL, D_MODEL), lambda b, i: (b, i, 0))]
    args = [x]
    if has_init:
        in_specs.append(pl.BlockSpec(st_block, lambda b, i: (b, 0, 0, 0)))
        args.append(s0)
    consts = [p["nw"], p["win"], p["lb"], p["gnw"], p["wo"]]
    for c in consts:
        in_specs.append(_layer_spec(c, 0) if c.ndim == 3 else _const_spec(c.shape))
    args += consts
    out_shape = (jax.ShapeDtypeStruct((n_seq, L, D_MODEL), F32),
                 jax.ShapeDtypeStruct((n_seq,) + st_block[1:], F32))
    out_specs = (pl.BlockSpec((NB, TL, D_MODEL), lambda b, i: (b, i, 0)),
                 pl.BlockSpec(st_block, lambda b, i: (b, 0, 0, 0)))
    scratch = [pltpu.VMEM(st_block, F32),
               pltpu.VMEM((R, R), BF16),
               pltpu.VMEM((R, D_MODEL), BF16),
               pltpu.VMEM((R, D_MODEL), BF16),
               pltpu.VMEM((R, D_MODEL), BF16),
               pltpu.VMEM((R, D_MODEL), BF16),
               pltpu.VMEM((R, D_MODEL), F32),
               pltpu.VMEM((R // C, D_MODEL), F32),
               pltpu.VMEM((R, D_MODEL), F32)]
    return pl.pallas_call(
        functools.partial(_hgrn_kernel, has_init=has_init, NB=NB, TL=TL, C=C),
        grid=grid, in_specs=in_specs, out_specs=out_specs, out_shape=out_shape,
        scratch_shapes=scratch,
        compiler_params=pltpu.CompilerParams(
            dimension_semantics=("arbitrary", "arbitrary"), vmem_limit_bytes=VMEM_LIMIT),
        name="hgrn_mixer_init" if has_init else "hgrn_mixer_zero",
    )(*args)


def _ffn_kernel(*refs, final, n_side):
    it = iter(refs)
    x_ref, nw_ref, wg_ref, wu_ref, wd_ref = [next(it) for _ in range(5)]
    fw_ref = next(it) if final else None
    side_in = [next(it) for _ in range(n_side)]
    o_ref = next(it)
    for w_ref in side_in:
        next(it)[...] = w_ref[...].astype(BF16)
    x = x_ref[...]
    h16 = _rms(x, nw_ref[...]).astype(BF16)
    act = _silu(_dot(h16, wg_ref[...])) * _dot(h16, wu_ref[...])
    y = x + _dot(act.astype(BF16), wd_ref[...])
    if final:
        y = _rms(y, fw_ref[...])
    o_ref[...] = y


def _ffn(x2d, nw, wg, wu, wd, final_w, side, *, TM):
    rows = x2d.shape[0]
    final = final_w is not None
    n_steps = rows // TM
    side_in, side_out, side_shapes = _side_cast(side, n_steps, lambda i: i)
    consts = [nw, wg, wu, wd] + ([final_w] if final else [])
    in_specs = [pl.BlockSpec((TM, D_MODEL), lambda i: (i, 0))] + [_const_spec(c.shape) for c in consts] + side_in
    outs = pl.pallas_call(
        functools.partial(_ffn_kernel, final=final, n_side=len(side)),
        grid=(n_steps,),
        in_specs=in_specs,
        out_specs=[pl.BlockSpec((TM, D_MODEL), lambda i: (i, 0))] + side_out,
        out_shape=[jax.ShapeDtypeStruct((rows, D_MODEL), F32)] + side_shapes,
        compiler_params=pltpu.CompilerParams(
            dimension_semantics=("arbitrary",), vmem_limit_bytes=VMEM_LIMIT),
        name="swiglu_final" if final else "swiglu",
    )(x2d, *consts, *[w for w, _ in side])
    return outs[0], list(outs[1:])


def _tiles(n_seq, L):
    if L >= SSD_LONG_TILE:
        return dict(NB=1, TL_SSD=SSD_LONG_TILE, TL_HGRN=HGRN_LONG_TILE, C=SCAN_CHUNK, TM=FFN_ROWS)
    return dict(NB=SHORT_SEQ_PER_STEP, TL_SSD=L, TL_HGRN=L, C=min(L, SCAN_CHUNK), TM=min(FFN_ROWS, n_seq * L))


def _trunk(x, s_ssd, cv, s_hgrn, ssd_p, hgrn_p, ffn_nw, final_w, w16, raw):
    n_seq, L, _ = x.shape
    t = _tiles(n_seq, L)
    s0 = None if s_ssd is None else s_ssd.reshape(n_seq, SSD_D_INNER, SSD_D_STATE)

    def missing(names):
        return [k for k in names if k not in w16]

    need = missing(("wg0", "wu0", "wd0"))
    x, s_new, cv_new, cast = _ssd_mixer(x, s0, cv, ssd_p, [raw[k] for k in need],
                                        NB=t["NB"], TL=t["TL_SSD"], C=t["C"])
    w16.update(zip(need, cast))
    need = missing(("hgrn_in", "hgrn_out", "wg1", "wu1", "wd1"))
    x, cast = _ffn(x.reshape(n_seq * L, D_MODEL), ffn_nw[0], w16["wg0"], w16["wu0"], w16["wd0"], None,
                   [raw[k] for k in need], TM=t["TM"])
    w16.update(zip(need, cast))
    x, h_new = _hgrn_mixer(x.reshape(n_seq, L, D_MODEL), s_hgrn,
                           dict(hgrn_p, win=w16["hgrn_in"], wo=w16["hgrn_out"]),
                           NB=t["NB"], TL=t["TL_HGRN"], C=t["C"])
    y, _ = _ffn(x.reshape(n_seq * L, D_MODEL), ffn_nw[1], w16["wg1"], w16["wu1"], w16["wd1"], final_w,
                [], TM=t["TM"])
    s_new = s_new.reshape(1, n_seq, SSD_N_HEADS, SSD_HEAD_DIM, SSD_D_STATE)
    return y.reshape(n_seq, L, D_MODEL), s_new, cv_new[None], h_new[None]


def kernel(x_prompt, x_sample, state_ssd, cache_conv, state_hgrn, ssd_norm_w, ssd_in_w, ssd_conv_w, ssd_conv_b, ssd_dt_bias, ssd_A_log, ssd_D, ssd_gnorm_w, ssd_out_w, hgrn_norm_w, hgrn_in_w, hgrn_lower_bounds, hgrn_gnorm_w, hgrn_out_w, ffn_norm_w, ffn_w_gate, ffn_w_up, ffn_w_down, final_norm_w):
    pair_order = jnp.concatenate([jnp.arange(0, SSD_N_HEADS, 2), jnp.arange(1, SSD_N_HEADS, 2)])
    ssd_p = {
        "nw": ssd_norm_w[0].reshape(1, D_MODEL),
        "win": ssd_in_w.astype(BF16),
        "cw": ssd_conv_w,
        "cb": ssd_conv_b[0].reshape(1, SSD_CONV_DIM),
        "dtb": ssd_dt_bias[0].reshape(1, SSD_N_HEADS),
        "dtbT": ssd_dt_bias[0][pair_order].reshape(SSD_N_HEADS, 1),
        "alog": ssd_A_log[0],
        "dexp": jnp.repeat(ssd_D[0], SSD_HEAD_DIM).reshape(1, SSD_D_INNER),
        "gnw": ssd_gnorm_w[0].reshape(1, SSD_D_INNER),
        "wo": ssd_out_w.astype(BF16),
    }
    hgrn_p = {
        "nw": hgrn_norm_w[0].reshape(1, D_MODEL),
        "lb": hgrn_lower_bounds,
        "gnw": jnp.tile(hgrn_gnorm_w[0], HGRN_N_HEADS).reshape(1, D_MODEL),
    }
    ffn_nw = ffn_norm_w.reshape(2, 1, D_MODEL)
    final_w = final_norm_w.reshape(1, D_MODEL)
    raw = {"wg0": (ffn_w_gate, 0), "wu0": (ffn_w_up, 0), "wd0": (ffn_w_down, 0),
           "hgrn_in": (hgrn_in_w, 0), "hgrn_out": (hgrn_out_w, 0),
           "wg1": (ffn_w_gate, 1), "wu1": (ffn_w_up, 1), "wd1": (ffn_w_down, 1)}
    w16 = {}

    y_p, ssd_s_p, conv_p, hgrn_s_p = _trunk(
        x_prompt, None, None, None, ssd_p, hgrn_p, ffn_nw, final_w, w16, raw)
    y_s, ssd_s_s, conv_s, hgrn_s_s = _trunk(
        x_sample, state_ssd[0], cache_conv[0], state_hgrn[0], ssd_p, hgrn_p, ffn_nw, final_w, w16, raw)
    return (y_p, y_s, ssd_s_p, conv_p, hgrn_s_p, ssd_s_s, conv_s, hgrn_s_s)
```

```python
import functools

import jax
import jax.numpy as jnp
from jax import lax
from jax.experimental import pallas as pl
from jax.experimental.pallas import tpu as pltpu

F32 = jnp.float32
BF16 = jnp.bfloat16

D_MODEL = 1024
NORM_EPS = 1e-6

SSD_D_INNER = 2048
SSD_HEAD_DIM = 64
SSD_N_HEADS = 32
SSD_N_GROUPS = 4
SSD_D_STATE = 128
SSD_CONV_W = 4
SSD_CONV_DIM = 3072
SSD_NORM_GROUP = 512
SSD_N_PAIRS = SSD_N_HEADS // 2
SSD_XBC_OFF = SSD_D_INNER
SSD_DT_OFF = SSD_D_INNER + SSD_CONV_DIM

HGRN_HEAD_DIM = 128
HGRN_N_HEADS = 8

FFN_HIDDEN = 2816

LANES = 128
SUBLANES = 8
BF16_TILE_ROWS = 16
SCAN_CHUNK = 64
SSD_LONG_TILE = 256
HGRN_LONG_TILE = 256
SHORT_SEQ_PER_STEP = 4
FFN_ROWS = 512
CONV_COLS = 512
HGRN_COLS = 256
VMEM_LIMIT = 56 * 1024 * 1024


def _dot(a, b):
    return jnp.dot(a, b, preferred_element_type=F32)


def _dot_nt(a, b):
    return lax.dot_general(a, b, (((1,), (1,)), ((), ())), preferred_element_type=F32)


def _dot_tn(a, b):
    return lax.dot_general(a, b, (((0,), (0,)), ((), ())), preferred_element_type=F32)


def _split3(a):
    hi = a.astype(BF16)
    r1 = a - hi.astype(F32)
    mid = r1.astype(BF16)
    lo = (r1 - mid.astype(F32)).astype(BF16)
    return hi, mid, lo


def _stack3(a):
    hi = a.astype(BF16).astype(F32)
    r1 = a - hi
    mid = r1.astype(BF16).astype(F32)
    return jnp.concatenate([hi, mid, r1 - mid, jnp.zeros_like(a)], axis=1).astype(BF16)


def _rms(x, w):
    return x * lax.rsqrt(jnp.mean(x * x, axis=-1, keepdims=True) + NORM_EPS) * w


def _silu(x):
    return x * jax.nn.sigmoid(x)


def _shift_rows(u, first_rows):
    rows, cols = u.shape
    k = first_rows.shape[0]
    nv = rows // SUBLANES
    r = pltpu.roll(u.reshape(nv, SUBLANES, cols), k, axis=1)
    first = jnp.concatenate([first_rows, jnp.zeros((SUBLANES - k, cols), u.dtype)], axis=0)
    prev = jnp.concatenate([first[None], r[:nv - 1]], axis=0)
    sub = lax.broadcasted_iota(jnp.int32, (nv, SUBLANES, cols), 1)
    return jnp.where(sub < k, prev, r).reshape(rows, cols)


def _chunk_triangle(R, C):
    rr = lax.broadcasted_iota(jnp.int32, (R, R), 0)
    cc = lax.broadcasted_iota(jnp.int32, (R, R), 1)
    return ((cc <= rr) & ((rr & -C) == (cc & -C))).astype(BF16)


def _ssd_kernel(*refs, has_init, NB, TL, C, n_side):
    it = iter(refs)
    x_ref = next(it)
    s0_ref = cv0_ref = None
    if has_init:
        s0_ref = next(it)
        cv0_ref = next(it)
    (nw_ref, win_ref, cw_ref, cb_ref, dtb_ref, dtbT_ref,
     alog_ref, alogp_ref, dexp_ref, gnw_ref, wo_ref) = [next(it) for _ in range(11)]
    side_in = [next(it) for _ in range(n_side)]
    xo_ref = next(it)
    so_ref = next(it)
    cvo_ref = next(it)
    for w_ref in side_in:
        next(it)[...] = w_ref[...].astype(BF16)
    (st_scr, hist_scr, z_scr, xbc_scr, y_scr, e_scr, tri_scr,
     colc_scr, we_scr) = [next(it) for _ in range(9)]
    es_scr = colcs_scr = None
    if 2 * C != LANES:
        es_scr = next(it)
        colcs_scr = next(it)

    R = NB * TL
    n_ch = R // C
    ch_per_seq = TL // C
    W2 = 2 * C
    i = pl.program_id(1)
    n_i = pl.num_programs(1)

    @pl.when(i == 0)
    def _init():
        ek = lax.broadcasted_iota(jnp.int32, e_scr.shape, 0)
        el = lax.broadcasted_iota(jnp.int32, e_scr.shape, 1)
        e_scr[...] = ((ek < 3 * SSD_N_HEADS) & ((ek & (SSD_N_HEADS - 1)) == el // SSD_HEAD_DIM)).astype(BF16)
        if es_scr is not None:
            ek = lax.broadcasted_iota(jnp.int32, es_scr.shape, 0)
            el = lax.broadcasted_iota(jnp.int32, es_scr.shape, 1)
            es_scr[...] = ((ek < 3 * SSD_N_HEADS) & ((ek & (SSD_N_HEADS - 1)) == el // C)).astype(BF16)
        tri_scr[...] = _chunk_triangle(R, C)
        hist_scr[...] = jnp.zeros_like(hist_scr)
        if has_init:
            for nb in range(NB):
                for g in range(SSD_N_GROUPS):
                    st_scr[nb, g] = s0_ref[nb, g * SSD_NORM_GROUP:(g + 1) * SSD_NORM_GROUP, :].T
                for k in range(SSD_CONV_DIM // CONV_COLS):
                    hist_scr[nb, k, SUBLANES - 3:SUBLANES, :] = cv0_ref[nb, :, k * CONV_COLS:(k + 1) * CONV_COLS]
        else:
            st_scr[...] = jnp.zeros_like(st_scr)

    x = x_ref[...].reshape(R, D_MODEL)
    h16 = _rms(x, nw_ref[...]).astype(BF16)

    for cb0 in range(0, SSD_CONV_DIM, CONV_COLS):
        cols = slice(cb0, cb0 + CONV_COLS)
        xr = _dot(h16, win_ref[:, SSD_XBC_OFF + cb0:SSD_XBC_OFF + cb0 + CONV_COLS])
        w0, w1, w2, w3 = (cw_ref[k:k + 1, cols] for k in range(SSD_CONV_W))
        for nb in range(NB):
            xn = xr[nb * TL:(nb + 1) * TL]
            hm = hist_scr[nb, cb0 // CONV_COLS]
            sx = _shift_rows(xn, hm[7:8])
            a2 = _shift_rows(w1 * xn + w0 * sx, w1 * hm[6:8] + w0 * hm[5:7])
            xbc_scr[nb * TL:(nb + 1) * TL, cols] = _silu(w3 * xn + w2 * sx + a2 + cb_ref[:, cols])
            hist_scr[nb, cb0 // CONV_COLS] = xn[TL - SUBLANES:TL]
    for zb0 in range(0, SSD_D_INNER, CONV_COLS):
        z_scr[:, zb0:zb0 + CONV_COLS] = _silu(_dot(h16, win_ref[:, zb0:zb0 + CONV_COLS]))

    dt_raw = _dot(h16, win_ref[:, SSD_DT_OFF:])
    dt = jax.nn.softplus(dt_raw + dtb_ref[...])
    pi = lax.broadcasted_iota(jnp.int32, (SSD_N_HEADS, SSD_N_HEADS), 0)
    pj = lax.broadcasted_iota(jnp.int32, (SSD_N_HEADS, SSD_N_HEADS), 1)
    pick = (pj == jnp.where(pi < SSD_N_PAIRS, 2 * pi, 2 * (pi - SSD_N_PAIRS) + 1)).astype(BF16)
    dtT = jax.nn.softplus(sum(_dot_nt(pick, p) for p in _split3(dt_raw)) + dtbT_ref[...])
    dtTp = jnp.concatenate(
        [jnp.concatenate([dtT[0:SSD_N_PAIRS, c * C:(c + 1) * C],
                          dtT[SSD_N_PAIRS:, c * C:(c + 1) * C]], axis=1) for c in range(n_ch)],
        axis=0)
    a_row = -jnp.exp(alog_ref[...])
    a_pair = -jnp.exp(alogp_ref[...])
    cum = sum(_dot(tri_scr[...], p) for p in _split3(dt * a_row))
    last = jnp.concatenate(
        [jnp.broadcast_to(cum[(c + 1) * C - 1:(c + 1) * C, :], (C, SSD_N_HEADS)) for c in range(n_ch)], axis=0)
    wdt = jnp.exp(last - cum) * dt
    cum3 = _stack3(cum)
    colc_scr[...] = _dot(cum3, e_scr[...])
    we_scr[...] = _dot(_stack3(wdt), e_scr[...])
    if colcs_scr is not None:
        colcs_scr[...] = _dot(cum3, es_scr[...])
    r2 = lax.broadcasted_iota(jnp.int32, (W2, W2), 0)
    c2 = lax.broadcasted_iota(jnp.int32, (W2, W2), 1)
    triT2_16 = ((r2 <= c2) & ((r2 < C) == (c2 < C))).astype(BF16)
    aTp = dtTp * jnp.concatenate([a_pair] * n_ch, axis=0)
    cumTp = sum(_dot(p, triT2_16) for p in _split3(aTp))

    tp = lax.broadcasted_iota(jnp.int32, (C, W2), 0)
    sp = lax.broadcasted_iota(jnp.int32, (C, W2), 1)
    tril_pair = jnp.where(sp < C, sp, sp - C) <= tp
    left_e = lax.broadcasted_iota(jnp.int32, (C, LANES), 1) < SSD_HEAD_DIM

    for c in range(n_ch):
        nb = c // ch_per_seq
        rows = slice(c * C, (c + 1) * C)
        dtTp_c = dtTp[c * SSD_N_PAIRS:(c + 1) * SSD_N_PAIRS, :]
        cumTp_c = cumTp[c * SSD_N_PAIRS:(c + 1) * SSD_N_PAIRS, :]
        for g in range(SSD_N_GROUPS):
            bo = SSD_D_INNER + g * SSD_D_STATE
            co = SSD_D_INNER + SSD_N_GROUPS * SSD_D_STATE + g * SSD_D_STATE
            Bg16 = xbc_scr[rows, bo:bo + SSD_D_STATE].astype(BF16)
            Cg16 = xbc_scr[rows, co:co + SSD_D_STATE].astype(BF16)
            cbp = _dot_nt(Cg16, jnp.concatenate([Bg16, Bg16], axis=0))
            cbm = jnp.where(tril_pair, cbp, 0.0)
            st_g = st_scr[nb, g]
            y_inter = _dot(Cg16, st_g.astype(BF16))
            xw_parts = []
            el_parts = []
            for q in range(4):
                jp = g * 4 + q
                lanes = slice(jp * LANES, (jp + 1) * LANES)
                colc_e = colc_scr[rows, lanes]
                if W2 == LANES:
                    colc_l = colc_e
                else:
                    colc_l = colcs_scr[rows, jp * W2:(jp + 1) * W2]
                decay = jnp.exp(jnp.minimum(colc_l - cumTp_c[jp:jp + 1, :], 0.0))
                mp = cbm * decay * dtTp_c[jp:jp + 1, :]
                xs_p = xbc_scr[rows, lanes]
                rhs = jnp.concatenate(
                    [jnp.where(left_e, xs_p, 0.0).astype(BF16),
                     jnp.where(left_e, 0.0, xs_p).astype(BF16)], axis=0)
                e_c = jnp.exp(colc_e)
                y_scr[rows, lanes] = _dot(mp.astype(BF16), rhs) + e_c * y_inter[:, q * LANES:(q + 1) * LANES]
                xw_parts.append((xs_p * we_scr[rows, lanes]).astype(BF16))
                el_parts.append(e_c[C - 1:C, :])
            xw_g = jnp.concatenate(xw_parts, axis=1)
            el_g = jnp.concatenate(el_parts, axis=1)
            st_scr[nb, g] = el_g * st_g + _dot_tn(Bg16, xw_g)

    out = None
    for g in range(SSD_N_GROUPS):
        cols = slice(g * SSD_NORM_GROUP, (g + 1) * SSD_NORM_GROUP)
        yg = (y_scr[:, cols] + dexp_ref[:, cols] * xbc_scr[:, cols]) * z_scr[:, cols]
        yn = yg * lax.rsqrt(jnp.mean(yg * yg, axis=-1, keepdims=True) + NORM_EPS) * gnw_ref[:, cols]
        part = _dot(yn.astype(BF16), wo_ref[cols, :])
        out = part if out is None else out + part
    xo_ref[...] = (x + out).reshape(NB, TL, D_MODEL)

    @pl.when(i == n_i - 1)
    def _fin():
        for nb in range(NB):
            for g in range(SSD_N_GROUPS):
                so_ref[nb, g * SSD_NORM_GROUP:(g + 1) * SSD_NORM_GROUP, :] = st_scr[nb, g].T
            for k in range(SSD_CONV_DIM // CONV_COLS):
                cvo_ref[nb, :, k * CONV_COLS:(k + 1) * CONV_COLS] = hist_scr[nb, k, SUBLANES - 3:SUBLANES, :]


def _const_spec(shape):
    nd = len(shape)
    return pl.BlockSpec(shape, lambda *_: (0,) * nd, pipeline_mode=pl.Buffered(1))


def _layer_spec(arr, layer):
    nd = arr.ndim - 1
    return pl.BlockSpec((None,) + arr.shape[1:], lambda *_: (layer,) + (0,) * nd,
                        pipeline_mode=pl.Buffered(1))


def _side_cast(side, n_steps, step_of):
    in_specs, out_specs, out_shapes = [], [], []
    for w, layer in side:
        rows, cols = w.shape[1:]
        n_blk = max(n for n in range(1, n_steps + 1) if rows % n == 0 and (rows // n) % BF16_TILE_ROWS == 0)
        br = rows // n_blk
        in_specs.append(pl.BlockSpec(
            (None, br, cols), lambda *g, layer=layer, n_blk=n_blk: (layer, jnp.minimum(step_of(*g), n_blk - 1), 0)))
        out_specs.append(pl.BlockSpec(
            (br, cols), lambda *g, n_blk=n_blk: (jnp.minimum(step_of(*g), n_blk - 1), 0)))
        out_shapes.append(jax.ShapeDtypeStruct((rows, cols), BF16))
    return in_specs, out_specs, out_shapes


def _ssd_mixer(x, s0, cv0, p, side, *, NB, TL, C):
    n_seq, L, _ = x.shape
    has_init = s0 is not None
    R = NB * TL
    grid = (n_seq // NB, L // TL)
    side_in, side_out, side_shapes = _side_cast(side, grid[0] * grid[1], lambda b, i: b * grid[1] + i)
    in_specs = [pl.BlockSpec((NB, TL, D_MODEL), lambda b, i: (b, i, 0))]
    args = [x]
    if has_init:
        in_specs += [pl.BlockSpec((NB, SSD_D_INNER, SSD_D_STATE), lambda b, i: (b, 0, 0)),
                     pl.BlockSpec((NB, 3, SSD_CONV_DIM), lambda b, i: (b, 0, 0))]
        args += [s0, cv0]
    alog_pair = jnp.concatenate(
        [jnp.broadcast_to(p["alog"][0::2, None], (SSD_N_PAIRS, C)),
         jnp.broadcast_to(p["alog"][1::2, None], (SSD_N_PAIRS, C))], axis=1)
    consts = [p["nw"], p["win"], p["cw"], p["cb"], p["dtb"], p["dtbT"],
              p["alog"].reshape(1, SSD_N_HEADS), alog_pair, p["dexp"], p["gnw"], p["wo"]]
    for c in consts:
        in_specs.append(_layer_spec(c, 0) if c.ndim == 3 else _const_spec(c.shape))
    args += consts
    in_specs += side_in
    args += [w for w, _ in side]
    out_shape = [jax.ShapeDtypeStruct((n_seq, L, D_MODEL), F32),
                 jax.ShapeDtypeStruct((n_seq, SSD_D_INNER, SSD_D_STATE), F32),
                 jax.ShapeDtypeStruct((n_seq, 3, SSD_CONV_DIM), F32)] + side_shapes
    out_specs = [pl.BlockSpec((NB, TL, D_MODEL), lambda b, i: (b, i, 0)),
                 pl.BlockSpec((NB, SSD_D_INNER, SSD_D_STATE), lambda b, i: (b, 0, 0)),
                 pl.BlockSpec((NB, 3, SSD_CONV_DIM), lambda b, i: (b, 0, 0))] + side_out
    scratch = [pltpu.VMEM((NB, SSD_N_GROUPS, SSD_D_STATE, SSD_NORM_GROUP), F32),
               pltpu.VMEM((NB, SSD_CONV_DIM // CONV_COLS, SUBLANES, CONV_COLS), F32),
               pltpu.VMEM((R, SSD_D_INNER), F32),
               pltpu.VMEM((R, SSD_CONV_DIM), F32),
               pltpu.VMEM((R, SSD_D_INNER), F32),
               pltpu.VMEM((LANES, SSD_D_INNER), BF16),
               pltpu.VMEM((R, R), BF16),
               pltpu.VMEM((R, SSD_D_INNER), F32),
               pltpu.VMEM((R, SSD_D_INNER), F32)]
    if 2 * C != LANES:
        scratch += [pltpu.VMEM((LANES, SSD_N_PAIRS * 2 * C), BF16),
                    pltpu.VMEM((R, SSD_N_PAIRS * 2 * C), F32)]
    outs = pl.pallas_call(
        functools.partial(_ssd_kernel, has_init=has_init, NB=NB, TL=TL, C=C, n_side=len(side)),
        grid=grid, in_specs=in_specs, out_specs=out_specs, out_shape=out_shape,
        scratch_shapes=scratch,
        compiler_params=pltpu.CompilerParams(
            dimension_semantics=("arbitrary", "arbitrary"), vmem_limit_bytes=VMEM_LIMIT),
        name="ssd_mixer_init" if has_init else "ssd_mixer_zero",
    )(*args)
    return outs[0], outs[1], outs[2], list(outs[3:])


def _hgrn_kernel(*refs, has_init, NB, TL, C):
    it = iter(refs)
    x_ref = next(it)
    s0_ref = next(it) if has_init else None
    nw_ref, win_ref, lb_ref, gnw_ref, wo_ref = [next(it) for _ in range(5)]
    xo_ref = next(it)
    so_ref = next(it)
    st_scr, tri_scr, qe_scr, ke_scr, kw_scr, v_scr, g_scr, elT_scr, o_scr = [next(it) for _ in range(9)]

    R = NB * TL
    n_ch = R // C
    ch_per_seq = TL // C
    HD = HGRN_HEAD_DIM
    i = pl.program_id(1)
    n_i = pl.num_programs(1)

    @pl.when(i == 0)
    def _init():
        tri_scr[...] = _chunk_triangle(R, C)
        if has_init:
            st_scr[...] = s0_ref[...]
        else:
            st_scr[...] = jnp.zeros_like(st_scr)

    x = x_ref[...].reshape(R, D_MODEL)
    h16 = _rms(x, nw_ref[...]).astype(BF16)
    lb_soft = jax.nn.softmax(lb_ref[...], axis=0)
    lb = (lb_soft[0:1, :] + lb_soft[1:2, :]) - lb_soft[0:1, :]

    for c0 in range(0, D_MODEL, HGRN_COLS):
        cols = slice(c0, c0 + HGRN_COLS)
        q = _silu(_dot(h16, win_ref[:, c0:c0 + HGRN_COLS]))
        f = _dot(h16, win_ref[:, D_MODEL + c0:D_MODEL + c0 + HGRN_COLS])
        v = _dot(h16, win_ref[:, 2 * D_MODEL + c0:2 * D_MODEL + c0 + HGRN_COLS])
        g = _dot(h16, win_ref[:, 3 * D_MODEL + c0:3 * D_MODEL + c0 + HGRN_COLS])
        lbc = lb[:, cols]
        forget = lbc + (1.0 - lbc) * jax.nn.sigmoid(f)
        k = 1.0 - forget
        b = sum(_dot(tri_scr[...], p) for p in _split3(jnp.log(forget)))
        ends = [b[(c + 1) * C - 1:(c + 1) * C, :] for c in range(n_ch)]
        last = jnp.concatenate([jnp.broadcast_to(e, (C, HGRN_COLS)) for e in ends], axis=0)
        qe_scr[:, cols] = (q * jnp.exp(b)).astype(BF16)
        ke_scr[:, cols] = (k * jnp.exp(-b)).astype(BF16)
        kw_scr[:, cols] = (k * jnp.exp(last - b)).astype(BF16)
        v_scr[:, cols] = v.astype(BF16)
        g_scr[:, cols] = _silu(g)
        el = jnp.exp(jnp.concatenate(ends + [jnp.zeros((LANES - n_ch, HGRN_COLS), F32)], axis=0))
        elT_scr[cols, :] = el.T

    rr = lax.broadcasted_iota(jnp.int32, (C, C), 0)
    cc = lax.broadcasted_iota(jnp.int32, (C, C), 1)
    tril = cc <= rr

    for c in range(n_ch):
        nb = c // ch_per_seq
        rows = slice(c * C, (c + 1) * C)
        for hh in range(HGRN_N_HEADS):
            sl = slice(hh * HD, (hh + 1) * HD)
            qe, v = qe_scr[rows, sl], v_scr[rows, sl]
            sc = jnp.where(tril, _dot_nt(qe, ke_scr[rows, sl]), 0.0)
            s_h = st_scr[nb, hh]
            o_scr[rows, sl] = _dot(jnp.concatenate([qe, sc.astype(BF16)], axis=1),
                                   jnp.concatenate([s_h.astype(BF16), v], axis=0))
            st_scr[nb, hh] = elT_scr[sl, c:c + 1] * s_h + _dot_tn(kw_scr[rows, sl], v)

    out = None
    for c0 in range(0, D_MODEL, HGRN_COLS):
        parts = []
        for h0 in range(c0, c0 + HGRN_COLS, HD):
            blk = o_scr[:, h0:h0 + HD]
            parts.append(blk * lax.rsqrt(jnp.mean(blk * blk, axis=-1, keepdims=True) + NORM_EPS))
        cols = slice(c0, c0 + HGRN_COLS)
        on = jnp.concatenate(parts, axis=1) * gnw_ref[:, cols] * g_scr[:, cols]
        part = _dot(on.astype(BF16), wo_ref[cols, :])
        out = part if out is None else out + part
    xo_ref[...] = (x + out).reshape(NB, TL, D_MODEL)

    @pl.when(i == n_i - 1)
    def _fin():
        so_ref[...] = st_scr[...]


def _hgrn_mixer(x, s0, p, *, NB, TL, C):
    n_seq, L, _ = x.shape
    has_init = s0 is not None
    R = NB * TL
    grid = (n_seq // NB, L // TL)
    st_block = (NB, HGRN_N_HEADS, HGRN_HEAD_DIM, HGRN_HEAD_DIM)
    in_specs = [pl.BlockSpec((NB, TL, D_MODEL), lambda b, i: (b, i, 0))]
    args = [x]
    if has_init:
        in_specs.append(pl.BlockSpec(st_block, lambda b, i: (b, 0, 0, 0)))
        args.append(s0)
    consts = [p["nw"], p["win"], p["lb"], p["gnw"], p["wo"]]
    for c in consts:
        in_specs.append(_layer_spec(c, 0) if c.ndim == 3 else _const_spec(c.shape))
    args += consts
    out_shape = (jax.ShapeDtypeStruct((n_seq, L, D_MODEL), F32),
                 jax.ShapeDtypeStruct((n_seq,) + st_block[1:], F32))
    out_specs = (pl.BlockSpec((NB, TL, D_MODEL), lambda b, i: (b, i, 0)),
                 pl.BlockSpec(st_block, lambda b, i: (b, 0, 0, 0)))
    scratch = [pltpu.VMEM(st_block, F32),
               pltpu.VMEM((R, R), BF16),
               pltpu.VMEM((R, D_MODEL), BF16),
               pltpu.VMEM((R, D_MODEL), BF16),
               pltpu.VMEM((R, D_MODEL), BF16),
               pltpu.VMEM((R, D_MODEL), BF16),
               pltpu.VMEM((R, D_MODEL), F32),
               pltpu.VMEM((D_MODEL, LANES), F32),
               pltpu.VMEM((R, D_MODEL), F32)]
    return pl.pallas_call(
        functools.partial(_hgrn_kernel, has_init=has_init, NB=NB, TL=TL, C=C),
        grid=grid, in_specs=in_specs, out_specs=out_specs, out_shape=out_shape,
        scratch_shapes=scratch,
        compiler_params=pltpu.CompilerParams(
            dimension_semantics=("arbitrary", "arbitrary"), vmem_limit_bytes=VMEM_LIMIT),
        name="hgrn_mixer_init" if has_init else "hgrn_mixer_zero",
    )(*args)


def _ffn_kernel(*refs, final, n_side):
    it = iter(refs)
    x_ref, nw_ref, wg_ref, wu_ref, wd_ref = [next(it) for _ in range(5)]
    fw_ref = next(it) if final else None
    side_in = [next(it) for _ in range(n_side)]
    o_ref = next(it)
    for w_ref in side_in:
        next(it)[...] = w_ref[...].astype(BF16)
    x = x_ref[...]
    h16 = _rms(x, nw_ref[...]).astype(BF16)
    act = _silu(_dot(h16, wg_ref[...])) * _dot(h16, wu_ref[...])
    y = x + _dot(act.astype(BF16), wd_ref[...])
    if final:
        y = _rms(y, fw_ref[...])
    o_ref[...] = y


def _ffn(x2d, nw, wg, wu, wd, final_w, side, *, TM):
    rows = x2d.shape[0]
    final = final_w is not None
    n_steps = rows // TM
    side_in, side_out, side_shapes = _side_cast(side, n_steps, lambda i: i)
    consts = [nw, wg, wu, wd] + ([final_w] if final else [])
    in_specs = [pl.BlockSpec((TM, D_MODEL), lambda i: (i, 0))] + [_const_spec(c.shape) for c in consts] + side_in
    outs = pl.pallas_call(
        functools.partial(_ffn_kernel, final=final, n_side=len(side)),
        grid=(n_steps,),
        in_specs=in_specs,
        out_specs=[pl.BlockSpec((TM, D_MODEL), lambda i: (i, 0))] + side_out,
        out_shape=[jax.ShapeDtypeStruct((rows, D_MODEL), F32)] + side_shapes,
        compiler_params=pltpu.CompilerParams(
            dimension_semantics=("arbitrary",), vmem_limit_bytes=VMEM_LIMIT),
        name="swiglu_final" if final else "swiglu",
    )(x2d, *consts, *[w for w, _ in side])
    return outs[0], list(outs[1:])


def _tiles(n_seq, L):
    if L >= SSD_LONG_TILE:
        return dict(NB=1, TL_SSD=SSD_LONG_TILE, TL_HGRN=HGRN_LONG_TILE, C=SCAN_CHUNK, TM=FFN_ROWS)
    return dict(NB=SHORT_SEQ_PER_STEP, TL_SSD=L, TL_HGRN=L, C=min(L, SCAN_CHUNK), TM=min(FFN_ROWS, n_seq * L))


def _trunk(x, s_ssd, cv, s_hgrn, ssd_p, hgrn_p, ffn_nw, final_w, w16, raw):
    n_seq, L, _ = x.shape
    t = _tiles(n_seq, L)
    s0 = None if s_ssd is None else s_ssd.reshape(n_seq, SSD_D_INNER, SSD_D_STATE)

    def missing(names):
        return [k for k in names if k not in w16]

    need = missing(("wg0", "wu0", "wd0"))
    x, s_new, cv_new, cast = _ssd_mixer(x, s0, cv, ssd_p, [raw[k] for k in need],
                                        NB=t["NB"], TL=t["TL_SSD"], C=t["C"])
    w16.update(zip(need, cast))
    need = missing(("hgrn_in", "hgrn_out", "wg1", "wu1", "wd1"))
    x, cast = _ffn(x.reshape(n_seq * L, D_MODEL), ffn_nw[0], w16["wg0"], w16["wu0"], w16["wd0"], None,
                   [raw[k] for k in need], TM=t["TM"])
    w16.update(zip(need, cast))
    x, h_new = _hgrn_mixer(x.reshape(n_seq, L, D_MODEL), s_hgrn,
                           dict(hgrn_p, win=w16["hgrn_in"], wo=w16["hgrn_out"]),
                           NB=t["NB"], TL=t["TL_HGRN"], C=t["C"])
    y, _ = _ffn(x.reshape(n_seq * L, D_MODEL), ffn_nw[1], w16["wg1"], w16["wu1"], w16["wd1"], final_w,
                [], TM=t["TM"])
    s_new = s_new.reshape(1, n_seq, SSD_N_HEADS, SSD_HEAD_DIM, SSD_D_STATE)
    return y.reshape(n_seq, L, D_MODEL), s_new, cv_new[None], h_new[None]


def kernel(x_prompt, x_sample, state_ssd, cache_conv, state_hgrn, ssd_norm_w, ssd_in_w, ssd_conv_w, ssd_conv_b, ssd_dt_bias, ssd_A_log, ssd_D, ssd_gnorm_w, ssd_out_w, hgrn_norm_w, hgrn_in_w, hgrn_lower_bounds, hgrn_gnorm_w, hgrn_out_w, ffn_norm_w, ffn_w_gate, ffn_w_up, ffn_w_down, final_norm_w):
    pair_order = jnp.concatenate([jnp.arange(0, SSD_N_HEADS, 2), jnp.arange(1, SSD_N_HEADS, 2)])
    ssd_p = {
        "nw": ssd_norm_w[0].reshape(1, D_MODEL),
        "win": ssd_in_w.astype(BF16),
        "cw": ssd_conv_w,
        "cb": ssd_conv_b[0].reshape(1, SSD_CONV_DIM),
        "dtb": ssd_dt_bias[0].reshape(1, SSD_N_HEADS),
        "dtbT": ssd_dt_bias[0][pair_order].reshape(SSD_N_HEADS, 1),
        "alog": ssd_A_log[0],
        "dexp": jnp.repeat(ssd_D[0], SSD_HEAD_DIM).reshape(1, SSD_D_INNER),
        "gnw": ssd_gnorm_w[0].reshape(1, SSD_D_INNER),
        "wo": ssd_out_w.astype(BF16),
    }
    hgrn_p = {
        "nw": hgrn_norm_w[0].reshape(1, D_MODEL),
        "lb": hgrn_lower_bounds,
        "gnw": jnp.tile(hgrn_gnorm_w[0], HGRN_N_HEADS).reshape(1, D_MODEL),
    }
    ffn_nw = ffn_norm_w.reshape(2, 1, D_MODEL)
    final_w = final_norm_w.reshape(1, D_MODEL)
    raw = {"wg0": (ffn_w_gate, 0), "wu0": (ffn_w_up, 0), "wd0": (ffn_w_down, 0),
           "hgrn_in": (hgrn_in_w, 0), "hgrn_out": (hgrn_out_w, 0),
           "wg1": (ffn_w_gate, 1), "wu1": (ffn_w_up, 1), "wd1": (ffn_w_down, 1)}
    w16 = {}

    y_p, ssd_s_p, conv_p, hgrn_s_p = _trunk(
        x_prompt, None, None, None, ssd_p, hgrn_p, ffn_nw, final_w, w16, raw)
    y_s, ssd_s_s, conv_s, hgrn_s_s = _trunk(
        x_sample, state_ssd[0], cache_conv[0], state_hgrn[0], ssd_p, hgrn_p, ffn_nw, final_w, w16, raw)
    return (y_p, y_s, ssd_s_p, conv_p, hgrn_s_p, ssd_s_s, conv_s, hgrn_s_s)
```

```python
import functools

import jax
import jax.numpy as jnp
from jax import lax
from jax.experimental import pallas as pl
from jax.experimental.pallas import tpu as pltpu

F32 = jnp.float32
BF16 = jnp.bfloat16

D_MODEL = 1024
NORM_EPS = 1e-6

SSD_D_INNER = 2048
SSD_HEAD_DIM = 64
SSD_N_HEADS = 32
SSD_N_GROUPS = 4
SSD_D_STATE = 128
SSD_CONV_W = 4
SSD_CONV_DIM = 3072
SSD_NORM_GROUP = 512
SSD_N_PAIRS = SSD_N_HEADS // 2
SSD_XBC_OFF = SSD_D_INNER
SSD_DT_OFF = SSD_D_INNER + SSD_CONV_DIM

HGRN_HEAD_DIM = 128
HGRN_N_HEADS = 8

FFN_HIDDEN = 2816

LANES = 128
SUBLANES = 8
BF16_TILE_ROWS = 16
SCAN_CHUNK = 64
SSD_LONG_TILE = 256
HGRN_LONG_TILE = 256
SHORT_SEQ_PER_STEP = 4
FFN_ROWS = 512
CONV_COLS = 512
HGRN_COLS = 256
VMEM_LIMIT = 56 * 1024 * 1024


def _dot(a, b):
    return jnp.dot(a, b, preferred_element_type=F32)


def _dot_nt(a, b):
    return lax.dot_general(a, b, (((1,), (1,)), ((), ())), preferred_element_type=F32)


def _dot_tn(a, b):
    return lax.dot_general(a, b, (((0,), (0,)), ((), ())), preferred_element_type=F32)


def _split3(a):
    hi = a.astype(BF16)
    r1 = a - hi.astype(F32)
    mid = r1.astype(BF16)
    lo = (r1 - mid.astype(F32)).astype(BF16)
    return hi, mid, lo


def _stack3(a):
    hi = a.astype(BF16).astype(F32)
    r1 = a - hi
    mid = r1.astype(BF16).astype(F32)
    return jnp.concatenate([hi, mid, r1 - mid, jnp.zeros_like(a)], axis=1).astype(BF16)


def _rms(x, w):
    return x * lax.rsqrt(jnp.mean(x * x, axis=-1, keepdims=True) + NORM_EPS) * w


def _silu(x):
    return x * jax.nn.sigmoid(x)


def _shift_rows(u, first_rows):
    rows, cols = u.shape
    k = first_rows.shape[0]
    nv = rows // SUBLANES
    r = pltpu.roll(u.reshape(nv, SUBLANES, cols), k, axis=1)
    first = jnp.concatenate([first_rows, jnp.zeros((SUBLANES - k, cols), u.dtype)], axis=0)
    prev = jnp.concatenate([first[None], r[:nv - 1]], axis=0)
    sub = lax.broadcasted_iota(jnp.int32, (nv, SUBLANES, cols), 1)
    return jnp.where(sub < k, prev, r).reshape(rows, cols)


def _chunk_triangle(R, C):
    rr = lax.broadcasted_iota(jnp.int32, (R, R), 0)
    cc = lax.broadcasted_iota(jnp.int32, (R, R), 1)
    return ((cc <= rr) & ((rr & -C) == (cc & -C))).astype(BF16)


def _ssd_kernel(*refs, has_init, NB, TL, C, n_side):
    it = iter(refs)
    x_ref = next(it)
    s0_ref = cv0_ref = None
    if has_init:
        s0_ref = next(it)
        cv0_ref = next(it)
    (nw_ref, win_ref, cw_ref, cb_ref, dtb_ref, dtbT_ref,
     alog_ref, alogp_ref, dexp_ref, gnw_ref, wo_ref) = [next(it) for _ in range(11)]
    side_in = [next(it) for _ in range(n_side)]
    xo_ref = next(it)
    so_ref = next(it)
    cvo_ref = next(it)
    for w_ref in side_in:
        next(it)[...] = w_ref[...].astype(BF16)
    (st_scr, st16_scr, hist_scr, z_scr, xbc_scr, y_scr, e_scr, tri_scr,
     colc_scr, mix_scr, xw16_scr, xsl16_scr, xsr16_scr, bc16_scr) = [next(it) for _ in range(14)]
    es_scr = colcs_scr = None
    if 2 * C != LANES:
        es_scr = next(it)
        colcs_scr = next(it)

    R = NB * TL
    n_ch = R // C
    ch_per_seq = TL // C
    W2 = 2 * C
    i = pl.program_id(1)
    n_i = pl.num_programs(1)

    @pl.when(i == 0)
    def _init():
        ek = lax.broadcasted_iota(jnp.int32, e_scr.shape, 0)
        el = lax.broadcasted_iota(jnp.int32, e_scr.shape, 1)
        e_scr[...] = ((ek < 3 * SSD_N_HEADS) & ((ek & (SSD_N_HEADS - 1)) == el // SSD_HEAD_DIM)).astype(BF16)
        if es_scr is not None:
            ek = lax.broadcasted_iota(jnp.int32, es_scr.shape, 0)
            el = lax.broadcasted_iota(jnp.int32, es_scr.shape, 1)
            es_scr[...] = ((ek < 3 * SSD_N_HEADS) & ((ek & (SSD_N_HEADS - 1)) == el // C)).astype(BF16)
        tri_scr[...] = _chunk_triangle(R, C)
        hist_scr[...] = jnp.zeros_like(hist_scr)
        if has_init:
            for nb in range(NB):
                for g in range(SSD_N_GROUPS):
                    st_scr[nb, g] = s0_ref[nb, g * SSD_NORM_GROUP:(g + 1) * SSD_NORM_GROUP, :].T
                for k in range(SSD_CONV_DIM // CONV_COLS):
                    hist_scr[nb, k, SUBLANES - 3:SUBLANES, :] = cv0_ref[nb, :, k * CONV_COLS:(k + 1) * CONV_COLS]
        else:
            st_scr[...] = jnp.zeros_like(st_scr)
        st16_scr[...] = st_scr[...].astype(BF16)

    x = x_ref[...].reshape(R, D_MODEL)
    h16 = _rms(x, nw_ref[...]).astype(BF16)

    for cb0 in range(0, SSD_CONV_DIM, CONV_COLS):
        cols = slice(cb0, cb0 + CONV_COLS)
        xr = _dot(h16, win_ref[:, SSD_XBC_OFF + cb0:SSD_XBC_OFF + cb0 + CONV_COLS])
        w0, w1, w2, w3 = (cw_ref[k:k + 1, cols] for k in range(SSD_CONV_W))
        for nb in range(NB):
            xn = xr[nb * TL:(nb + 1) * TL]
            hm = hist_scr[nb, cb0 // CONV_COLS]
            sx = _shift_rows(xn, hm[7:8])
            a2 = _shift_rows(w1 * xn + w0 * sx, w1 * hm[6:8] + w0 * hm[5:7])
            xbc_scr[nb * TL:(nb + 1) * TL, cols] = _silu(w3 * xn + w2 * sx + a2 + cb_ref[:, cols])
            hist_scr[nb, cb0 // CONV_COLS] = xn[TL - SUBLANES:TL]
    for zb0 in range(0, SSD_D_INNER, CONV_COLS):
        z_scr[:, zb0:zb0 + CONV_COLS] = _silu(_dot(h16, win_ref[:, zb0:zb0 + CONV_COLS]))

    dt_raw = _dot(h16, win_ref[:, SSD_DT_OFF:])
    dt = jax.nn.softplus(dt_raw + dtb_ref[...])
    pi = lax.broadcasted_iota(jnp.int32, (SSD_N_HEADS, SSD_N_HEADS), 0)
    pj = lax.broadcasted_iota(jnp.int32, (SSD_N_HEADS, SSD_N_HEADS), 1)
    pick = (pj == jnp.where(pi < SSD_N_PAIRS, 2 * pi, 2 * (pi - SSD_N_PAIRS) + 1)).astype(BF16)
    dtT = jax.nn.softplus(sum(_dot_nt(pick, p) for p in _split3(dt_raw)) + dtbT_ref[...])
    dtTp = jnp.concatenate(
        [jnp.concatenate([dtT[0:SSD_N_PAIRS, c * C:(c + 1) * C],
                          dtT[SSD_N_PAIRS:, c * C:(c + 1) * C]], axis=1) for c in range(n_ch)],
        axis=0)
    a_row = -jnp.exp(alog_ref[...])
    a_pair = -jnp.exp(alogp_ref[...])
    cum = sum(_dot(tri_scr[...], p) for p in _split3(dt * a_row))
    last = jnp.concatenate(
        [jnp.broadcast_to(cum[(c + 1) * C - 1:(c + 1) * C, :], (C, SSD_N_HEADS)) for c in range(n_ch)], axis=0)
    wdt = jnp.exp(last - cum) * dt
    cum3 = _stack3(cum)
    colc_scr[...] = _dot(cum3, e_scr[...])
    xs = xbc_scr[:, :SSD_D_INNER]
    xw16_scr[...] = (xs * _dot(_stack3(wdt), e_scr[...])).astype(BF16)
    head_left = (lax.broadcasted_iota(jnp.int32, (R, SSD_D_INNER), 1) & (LANES - 1)) < SSD_HEAD_DIM
    xsl16_scr[...] = jnp.where(head_left, xs, 0.0).astype(BF16)
    xsr16_scr[...] = jnp.where(head_left, 0.0, xs).astype(BF16)
    bc16_scr[...] = xbc_scr[:, SSD_D_INNER:].astype(BF16)
    if colcs_scr is not None:
        colcs_scr[...] = _dot(cum3, es_scr[...])
    r2 = lax.broadcasted_iota(jnp.int32, (W2, W2), 0)
    c2 = lax.broadcasted_iota(jnp.int32, (W2, W2), 1)
    triT2_16 = ((r2 <= c2) & ((r2 < C) == (c2 < C))).astype(BF16)
    aTp = dtTp * jnp.concatenate([a_pair] * n_ch, axis=0)
    cumTp = sum(_dot(p, triT2_16) for p in _split3(aTp))

    tp = lax.broadcasted_iota(jnp.int32, (C, W2), 0)
    sp = lax.broadcasted_iota(jnp.int32, (C, W2), 1)
    tril_pair = jnp.where(sp < C, sp, sp - C) <= tp
    colc_l_scr = colc_scr if colcs_scr is None else colcs_scr
    for c in range(n_ch):
        rows = slice(c * C, (c + 1) * C)
        for g in range(SSD_N_GROUPS):
            b16 = bc16_scr[rows, g * SSD_D_STATE:(g + 1) * SSD_D_STATE]
            c16 = bc16_scr[rows, (SSD_N_GROUPS + g) * SSD_D_STATE:(SSD_N_GROUPS + g + 1) * SSD_D_STATE]
            cbm = jnp.where(tril_pair, _dot_nt(c16, jnp.concatenate([b16, b16], axis=0)), 0.0)
            for q in range(4):
                jp = g * 4 + q
                k = c * SSD_N_PAIRS + jp
                lanes_s = slice(jp * W2, (jp + 1) * W2)
                decay = jnp.exp(jnp.minimum(colc_l_scr[rows, lanes_s] - cumTp[k:k + 1, :], 0.0))
                mix_scr[rows, lanes_s] = (cbm * decay * dtTp[k:k + 1, :]).astype(mix_scr.dtype)
    colc_scr[...] = jnp.exp(colc_scr[...])

    for c in range(n_ch):
        nb = c // ch_per_seq
        rows = slice(c * C, (c + 1) * C)
        for g in range(SSD_N_GROUPS):
            gl = slice(g * SSD_NORM_GROUP, (g + 1) * SSD_NORM_GROUP)
            b16 = bc16_scr[rows, g * SSD_D_STATE:(g + 1) * SSD_D_STATE]
            c16 = bc16_scr[rows, (SSD_N_GROUPS + g) * SSD_D_STATE:(SSD_N_GROUPS + g + 1) * SSD_D_STATE]
            y_inter = _dot(c16, st16_scr[nb, g])
            for q in range(4):
                jp = g * 4 + q
                lanes = slice(jp * LANES, (jp + 1) * LANES)
                mix16 = mix_scr[rows, jp * W2:(jp + 1) * W2].astype(BF16)
                rhs = jnp.concatenate([xsl16_scr[rows, lanes], xsr16_scr[rows, lanes]], axis=0)
                y_scr[rows, lanes] = _dot(mix16, rhs) + colc_scr[rows, lanes] * y_inter[:, q * LANES:(q + 1) * LANES]
            el_g = colc_scr[(c + 1) * C - 1:(c + 1) * C, gl]
            st_new = el_g * st_scr[nb, g] + _dot_tn(b16, xw16_scr[rows, gl])
            st_scr[nb, g] = st_new
            st16_scr[nb, g] = st_new.astype(BF16)

    out = None
    for g in range(SSD_N_GROUPS):
        cols = slice(g * SSD_NORM_GROUP, (g + 1) * SSD_NORM_GROUP)
        yg = (y_scr[:, cols] + dexp_ref[:, cols] * xbc_scr[:, cols]) * z_scr[:, cols]
        yn = yg * lax.rsqrt(jnp.mean(yg * yg, axis=-1, keepdims=True) + NORM_EPS) * gnw_ref[:, cols]
        part = _dot(yn.astype(BF16), wo_ref[cols, :])
        out = part if out is None else out + part
    xo_ref[...] = (x + out).reshape(NB, TL, D_MODEL)

    @pl.when(i == n_i - 1)
    def _fin():
        for nb in range(NB):
            for g in range(SSD_N_GROUPS):
                so_ref[nb, g * SSD_NORM_GROUP:(g + 1) * SSD_NORM_GROUP, :] = st_scr[nb, g].T
            for k in range(SSD_CONV_DIM // CONV_COLS):
                cvo_ref[nb, :, k * CONV_COLS:(k + 1) * CONV_COLS] = hist_scr[nb, k, SUBLANES - 3:SUBLANES, :]


def _const_spec(shape):
    nd = len(shape)
    return pl.BlockSpec(shape, lambda *_: (0,) * nd, pipeline_mode=pl.Buffered(1))


def _layer_spec(arr, layer):
    nd = arr.ndim - 1
    return pl.BlockSpec((None,) + arr.shape[1:], lambda *_: (layer,) + (0,) * nd,
                        pipeline_mode=pl.Buffered(1))


def _side_cast(side, n_steps, step_of):
    in_specs, out_specs, out_shapes = [], [], []
    for w, layer in side:
        rows, cols = w.shape[1:]
        n_blk = max(n for n in range(1, n_steps + 1) if rows % n == 0 and (rows // n) % BF16_TILE_ROWS == 0)
        br = rows // n_blk
        in_specs.append(pl.BlockSpec(
            (None, br, cols), lambda *g, layer=layer, n_blk=n_blk: (layer, jnp.minimum(step_of(*g), n_blk - 1), 0)))
        out_specs.append(pl.BlockSpec(
            (br, cols), lambda *g, n_blk=n_blk: (jnp.minimum(step_of(*g), n_blk - 1), 0)))
        out_shapes.append(jax.ShapeDtypeStruct((rows, cols), BF16))
    return in_specs, out_specs, out_shapes


def _ssd_mixer(x, s0, cv0, p, side, *, NB, TL, C):
    n_seq, L, _ = x.shape
    has_init = s0 is not None
    R = NB * TL
    grid = (n_seq // NB, L // TL)
    side_in, side_out, side_shapes = _side_cast(side, grid[0] * grid[1], lambda b, i: b * grid[1] + i)
    in_specs = [pl.BlockSpec((NB, TL, D_MODEL), lambda b, i: (b, i, 0))]
    args = [x]
    if has_init:
        in_specs += [pl.BlockSpec((NB, SSD_D_INNER, SSD_D_STATE), lambda b, i: (b, 0, 0)),
                     pl.BlockSpec((NB, 3, SSD_CONV_DIM), lambda b, i: (b, 0, 0))]
        args += [s0, cv0]
    alog_pair = jnp.concatenate(
        [jnp.broadcast_to(p["alog"][0::2, None], (SSD_N_PAIRS, C)),
         jnp.broadcast_to(p["alog"][1::2, None], (SSD_N_PAIRS, C))], axis=1)
    consts = [p["nw"], p["win"], p["cw"], p["cb"], p["dtb"], p["dtbT"],
              p["alog"].reshape(1, SSD_N_HEADS), alog_pair, p["dexp"], p["gnw"], p["wo"]]
    for c in consts:
        in_specs.append(_layer_spec(c, 0) if c.ndim == 3 else _const_spec(c.shape))
    args += consts
    in_specs += side_in
    args += [w for w, _ in side]
    out_shape = [jax.ShapeDtypeStruct((n_seq, L, D_MODEL), F32),
                 jax.ShapeDtypeStruct((n_seq, SSD_D_INNER, SSD_D_STATE), F32),
                 jax.ShapeDtypeStruct((n_seq, 3, SSD_CONV_DIM), F32)] + side_shapes
    out_specs = [pl.BlockSpec((NB, TL, D_MODEL), lambda b, i: (b, i, 0)),
                 pl.BlockSpec((NB, SSD_D_INNER, SSD_D_STATE), lambda b, i: (b, 0, 0)),
                 pl.BlockSpec((NB, 3, SSD_CONV_DIM), lambda b, i: (b, 0, 0))] + side_out
    scratch = [pltpu.VMEM((NB, SSD_N_GROUPS, SSD_D_STATE, SSD_NORM_GROUP), F32),
               pltpu.VMEM((NB, SSD_N_GROUPS, SSD_D_STATE, SSD_NORM_GROUP), BF16),
               pltpu.VMEM((NB, SSD_CONV_DIM // CONV_COLS, SUBLANES, CONV_COLS), F32),
               pltpu.VMEM((R, SSD_D_INNER), F32),
               pltpu.VMEM((R, SSD_CONV_DIM), F32),
               pltpu.VMEM((R, SSD_D_INNER), F32),
               pltpu.VMEM((LANES, SSD_D_INNER), BF16),
               pltpu.VMEM((R, R), BF16),
               pltpu.VMEM((R, SSD_D_INNER), F32),
               pltpu.VMEM((R, SSD_N_PAIRS * 2 * C), BF16 if 2 * C == LANES else F32),
               pltpu.VMEM((R, SSD_D_INNER), BF16),
               pltpu.VMEM((R, SSD_D_INNER), BF16),
               pltpu.VMEM((R, SSD_D_INNER), BF16),
               pltpu.VMEM((R, 2 * SSD_N_GROUPS * SSD_D_STATE), BF16)]
    if 2 * C != LANES:
        scratch += [pltpu.VMEM((LANES, SSD_N_PAIRS * 2 * C), BF16),
                    pltpu.VMEM((R, SSD_N_PAIRS * 2 * C), F32)]
    outs = pl.pallas_call(
        functools.partial(_ssd_kernel, has_init=has_init, NB=NB, TL=TL, C=C, n_side=len(side)),
        grid=grid, in_specs=in_specs, out_specs=out_specs, out_shape=out_shape,
        scratch_shapes=scratch,
        compiler_params=pltpu.CompilerParams(
            dimension_semantics=("arbitrary", "arbitrary"), vmem_limit_bytes=VMEM_LIMIT),
        name="ssd_mixer_init" if has_init else "ssd_mixer_zero",
    )(*args)
    return outs[0], outs[1], outs[2], list(outs[3:])


def _hgrn_kernel(*refs, has_init, NB, TL, C):
    it = iter(refs)
    x_ref = next(it)
    s0_ref = next(it) if has_init else None
    nw_ref, win_ref, lb_ref, gnw_ref, wo_ref = [next(it) for _ in range(5)]
    xo_ref = next(it)
    so_ref = next(it)
    st_scr, tri_scr, qe_scr, ke_scr, kw_scr, v_scr, g_scr, elT_scr, o_scr = [next(it) for _ in range(9)]

    R = NB * TL
    n_ch = R // C
    ch_per_seq = TL // C
    HD = HGRN_HEAD_DIM
    i = pl.program_id(1)
    n_i = pl.num_programs(1)

    @pl.when(i == 0)
    def _init():
        tri_scr[...] = _chunk_triangle(R, C)
        if has_init:
            st_scr[...] = s0_ref[...]
        else:
            st_scr[...] = jnp.zeros_like(st_scr)

    x = x_ref[...].reshape(R, D_MODEL)
    h16 = _rms(x, nw_ref[...]).astype(BF16)
    lb_soft = jax.nn.softmax(lb_ref[...], axis=0)
    lb = (lb_soft[0:1, :] + lb_soft[1:2, :]) - lb_soft[0:1, :]

    for c0 in range(0, D_MODEL, HGRN_COLS):
        cols = slice(c0, c0 + HGRN_COLS)
        q = _silu(_dot(h16, win_ref[:, c0:c0 + HGRN_COLS]))
        f = _dot(h16, win_ref[:, D_MODEL + c0:D_MODEL + c0 + HGRN_COLS])
        v = _dot(h16, win_ref[:, 2 * D_MODEL + c0:2 * D_MODEL + c0 + HGRN_COLS])
        g = _dot(h16, win_ref[:, 3 * D_MODEL + c0:3 * D_MODEL + c0 + HGRN_COLS])
        lbc = lb[:, cols]
        forget = lbc + (1.0 - lbc) * jax.nn.sigmoid(f)
        k = 1.0 - forget
        b = sum(_dot(tri_scr[...], p) for p in _split3(jnp.log(forget)))
        ends = [b[(c + 1) * C - 1:(c + 1) * C, :] for c in range(n_ch)]
        last = jnp.concatenate([jnp.broadcast_to(e, (C, HGRN_COLS)) for e in ends], axis=0)
        qe_scr[:, cols] = (q * jnp.exp(b)).astype(BF16)
        ke_scr[:, cols] = (k * jnp.exp(-b)).astype(BF16)
        kw_scr[:, cols] = (k * jnp.exp(last - b)).astype(BF16)
        v_scr[:, cols] = v.astype(BF16)
        g_scr[:, cols] = _silu(g)
        el = jnp.exp(jnp.concatenate(ends + [jnp.zeros((LANES - n_ch, HGRN_COLS), F32)], axis=0))
        elT_scr[cols, :] = el.T

    rr = lax.broadcasted_iota(jnp.int32, (C, C), 0)
    cc = lax.broadcasted_iota(jnp.int32, (C, C), 1)
    tril = cc <= rr

    for c in range(n_ch):
        nb = c // ch_per_seq
        rows = slice(c * C, (c + 1) * C)
        for hh in range(HGRN_N_HEADS):
            sl = slice(hh * HD, (hh + 1) * HD)
            qe, v = qe_scr[rows, sl], v_scr[rows, sl]
            sc = jnp.where(tril, _dot_nt(qe, ke_scr[rows, sl]), 0.0)
            s_h = st_scr[nb, hh]
            o_scr[rows, sl] = _dot(jnp.concatenate([qe, sc.astype(BF16)], axis=1),
                                   jnp.concatenate([s_h.astype(BF16), v], axis=0))
            st_scr[nb, hh] = elT_scr[sl, c:c + 1] * s_h + _dot_tn(kw_scr[rows, sl], v)

    out = None
    for c0 in range(0, D_MODEL, HGRN_COLS):
        parts = []
        for h0 in range(c0, c0 + HGRN_COLS, HD):
            blk = o_scr[:, h0:h0 + HD]
            parts.append(blk * lax.rsqrt(jnp.mean(blk * blk, axis=-1, keepdims=True) + NORM_EPS))
        cols = slice(c0, c0 + HGRN_COLS)
        on = jnp.concatenate(parts, axis=1) * gnw_ref[:, cols] * g_scr[:, cols]
        part = _dot(on.astype(BF16), wo_ref[cols, :])
        out = part if out is None else out + part
    xo_ref[...] = (x + out).reshape(NB, TL, D_MODEL)

    @pl.when(i == n_i - 1)
    def _fin():
        so_ref[...] = st_scr[...]


def _hgrn_mixer(x, s0, p, *, NB, TL, C):
    n_seq, L, _ = x.shape
    has_init = s0 is not None
    R = NB * TL
    grid = (n_seq // NB, L // TL)
    st_block = (NB, HGRN_N_HEADS, HGRN_HEAD_DIM, HGRN_HEAD_DIM)
    in_specs = [pl.BlockSpec((NB, TL, D_MODEL), lambda b, i: (b, i, 0))]
    args = [x]
    if has_init:
        in_specs.append(pl.BlockSpec(st_block, lambda b, i: (b, 0, 0, 0)))
        args.append(s0)
    consts = [p["nw"], p["win"], p["lb"], p["gnw"], p["wo"]]
    for c in consts:
        in_specs.append(_layer_spec(c, 0) if c.ndim == 3 else _const_spec(c.shape))
    args += consts
    out_shape = (jax.ShapeDtypeStruct((n_seq, L, D_MODEL), F32),
                 jax.ShapeDtypeStruct((n_seq,) + st_block[1:], F32))
    out_specs = (pl.BlockSpec((NB, TL, D_MODEL), lambda b, i: (b, i, 0)),
                 pl.BlockSpec(st_block, lambda b, i: (b, 0, 0, 0)))
    scratch = [pltpu.VMEM(st_block, F32),
               pltpu.VMEM((R, R), BF16),
               pltpu.VMEM((R, D_MODEL), BF16),
               pltpu.VMEM((R, D_MODEL), BF16),
               pltpu.VMEM((R, D_MODEL), BF16),
               pltpu.VMEM((R, D_MODEL), BF16),
               pltpu.VMEM((R, D_MODEL), F32),
               pltpu.VMEM((D_MODEL, LANES), F32),
               pltpu.VMEM((R, D_MODEL), F32)]
    return pl.pallas_call(
        functools.partial(_hgrn_kernel, has_init=has_init, NB=NB, TL=TL, C=C),
        grid=grid, in_specs=in_specs, out_specs=out_specs, out_shape=out_shape,
        scratch_shapes=scratch,
        compiler_params=pltpu.CompilerParams(
            dimension_semantics=("arbitrary", "arbitrary"), vmem_limit_bytes=VMEM_LIMIT),
        name="hgrn_mixer_init" if has_init else "hgrn_mixer_zero",
    )(*args)


def _ffn_kernel(*refs, final, n_side):
    it = iter(refs)
    x_ref, nw_ref, wg_ref, wu_ref, wd_ref = [next(it) for _ in range(5)]
    fw_ref = next(it) if final else None
    side_in = [next(it) for _ in range(n_side)]
    o_ref = next(it)
    for w_ref in side_in:
        next(it)[...] = w_ref[...].astype(BF16)
    x = x_ref[...]
    h16 = _rms(x, nw_ref[...]).astype(BF16)
    act = _silu(_dot(h16, wg_ref[...])) * _dot(h16, wu_ref[...])
    y = x + _dot(act.astype(BF16), wd_ref[...])
    if final:
        y = _rms(y, fw_ref[...])
    o_ref[...] = y


def _ffn(x2d, nw, wg, wu, wd, final_w, side, *, TM):
    rows = x2d.shape[0]
    final = final_w is not None
    n_steps = rows // TM
    side_in, side_out, side_shapes = _side_cast(side, n_steps, lambda i: i)
    consts = [nw, wg, wu, wd] + ([final_w] if final else [])
    in_specs = [pl.BlockSpec((TM, D_MODEL), lambda i: (i, 0))] + [_const_spec(c.shape) for c in consts] + side_in
    outs = pl.pallas_call(
        functools.partial(_ffn_kernel, final=final, n_side=len(side)),
        grid=(n_steps,),
        in_specs=in_specs,
        out_specs=[pl.BlockSpec((TM, D_MODEL), lambda i: (i, 0))] + side_out,
        out_shape=[jax.ShapeDtypeStruct((rows, D_MODEL), F32)] + side_shapes,
        compiler_params=pltpu.CompilerParams(
            dimension_semantics=("arbitrary",), vmem_limit_bytes=VMEM_LIMIT),
        name="swiglu_final" if final else "swiglu",
    )(x2d, *consts, *[w for w, _ in side])
    return outs[0], list(outs[1:])


def _tiles(n_seq, L):
    if L >= SSD_LONG_TILE:
        return dict(NB=1, TL_SSD=SSD_LONG_TILE, TL_HGRN=HGRN_LONG_TILE, C=SCAN_CHUNK, TM=FFN_ROWS)
    return dict(NB=SHORT_SEQ_PER_STEP, TL_SSD=L, TL_HGRN=L, C=min(L, SCAN_CHUNK), TM=min(FFN_ROWS, n_seq * L))


def _trunk(x, s_ssd, cv, s_hgrn, ssd_p, hgrn_p, ffn_nw, final_w, w16, raw):
    n_seq, L, _ = x.shape
    t = _tiles(n_seq, L)
    s0 = None if s_ssd is None else s_ssd.reshape(n_seq, SSD_D_INNER, SSD_D_STATE)

    def missing(names):
        return [k for k in names if k not in w16]

    need = missing(("wg0", "wu0", "wd0"))
    x, s_new, cv_new, cast = _ssd_mixer(x, s0, cv, ssd_p, [raw[k] for k in need],
                                        NB=t["NB"], TL=t["TL_SSD"], C=t["C"])
    w16.update(zip(need, cast))
    need = missing(("hgrn_in", "hgrn_out", "wg1", "wu1", "wd1"))
    x, cast = _ffn(x.reshape(n_seq * L, D_MODEL), ffn_nw[0], w16["wg0"], w16["wu0"], w16["wd0"], None,
                   [raw[k] for k in need], TM=t["TM"])
    w16.update(zip(need, cast))
    x, h_new = _hgrn_mixer(x.reshape(n_seq, L, D_MODEL), s_hgrn,
                           dict(hgrn_p, win=w16["hgrn_in"], wo=w16["hgrn_out"]),
                           NB=t["NB"], TL=t["TL_HGRN"], C=t["C"])
    y, _ = _ffn(x.reshape(n_seq * L, D_MODEL), ffn_nw[1], w16["wg1"], w16["wu1"], w16["wd1"], final_w,
                [], TM=t["TM"])
    s_new = s_new.reshape(1, n_seq, SSD_N_HEADS, SSD_HEAD_DIM, SSD_D_STATE)
    return y.reshape(n_seq, L, D_MODEL), s_new, cv_new[None], h_new[None]


def kernel(x_prompt, x_sample, state_ssd, cache_conv, state_hgrn, ssd_norm_w, ssd_in_w, ssd_conv_w, ssd_conv_b, ssd_dt_bias, ssd_A_log, ssd_D, ssd_gnorm_w, ssd_out_w, hgrn_norm_w, hgrn_in_w, hgrn_lower_bounds, hgrn_gnorm_w, hgrn_out_w, ffn_norm_w, ffn_w_gate, ffn_w_up, ffn_w_down, final_norm_w):
    pair_order = jnp.concatenate([jnp.arange(0, SSD_N_HEADS, 2), jnp.arange(1, SSD_N_HEADS, 2)])
    ssd_p = {
        "nw": ssd_norm_w[0].reshape(1, D_MODEL),
        "win": ssd_in_w.astype(BF16),
        "cw": ssd_conv_w,
        "cb": ssd_conv_b[0].reshape(1, SSD_CONV_DIM),
        "dtb": ssd_dt_bias[0].reshape(1, SSD_N_HEADS),
        "dtbT": ssd_dt_bias[0][pair_order].reshape(SSD_N_HEADS, 1),
        "alog": ssd_A_log[0],
        "dexp": jnp.repeat(ssd_D[0], SSD_HEAD_DIM).reshape(1, SSD_D_INNER),
        "gnw": ssd_gnorm_w[0].reshape(1, SSD_D_INNER),
        "wo": ssd_out_w.astype(BF16),
    }
    hgrn_p = {
        "nw": hgrn_norm_w[0].reshape(1, D_MODEL),
        "lb": hgrn_lower_bounds,
        "gnw": jnp.tile(hgrn_gnorm_w[0], HGRN_N_HEADS).reshape(1, D_MODEL),
    }
    ffn_nw = ffn_norm_w.reshape(2, 1, D_MODEL)
    final_w = final_norm_w.reshape(1, D_MODEL)
    raw = {"wg0": (ffn_w_gate, 0), "wu0": (ffn_w_up, 0), "wd0": (ffn_w_down, 0),
           "hgrn_in": (hgrn_in_w, 0), "hgrn_out": (hgrn_out_w, 0),
           "wg1": (ffn_w_gate, 1), "wu1": (ffn_w_up, 1), "wd1": (ffn_w_down, 1)}
    w16 = {}

    y_p, ssd_s_p, conv_p, hgrn_s_p = _trunk(
        x_prompt, None, None, None, ssd_p, hgrn_p, ffn_nw, final_w, w16, raw)
    y_s, ssd_s_s, conv_s, hgrn_s_s = _trunk(
        x_sample, state_ssd[0], cache_conv[0], state_hgrn[0], ssd_p, hgrn_p, ffn_nw, final_w, w16, raw)
    return (y_p, y_s, ssd_s_p, conv_p, hgrn_s_p, ssd_s_s, conv_s, hgrn_s_s)
```

```python
import functools

import jax
import jax.numpy as jnp
from jax import lax
from jax.experimental import pallas as pl
from jax.experimental.pallas import tpu as pltpu

F32 = jnp.float32
BF16 = jnp.bfloat16

D_MODEL = 1024
NORM_EPS = 1e-6

SSD_D_INNER = 2048
SSD_HEAD_DIM = 64
SSD_N_HEADS = 32
SSD_N_GROUPS = 4
SSD_D_STATE = 128
SSD_CONV_W = 4
SSD_CONV_DIM = 3072
SSD_NORM_GROUP = 512
SSD_N_PAIRS = SSD_N_HEADS // 2
SSD_XBC_OFF = SSD_D_INNER
SSD_DT_OFF = SSD_D_INNER + SSD_CONV_DIM

HGRN_HEAD_DIM = 128
HGRN_N_HEADS = 8

FFN_HIDDEN = 2816

LANES = 128
SUBLANES = 8
BF16_TILE_ROWS = 16
SCAN_CHUNK = 64
HGRN_CHUNK = 128
SSD_LONG_TILE = 256
HGRN_LONG_TILE = 256
SHORT_SEQ_PER_STEP = 4
FFN_ROWS = 512
CONV_COLS = 512
HGRN_COLS = 256
VMEM_LIMIT = 56 * 1024 * 1024


def _dot(a, b):
    return jnp.dot(a, b, preferred_element_type=F32)


def _dot_nt(a, b):
    return lax.dot_general(a, b, (((1,), (1,)), ((), ())), preferred_element_type=F32)


def _dot_tn(a, b):
    return lax.dot_general(a, b, (((0,), (0,)), ((), ())), preferred_element_type=F32)


def _split3(a):
    hi = a.astype(BF16)
    r1 = a - hi.astype(F32)
    mid = r1.astype(BF16)
    lo = (r1 - mid.astype(F32)).astype(BF16)
    return hi, mid, lo


def _stack3(a):
    hi = a.astype(BF16).astype(F32)
    r1 = a - hi
    mid = r1.astype(BF16).astype(F32)
    return jnp.concatenate([hi, mid, r1 - mid, jnp.zeros_like(a)], axis=1).astype(BF16)


def _rms(x, w):
    return x * lax.rsqrt(jnp.mean(x * x, axis=-1, keepdims=True) + NORM_EPS) * w


def _silu(x):
    return x * jax.nn.sigmoid(x)


def _shift_rows(u, first_rows):
    rows, cols = u.shape
    k = first_rows.shape[0]
    nv = rows // SUBLANES
    r = pltpu.roll(u.reshape(nv, SUBLANES, cols), k, axis=1)
    first = jnp.concatenate([first_rows, jnp.zeros((SUBLANES - k, cols), u.dtype)], axis=0)
    prev = jnp.concatenate([first[None], r[:nv - 1]], axis=0)
    sub = lax.broadcasted_iota(jnp.int32, (nv, SUBLANES, cols), 1)
    return jnp.where(sub < k, prev, r).reshape(rows, cols)


def _chunk_triangle(R, C):
    rr = lax.broadcasted_iota(jnp.int32, (R, R), 0)
    cc = lax.broadcasted_iota(jnp.int32, (R, R), 1)
    return ((cc <= rr) & ((rr & -C) == (cc & -C))).astype(BF16)


def _ssd_kernel(*refs, has_init, NB, TL, C, n_side):
    it = iter(refs)
    x_ref = next(it)
    s0_ref = cv0_ref = None
    if has_init:
        s0_ref = next(it)
        cv0_ref = next(it)
    (nw_ref, win_ref, cw_ref, cb_ref, dtb_ref, dtbT_ref,
     alog_ref, alogp_ref, dexp_ref, gnw_ref, wo_ref) = [next(it) for _ in range(11)]
    side_in = [next(it) for _ in range(n_side)]
    xo_ref = next(it)
    so_ref = next(it)
    cvo_ref = next(it)
    for w_ref in side_in:
        next(it)[...] = w_ref[...].astype(BF16)
    (st_scr, st16_scr, hist_scr, z_scr, xbc_scr, y_scr, e_scr, tri_scr,
     colc_scr, mix_scr, xw16_scr, xsl16_scr, xsr16_scr, bc16_scr) = [next(it) for _ in range(14)]
    es_scr = colcs_scr = None
    if 2 * C != LANES:
        es_scr = next(it)
        colcs_scr = next(it)

    R = NB * TL
    n_ch = R // C
    ch_per_seq = TL // C
    W2 = 2 * C
    i = pl.program_id(1)
    n_i = pl.num_programs(1)

    @pl.when(i == 0)
    def _init():
        ek = lax.broadcasted_iota(jnp.int32, e_scr.shape, 0)
        el = lax.broadcasted_iota(jnp.int32, e_scr.shape, 1)
        e_scr[...] = ((ek < 3 * SSD_N_HEADS) & ((ek & (SSD_N_HEADS - 1)) == el // SSD_HEAD_DIM)).astype(BF16)
        if es_scr is not None:
            ek = lax.broadcasted_iota(jnp.int32, es_scr.shape, 0)
            el = lax.broadcasted_iota(jnp.int32, es_scr.shape, 1)
            es_scr[...] = ((ek < 3 * SSD_N_HEADS) & ((ek & (SSD_N_HEADS - 1)) == el // C)).astype(BF16)
        tri_scr[...] = _chunk_triangle(R, C)
        hist_scr[...] = jnp.zeros_like(hist_scr)
        if has_init:
            for nb in range(NB):
                for g in range(SSD_N_GROUPS):
                    st_scr[nb, g] = s0_ref[nb, g * SSD_NORM_GROUP:(g + 1) * SSD_NORM_GROUP, :].T
                for k in range(SSD_CONV_DIM // CONV_COLS):
                    hist_scr[nb, k, SUBLANES - 3:SUBLANES, :] = cv0_ref[nb, :, k * CONV_COLS:(k + 1) * CONV_COLS]
        else:
            st_scr[...] = jnp.zeros_like(st_scr)
        st16_scr[...] = st_scr[...].astype(BF16)

    x = x_ref[...].reshape(R, D_MODEL)
    h16 = _rms(x, nw_ref[...]).astype(BF16)

    for cb0 in range(0, SSD_CONV_DIM, CONV_COLS):
        cols = slice(cb0, cb0 + CONV_COLS)
        xr = _dot(h16, win_ref[:, SSD_XBC_OFF + cb0:SSD_XBC_OFF + cb0 + CONV_COLS])
        w0, w1, w2, w3 = (cw_ref[k:k + 1, cols] for k in range(SSD_CONV_W))
        for nb in range(NB):
            xn = xr[nb * TL:(nb + 1) * TL]
            hm = hist_scr[nb, cb0 // CONV_COLS]
            sx = _shift_rows(xn, hm[7:8])
            a2 = _shift_rows(w1 * xn + w0 * sx, w1 * hm[6:8] + w0 * hm[5:7])
            xbc_scr[nb * TL:(nb + 1) * TL, cols] = _silu(w3 * xn + w2 * sx + a2 + cb_ref[:, cols])
            hist_scr[nb, cb0 // CONV_COLS] = xn[TL - SUBLANES:TL]
    for zb0 in range(0, SSD_D_INNER, CONV_COLS):
        z_scr[:, zb0:zb0 + CONV_COLS] = _silu(_dot(h16, win_ref[:, zb0:zb0 + CONV_COLS]))

    dt_raw = _dot(h16, win_ref[:, SSD_DT_OFF:])
    dt = jax.nn.softplus(dt_raw + dtb_ref[...])
    pi = lax.broadcasted_iota(jnp.int32, (SSD_N_HEADS, SSD_N_HEADS), 0)
    pj = lax.broadcasted_iota(jnp.int32, (SSD_N_HEADS, SSD_N_HEADS), 1)
    pick = (pj == jnp.where(pi < SSD_N_PAIRS, 2 * pi, 2 * (pi - SSD_N_PAIRS) + 1)).astype(BF16)
    dtT = jax.nn.softplus(sum(_dot_nt(pick, p) for p in _split3(dt_raw)) + dtbT_ref[...])
    dtTp = jnp.concatenate(
        [jnp.concatenate([dtT[0:SSD_N_PAIRS, c * C:(c + 1) * C],
                          dtT[SSD_N_PAIRS:, c * C:(c + 1) * C]], axis=1) for c in range(n_ch)],
        axis=0)
    a_row = -jnp.exp(alog_ref[...])
    a_pair = -jnp.exp(alogp_ref[...])
    cum = sum(_dot(tri_scr[...], p) for p in _split3(dt * a_row))
    last = jnp.concatenate(
        [jnp.broadcast_to(cum[(c + 1) * C - 1:(c + 1) * C, :], (C, SSD_N_HEADS)) for c in range(n_ch)], axis=0)
    wdt = jnp.exp(last - cum) * dt
    cum3 = _stack3(cum)
    colc_scr[...] = _dot(cum3, e_scr[...])
    xs = xbc_scr[:, :SSD_D_INNER]
    xw16_scr[...] = (xs * _dot(_stack3(wdt), e_scr[...])).astype(BF16)
    head_left = (lax.broadcasted_iota(jnp.int32, (R, SSD_D_INNER), 1) & (LANES - 1)) < SSD_HEAD_DIM
    xsl16_scr[...] = jnp.where(head_left, xs, 0.0).astype(BF16)
    xsr16_scr[...] = jnp.where(head_left, 0.0, xs).astype(BF16)
    bc16_scr[...] = xbc_scr[:, SSD_D_INNER:].astype(BF16)
    if colcs_scr is not None:
        colcs_scr[...] = _dot(cum3, es_scr[...])
    r2 = lax.broadcasted_iota(jnp.int32, (W2, W2), 0)
    c2 = lax.broadcasted_iota(jnp.int32, (W2, W2), 1)
    triT2_16 = ((r2 <= c2) & ((r2 < C) == (c2 < C))).astype(BF16)
    aTp = dtTp * jnp.concatenate([a_pair] * n_ch, axis=0)
    cumTp = sum(_dot(p, triT2_16) for p in _split3(aTp))

    tp = lax.broadcasted_iota(jnp.int32, (C, W2), 0)
    sp = lax.broadcasted_iota(jnp.int32, (C, W2), 1)
    tril_pair = jnp.where(sp < C, sp, sp - C) <= tp
    colc_l_scr = colc_scr if colcs_scr is None else colcs_scr
    for c in range(n_ch):
        rows = slice(c * C, (c + 1) * C)
        for g in range(SSD_N_GROUPS):
            b16 = bc16_scr[rows, g * SSD_D_STATE:(g + 1) * SSD_D_STATE]
            c16 = bc16_scr[rows, (SSD_N_GROUPS + g) * SSD_D_STATE:(SSD_N_GROUPS + g + 1) * SSD_D_STATE]
            cbm = jnp.where(tril_pair, _dot_nt(c16, jnp.concatenate([b16, b16], axis=0)), 0.0)
            for q in range(4):
                jp = g * 4 + q
                k = c * SSD_N_PAIRS + jp
                lanes_s = slice(jp * W2, (jp + 1) * W2)
                decay = jnp.exp(jnp.minimum(colc_l_scr[rows, lanes_s] - cumTp[k:k + 1, :], 0.0))
                mix_scr[rows, lanes_s] = (cbm * decay * dtTp[k:k + 1, :]).astype(mix_scr.dtype)
    colc_scr[...] = jnp.exp(colc_scr[...])

    for c in range(n_ch):
        nb = c // ch_per_seq
        rows = slice(c * C, (c + 1) * C)
        for g in range(SSD_N_GROUPS):
            gl = slice(g * SSD_NORM_GROUP, (g + 1) * SSD_NORM_GROUP)
            b16 = bc16_scr[rows, g * SSD_D_STATE:(g + 1) * SSD_D_STATE]
            c16 = bc16_scr[rows, (SSD_N_GROUPS + g) * SSD_D_STATE:(SSD_N_GROUPS + g + 1) * SSD_D_STATE]
            y_inter = _dot(c16, st16_scr[nb, g])
            for q in range(4):
                jp = g * 4 + q
                lanes = slice(jp * LANES, (jp + 1) * LANES)
                mix16 = mix_scr[rows, jp * W2:(jp + 1) * W2].astype(BF16)
                rhs = jnp.concatenate([xsl16_scr[rows, lanes], xsr16_scr[rows, lanes]], axis=0)
                y_scr[rows, lanes] = _dot(mix16, rhs) + colc_scr[rows, lanes] * y_inter[:, q * LANES:(q + 1) * LANES]
            el_g = colc_scr[(c + 1) * C - 1:(c + 1) * C, gl]
            st_new = el_g * st_scr[nb, g] + _dot_tn(b16, xw16_scr[rows, gl])
            st_scr[nb, g] = st_new
            st16_scr[nb, g] = st_new.astype(BF16)

    out = None
    for g in range(SSD_N_GROUPS):
        cols = slice(g * SSD_NORM_GROUP, (g + 1) * SSD_NORM_GROUP)
        yg = (y_scr[:, cols] + dexp_ref[:, cols] * xbc_scr[:, cols]) * z_scr[:, cols]
        yn = yg * lax.rsqrt(jnp.mean(yg * yg, axis=-1, keepdims=True) + NORM_EPS) * gnw_ref[:, cols]
        part = _dot(yn.astype(BF16), wo_ref[cols, :])
        out = part if out is None else out + part
    xo_ref[...] = (x + out).reshape(NB, TL, D_MODEL)

    @pl.when(i == n_i - 1)
    def _fin():
        for nb in range(NB):
            for g in range(SSD_N_GROUPS):
                so_ref[nb, g * SSD_NORM_GROUP:(g + 1) * SSD_NORM_GROUP, :] = st_scr[nb, g].T
            for k in range(SSD_CONV_DIM // CONV_COLS):
                cvo_ref[nb, :, k * CONV_COLS:(k + 1) * CONV_COLS] = hist_scr[nb, k, SUBLANES - 3:SUBLANES, :]


def _const_spec(shape):
    nd = len(shape)
    return pl.BlockSpec(shape, lambda *_: (0,) * nd, pipeline_mode=pl.Buffered(1))


def _layer_spec(arr, layer):
    nd = arr.ndim - 1
    return pl.BlockSpec((None,) + arr.shape[1:], lambda *_: (layer,) + (0,) * nd,
                        pipeline_mode=pl.Buffered(1))


def _side_cast(side, n_steps, step_of):
    in_specs, out_specs, out_shapes = [], [], []
    for w, layer in side:
        rows, cols = w.shape[1:]
        n_blk = max(n for n in range(1, n_steps + 1) if rows % n == 0 and (rows // n) % BF16_TILE_ROWS == 0)
        br = rows // n_blk
        in_specs.append(pl.BlockSpec(
            (None, br, cols), lambda *g, layer=layer, n_blk=n_blk: (layer, jnp.minimum(step_of(*g), n_blk - 1), 0)))
        out_specs.append(pl.BlockSpec(
            (br, cols), lambda *g, n_blk=n_blk: (jnp.minimum(step_of(*g), n_blk - 1), 0)))
        out_shapes.append(jax.ShapeDtypeStruct((rows, cols), BF16))
    return in_specs, out_specs, out_shapes


def _ssd_mixer(x, s0, cv0, p, side, *, NB, TL, C):
    n_seq, L, _ = x.shape
    has_init = s0 is not None
    R = NB * TL
    grid = (n_seq // NB, L // TL)
    side_in, side_out, side_shapes = _side_cast(side, grid[0] * grid[1], lambda b, i: b * grid[1] + i)
    in_specs = [pl.BlockSpec((NB, TL, D_MODEL), lambda b, i: (b, i, 0))]
    args = [x]
    if has_init:
        in_specs += [pl.BlockSpec((NB, SSD_D_INNER, SSD_D_STATE), lambda b, i: (b, 0, 0)),
                     pl.BlockSpec((NB, 3, SSD_CONV_DIM), lambda b, i: (b, 0, 0))]
        args += [s0, cv0]
    alog_pair = jnp.concatenate(
        [jnp.broadcast_to(p["alog"][0::2, None], (SSD_N_PAIRS, C)),
         jnp.broadcast_to(p["alog"][1::2, None], (SSD_N_PAIRS, C))], axis=1)
    consts = [p["nw"], p["win"], p["cw"], p["cb"], p["dtb"], p["dtbT"],
              p["alog"].reshape(1, SSD_N_HEADS), alog_pair, p["dexp"], p["gnw"], p["wo"]]
    for c in consts:
        in_specs.append(_layer_spec(c, 0) if c.ndim == 3 else _const_spec(c.shape))
    args += consts
    in_specs += side_in
    args += [w for w, _ in side]
    out_shape = [jax.ShapeDtypeStruct((n_seq, L, D_MODEL), F32),
                 jax.ShapeDtypeStruct((n_seq, SSD_D_INNER, SSD_D_STATE), F32),
                 jax.ShapeDtypeStruct((n_seq, 3, SSD_CONV_DIM), F32)] + side_shapes
    out_specs = [pl.BlockSpec((NB, TL, D_MODEL), lambda b, i: (b, i, 0)),
                 pl.BlockSpec((NB, SSD_D_INNER, SSD_D_STATE), lambda b, i: (b, 0, 0)),
                 pl.BlockSpec((NB, 3, SSD_CONV_DIM), lambda b, i: (b, 0, 0))] + side_out
    scratch = [pltpu.VMEM((NB, SSD_N_GROUPS, SSD_D_STATE, SSD_NORM_GROUP), F32),
               pltpu.VMEM((NB, SSD_N_GROUPS, SSD_D_STATE, SSD_NORM_GROUP), BF16),
               pltpu.VMEM((NB, SSD_CONV_DIM // CONV_COLS, SUBLANES, CONV_COLS), F32),
               pltpu.VMEM((R, SSD_D_INNER), F32),
               pltpu.VMEM((R, SSD_CONV_DIM), F32),
               pltpu.VMEM((R, SSD_D_INNER), F32),
               pltpu.VMEM((LANES, SSD_D_INNER), BF16),
               pltpu.VMEM((R, R), BF16),
               pltpu.VMEM((R, SSD_D_INNER), F32),
               pltpu.VMEM((R, SSD_N_PAIRS * 2 * C), BF16 if 2 * C == LANES else F32),
               pltpu.VMEM((R, SSD_D_INNER), BF16),
               pltpu.VMEM((R, SSD_D_INNER), BF16),
               pltpu.VMEM((R, SSD_D_INNER), BF16),
               pltpu.VMEM((R, 2 * SSD_N_GROUPS * SSD_D_STATE), BF16)]
    if 2 * C != LANES:
        scratch += [pltpu.VMEM((LANES, SSD_N_PAIRS * 2 * C), BF16),
                    pltpu.VMEM((R, SSD_N_PAIRS * 2 * C), F32)]
    outs = pl.pallas_call(
        functools.partial(_ssd_kernel, has_init=has_init, NB=NB, TL=TL, C=C, n_side=len(side)),
        grid=grid, in_specs=in_specs, out_specs=out_specs, out_shape=out_shape,
        scratch_shapes=scratch,
        compiler_params=pltpu.CompilerParams(
            dimension_semantics=("arbitrary", "arbitrary"), vmem_limit_bytes=VMEM_LIMIT),
        name="ssd_mixer_init" if has_init else "ssd_mixer_zero",
    )(*args)
    return outs[0], outs[1], outs[2], list(outs[3:])


def _hgrn_kernel(*refs, has_init, NB, TL, C):
    it = iter(refs)
    x_ref = next(it)
    s0_ref = next(it) if has_init else None
    nw_ref, win_ref, lb_ref, gnw_ref, wo_ref = [next(it) for _ in range(5)]
    xo_ref = next(it)
    so_ref = next(it)
    (st_scr, tri_scr, qe_scr, qm_scr, ke_scr, kw_scr, v_scr, g_scr, elT_scr, sc_scr,
     o_scr) = [next(it) for _ in range(11)]
    assert C <= HGRN_HEAD_DIM

    R = NB * TL
    n_ch = R // C
    ch_per_seq = TL // C
    HD = HGRN_HEAD_DIM
    i = pl.program_id(1)
    n_i = pl.num_programs(1)

    @pl.when(i == 0)
    def _init():
        tri_scr[...] = _chunk_triangle(R, C)
        if has_init:
            st_scr[...] = s0_ref[...]
        else:
            st_scr[...] = jnp.zeros_like(st_scr)

    x = x_ref[...].reshape(R, D_MODEL)
    h16 = _rms(x, nw_ref[...]).astype(BF16)
    lb_soft = jax.nn.softmax(lb_ref[...], axis=0)
    lb = (lb_soft[0:1, :] + lb_soft[1:2, :]) - lb_soft[0:1, :]

    for c0 in range(0, D_MODEL, HGRN_COLS):
        cols = slice(c0, c0 + HGRN_COLS)
        q = _silu(_dot(h16, win_ref[:, c0:c0 + HGRN_COLS]))
        f = _dot(h16, win_ref[:, D_MODEL + c0:D_MODEL + c0 + HGRN_COLS])
        v = _dot(h16, win_ref[:, 2 * D_MODEL + c0:2 * D_MODEL + c0 + HGRN_COLS])
        g = _dot(h16, win_ref[:, 3 * D_MODEL + c0:3 * D_MODEL + c0 + HGRN_COLS])
        lbc = lb[:, cols]
        forget = lbc + (1.0 - lbc) * jax.nn.sigmoid(f)
        k = 1.0 - forget
        b = sum(_dot(tri_scr[...], p) for p in _split3(jnp.log(forget)))
        ends = [b[(c + 1) * C - 1:(c + 1) * C, :] for c in range(n_ch)]
        last = jnp.concatenate([jnp.broadcast_to(e, (C, HGRN_COLS)) for e in ends], axis=0)
        mid = jnp.concatenate([jnp.broadcast_to(b[c * C + C // 2:c * C + C // 2 + 1, :], (C, HGRN_COLS))
                               for c in range(n_ch)], axis=0)
        qe_scr[:, cols] = (q * jnp.exp(b)).astype(BF16)
        qm_scr[:, cols] = (q * jnp.exp(b - mid)).astype(BF16)
        ke_scr[:, cols] = (k * jnp.exp(mid - b)).astype(BF16)
        kw_scr[:, cols] = (k * jnp.exp(last - b)).astype(BF16)
        v_scr[:, cols] = v.astype(BF16)
        g_scr[:, cols] = _silu(g)
        el = jnp.exp(jnp.concatenate(ends + [jnp.zeros((LANES - n_ch, HGRN_COLS), F32)], axis=0))
        elT_scr[cols, :] = el.T

    rr = lax.broadcasted_iota(jnp.int32, (C, C), 0)
    cc = lax.broadcasted_iota(jnp.int32, (C, C), 1)
    tril = cc <= rr

    for c in range(n_ch):
        rows = slice(c * C, (c + 1) * C)
        for hh in range(HGRN_N_HEADS):
            sl = slice(hh * HD, (hh + 1) * HD)
            sc = jnp.where(tril, _dot_nt(qm_scr[rows, sl], ke_scr[rows, sl]), 0.0)
            sc_scr[rows, hh * HD:hh * HD + C] = sc.astype(BF16)

    for c in range(n_ch):
        nb = c // ch_per_seq
        rows = slice(c * C, (c + 1) * C)
        for hh in range(HGRN_N_HEADS):
            sl = slice(hh * HD, (hh + 1) * HD)
            v = v_scr[rows, sl]
            s_h = st_scr[nb, hh]
            o_scr[rows, sl] = _dot(jnp.concatenate([qe_scr[rows, sl], sc_scr[rows, hh * HD:hh * HD + C]], axis=1),
                                   jnp.concatenate([s_h.astype(BF16), v], axis=0))
            st_scr[nb, hh] = elT_scr[sl, c:c + 1] * s_h + _dot_tn(kw_scr[rows, sl], v)

    out = None
    for c0 in range(0, D_MODEL, HGRN_COLS):
        parts = []
        for h0 in range(c0, c0 + HGRN_COLS, HD):
            blk = o_scr[:, h0:h0 + HD]
            parts.append(blk * lax.rsqrt(jnp.mean(blk * blk, axis=-1, keepdims=True) + NORM_EPS))
        cols = slice(c0, c0 + HGRN_COLS)
        on = jnp.concatenate(parts, axis=1) * gnw_ref[:, cols] * g_scr[:, cols]
        part = _dot(on.astype(BF16), wo_ref[cols, :])
        out = part if out is None else out + part
    xo_ref[...] = (x + out).reshape(NB, TL, D_MODEL)

    @pl.when(i == n_i - 1)
    def _fin():
        so_ref[...] = st_scr[...]


def _hgrn_mixer(x, s0, p, *, NB, TL, C):
    n_seq, L, _ = x.shape
    has_init = s0 is not None
    R = NB * TL
    grid = (n_seq // NB, L // TL)
    st_block = (NB, HGRN_N_HEADS, HGRN_HEAD_DIM, HGRN_HEAD_DIM)
    in_specs = [pl.BlockSpec((NB, TL, D_MODEL), lambda b, i: (b, i, 0))]
    args = [x]
    if has_init:
        in_specs.append(pl.BlockSpec(st_block, lambda b, i: (b, 0, 0, 0)))
        args.append(s0)
    consts = [p["nw"], p["win"], p["lb"], p["gnw"], p["wo"]]
    for c in consts:
        in_specs.append(_layer_spec(c, 0) if c.ndim == 3 else _const_spec(c.shape))
    args += consts
    out_shape = (jax.ShapeDtypeStruct((n_seq, L, D_MODEL), F32),
                 jax.ShapeDtypeStruct((n_seq,) + st_block[1:], F32))
    out_specs = (pl.BlockSpec((NB, TL, D_MODEL), lambda b, i: (b, i, 0)),
                 pl.BlockSpec(st_block, lambda b, i: (b, 0, 0, 0)))
    scratch = [pltpu.VMEM(st_block, F32),
               pltpu.VMEM((R, R), BF16),
               pltpu.VMEM((R, D_MODEL), BF16),
               pltpu.VMEM((R, D_MODEL), BF16),
               pltpu.VMEM((R, D_MODEL), BF16),
               pltpu.VMEM((R, D_MODEL), BF16),
               pltpu.VMEM((R, D_MODEL), BF16),
               pltpu.VMEM((R, D_MODEL), F32),
               pltpu.VMEM((D_MODEL, LANES), F32),
               pltpu.VMEM((R, D_MODEL), BF16),
               pltpu.VMEM((R, D_MODEL), F32)]
    return pl.pallas_call(
        functools.partial(_hgrn_kernel, has_init=has_init, NB=NB, TL=TL, C=C),
        grid=grid, in_specs=in_specs, out_specs=out_specs, out_shape=out_shape,
        scratch_shapes=scratch,
        compiler_params=pltpu.CompilerParams(
            dimension_semantics=("arbitrary", "arbitrary"), vmem_limit_bytes=VMEM_LIMIT),
        name="hgrn_mixer_init" if has_init else "hgrn_mixer_zero",
    )(*args)


def _ffn_kernel(*refs, final, n_side):
    it = iter(refs)
    x_ref, nw_ref, wg_ref, wu_ref, wd_ref = [next(it) for _ in range(5)]
    fw_ref = next(it) if final else None
    side_in = [next(it) for _ in range(n_side)]
    o_ref = next(it)
    for w_ref in side_in:
        next(it)[...] = w_ref[...].astype(BF16)
    x = x_ref[...]
    h16 = _rms(x, nw_ref[...]).astype(BF16)
    act = _silu(_dot(h16, wg_ref[...])) * _dot(h16, wu_ref[...])
    y = x + _dot(act.astype(BF16), wd_ref[...])
    if final:
        y = _rms(y, fw_ref[...])
    o_ref[...] = y


def _ffn(x2d, nw, wg, wu, wd, final_w, side, *, TM):
    rows = x2d.shape[0]
    final = final_w is not None
    n_steps = rows // TM
    side_in, side_out, side_shapes = _side_cast(side, n_steps, lambda i: i)
    consts = [nw, wg, wu, wd] + ([final_w] if final else [])
    in_specs = [pl.BlockSpec((TM, D_MODEL), lambda i: (i, 0))] + [_const_spec(c.shape) for c in consts] + side_in
    outs = pl.pallas_call(
        functools.partial(_ffn_kernel, final=final, n_side=len(side)),
        grid=(n_steps,),
        in_specs=in_specs,
        out_specs=[pl.BlockSpec((TM, D_MODEL), lambda i: (i, 0))] + side_out,
        out_shape=[jax.ShapeDtypeStruct((rows, D_MODEL), F32)] + side_shapes,
        compiler_params=pltpu.CompilerParams(
            dimension_semantics=("arbitrary",), vmem_limit_bytes=VMEM_LIMIT),
        name="swiglu_final" if final else "swiglu",
    )(x2d, *consts, *[w for w, _ in side])
    return outs[0], list(outs[1:])


def _tiles(n_seq, L):
    if L >= SSD_LONG_TILE:
        return dict(NB=1, TL_SSD=SSD_LONG_TILE, TL_HGRN=HGRN_LONG_TILE, C=SCAN_CHUNK, C_HGRN=HGRN_CHUNK,
                    TM=FFN_ROWS)
    return dict(NB=SHORT_SEQ_PER_STEP, TL_SSD=L, TL_HGRN=L, C=min(L, SCAN_CHUNK), C_HGRN=min(L, HGRN_CHUNK),
                TM=min(FFN_ROWS, n_seq * L))


def _trunk(x, s_ssd, cv, s_hgrn, ssd_p, hgrn_p, ffn_nw, final_w, w16, raw):
    n_seq, L, _ = x.shape
    t = _tiles(n_seq, L)
    s0 = None if s_ssd is None else s_ssd.reshape(n_seq, SSD_D_INNER, SSD_D_STATE)

    def missing(names):
        return [k for k in names if k not in w16]

    need = missing(("wg0", "wu0", "wd0"))
    x, s_new, cv_new, cast = _ssd_mixer(x, s0, cv, ssd_p, [raw[k] for k in need],
                                        NB=t["NB"], TL=t["TL_SSD"], C=t["C"])
    w16.update(zip(need, cast))
    need = missing(("hgrn_in", "hgrn_out", "wg1", "wu1", "wd1"))
    x, cast = _ffn(x.reshape(n_seq * L, D_MODEL), ffn_nw[0], w16["wg0"], w16["wu0"], w16["wd0"], None,
                   [raw[k] for k in need], TM=t["TM"])
    w16.update(zip(need, cast))
    x, h_new = _hgrn_mixer(x.reshape(n_seq, L, D_MODEL), s_hgrn,
                           dict(hgrn_p, win=w16["hgrn_in"], wo=w16["hgrn_out"]),
                           NB=t["NB"], TL=t["TL_HGRN"], C=t["C_HGRN"])
    y, _ = _ffn(x.reshape(n_seq * L, D_MODEL), ffn_nw[1], w16["wg1"], w16["wu1"], w16["wd1"], final_w,
                [], TM=t["TM"])
    s_new = s_new.reshape(1, n_seq, SSD_N_HEADS, SSD_HEAD_DIM, SSD_D_STATE)
    return y.reshape(n_seq, L, D_MODEL), s_new, cv_new[None], h_new[None]


def kernel(x_prompt, x_sample, state_ssd, cache_conv, state_hgrn, ssd_norm_w, ssd_in_w, ssd_conv_w, ssd_conv_b, ssd_dt_bias, ssd_A_log, ssd_D, ssd_gnorm_w, ssd_out_w, hgrn_norm_w, hgrn_in_w, hgrn_lower_bounds, hgrn_gnorm_w, hgrn_out_w, ffn_norm_w, ffn_w_gate, ffn_w_up, ffn_w_down, final_norm_w):
    pair_order = jnp.concatenate([jnp.arange(0, SSD_N_HEADS, 2), jnp.arange(1, SSD_N_HEADS, 2)])
    ssd_p = {
        "nw": ssd_norm_w[0].reshape(1, D_MODEL),
        "win": ssd_in_w.astype(BF16),
        "cw": ssd_conv_w,
        "cb": ssd_conv_b[0].reshape(1, SSD_CONV_DIM),
        "dtb": ssd_dt_bias[0].reshape(1, SSD_N_HEADS),
        "dtbT": ssd_dt_bias[0][pair_order].reshape(SSD_N_HEADS, 1),
        "alog": ssd_A_log[0],
        "dexp": jnp.repeat(ssd_D[0], SSD_HEAD_DIM).reshape(1, SSD_D_INNER),
        "gnw": ssd_gnorm_w[0].reshape(1, SSD_D_INNER),
        "wo": ssd_out_w.astype(BF16),
    }
    hgrn_p = {
        "nw": hgrn_norm_w[0].reshape(1, D_MODEL),
        "lb": hgrn_lower_bounds,
        "gnw": jnp.tile(hgrn_gnorm_w[0], HGRN_N_HEADS).reshape(1, D_MODEL),
    }
    ffn_nw = ffn_norm_w.reshape(2, 1, D_MODEL)
    final_w = final_norm_w.reshape(1, D_MODEL)
    raw = {"wg0": (ffn_w_gate, 0), "wu0": (ffn_w_up, 0), "wd0": (ffn_w_down, 0),
           "hgrn_in": (hgrn_in_w, 0), "hgrn_out": (hgrn_out_w, 0),
           "wg1": (ffn_w_gate, 1), "wu1": (ffn_w_up, 1), "wd1": (ffn_w_down, 1)}
    w16 = {}

    y_p, ssd_s_p, conv_p, hgrn_s_p = _trunk(
        x_prompt, None, None, None, ssd_p, hgrn_p, ffn_nw, final_w, w16, raw)
    y_s, ssd_s_s, conv_s, hgrn_s_s = _trunk(
        x_sample, state_ssd[0], cache_conv[0], state_hgrn[0], ssd_p, hgrn_p, ffn_nw, final_w, w16, raw)
    return (y_p, y_s, ssd_s_p, conv_p, hgrn_s_p, ssd_s_s, conv_s, hgrn_s_s)
```

```python
import functools

import jax
import jax.numpy as jnp
from jax import lax
from jax.experimental import pallas as pl
from jax.experimental.pallas import tpu as pltpu

F32 = jnp.float32
BF16 = jnp.bfloat16

D_MODEL = 1024
NORM_EPS = 1e-6

SSD_D_INNER = 2048
SSD_HEAD_DIM = 64
SSD_N_HEADS = 32
SSD_N_GROUPS = 4
SSD_D_STATE = 128
SSD_CONV_W = 4
SSD_CONV_DIM = 3072
SSD_NORM_GROUP = 512
SSD_N_PAIRS = SSD_N_HEADS // 2
SSD_XBC_OFF = SSD_D_INNER
SSD_DT_OFF = SSD_D_INNER + SSD_CONV_DIM

HGRN_HEAD_DIM = 128
HGRN_N_HEADS = 8

FFN_HIDDEN = 2816

LANES = 128
SUBLANES = 8
BF16_TILE_ROWS = 16
MXU_DEPTH = 256
SCAN_CHUNK = 64
HGRN_CHUNK = 128
SSD_LONG_TILE = 256
HGRN_LONG_TILE = 512
SHORT_SEQ_PER_STEP = 4
FFN_ROWS = 512
CONV_COLS = 512
HGRN_COLS = 256
VMEM_LIMIT = 56 * 1024 * 1024


def _dot(a, b):
    return jnp.dot(a, b, preferred_element_type=F32)


def _dot_nt(a, b):
    return lax.dot_general(a, b, (((1,), (1,)), ((), ())), preferred_element_type=F32)


def _dot_tn(a, b):
    return lax.dot_general(a, b, (((0,), (0,)), ((), ())), preferred_element_type=F32)


def _split3(a):
    hi = a.astype(BF16)
    r1 = a - hi.astype(F32)
    mid = r1.astype(BF16)
    lo = (r1 - mid.astype(F32)).astype(BF16)
    return hi, mid, lo


def _stack3(a):
    hi = a.astype(BF16).astype(F32)
    r1 = a - hi
    mid = r1.astype(BF16).astype(F32)
    return jnp.concatenate([hi, mid, r1 - mid, jnp.zeros_like(a)], axis=1).astype(BF16)


def _rms(x, w):
    return x * lax.rsqrt(jnp.mean(x * x, axis=-1, keepdims=True) + NORM_EPS) * w


def _silu(x):
    return x * jax.nn.sigmoid(x)


def _shift_rows(u, first_rows):
    rows, cols = u.shape
    k = first_rows.shape[0]
    nv = rows // SUBLANES
    r = pltpu.roll(u.reshape(nv, SUBLANES, cols), k, axis=1)
    first = jnp.concatenate([first_rows, jnp.zeros((SUBLANES - k, cols), u.dtype)], axis=0)
    prev = jnp.concatenate([first[None], r[:nv - 1]], axis=0)
    sub = lax.broadcasted_iota(jnp.int32, (nv, SUBLANES, cols), 1)
    return jnp.where(sub < k, prev, r).reshape(rows, cols)


def _chunk_triangle(R, C):
    rr = lax.broadcasted_iota(jnp.int32, (R, R), 0)
    cc = lax.broadcasted_iota(jnp.int32, (R, R), 1)
    return ((cc <= rr) & ((rr & -C) == (cc & -C))).astype(BF16)


def _ssd_kernel(*refs, has_init, NB, TL, C, n_side):
    it = iter(refs)
    x_ref = next(it)
    s0_ref = cv0_ref = None
    if has_init:
        s0_ref = next(it)
        cv0_ref = next(it)
    (nw_ref, win_ref, cw_ref, cb_ref, dtb_ref, dtbT_ref,
     alog_ref, alogp_ref, dexp_ref, gnw_ref, wo_ref) = [next(it) for _ in range(11)]
    side_in = [next(it) for _ in range(n_side)]
    xo_ref = next(it)
    so_ref = next(it)
    cvo_ref = next(it)
    for w_ref in side_in:
        next(it)[...] = w_ref[...].astype(BF16)
    (st_scr, st16_scr, hist_scr, z_scr, xbc_scr, y_scr, e_scr, tri_scr,
     colc_scr, mix_scr, xw16_scr, xsl16_scr, xsr16_scr, bc16_scr) = [next(it) for _ in range(14)]
    es_scr = colcs_scr = None
    if 2 * C != LANES:
        es_scr = next(it)
        colcs_scr = next(it)

    R = NB * TL
    n_ch = R // C
    ch_per_seq = TL // C
    W2 = 2 * C
    i = pl.program_id(1)
    n_i = pl.num_programs(1)

    @pl.when(i == 0)
    def _init():
        ek = lax.broadcasted_iota(jnp.int32, e_scr.shape, 0)
        el = lax.broadcasted_iota(jnp.int32, e_scr.shape, 1)
        e_scr[...] = ((ek < 3 * SSD_N_HEADS) & ((ek & (SSD_N_HEADS - 1)) == el // SSD_HEAD_DIM)).astype(BF16)
        if es_scr is not None:
            ek = lax.broadcasted_iota(jnp.int32, es_scr.shape, 0)
            el = lax.broadcasted_iota(jnp.int32, es_scr.shape, 1)
            es_scr[...] = ((ek < 3 * SSD_N_HEADS) & ((ek & (SSD_N_HEADS - 1)) == el // C)).astype(BF16)
        tri_scr[...] = _chunk_triangle(R, C)
        hist_scr[...] = jnp.zeros_like(hist_scr)
        if has_init:
            for nb in range(NB):
                for g in range(SSD_N_GROUPS):
                    st_scr[nb, g] = s0_ref[nb, g * SSD_NORM_GROUP:(g + 1) * SSD_NORM_GROUP, :].T
                for k in range(SSD_CONV_DIM // CONV_COLS):
                    hist_scr[nb, k, SUBLANES - 3:SUBLANES, :] = cv0_ref[nb, :, k * CONV_COLS:(k + 1) * CONV_COLS]
        else:
            st_scr[...] = jnp.zeros_like(st_scr)
        st16_scr[...] = st_scr[...].astype(BF16)

    x = x_ref[...].reshape(R, D_MODEL)
    h16 = _rms(x, nw_ref[...]).astype(BF16)

    for cb0 in range(0, SSD_CONV_DIM, CONV_COLS):
        cols = slice(cb0, cb0 + CONV_COLS)
        xr = _dot(h16, win_ref[:, SSD_XBC_OFF + cb0:SSD_XBC_OFF + cb0 + CONV_COLS])
        w0, w1, w2, w3 = (cw_ref[k:k + 1, cols] for k in range(SSD_CONV_W))
        for nb in range(NB):
            xn = xr[nb * TL:(nb + 1) * TL]
            hm = hist_scr[nb, cb0 // CONV_COLS]
            sx = _shift_rows(xn, hm[7:8])
            a2 = _shift_rows(w1 * xn + w0 * sx, w1 * hm[6:8] + w0 * hm[5:7])
            xbc_scr[nb * TL:(nb + 1) * TL, cols] = _silu(w3 * xn + w2 * sx + a2 + cb_ref[:, cols])
            hist_scr[nb, cb0 // CONV_COLS] = xn[TL - SUBLANES:TL]
    for zb0 in range(0, SSD_D_INNER, CONV_COLS):
        z_scr[:, zb0:zb0 + CONV_COLS] = _silu(_dot(h16, win_ref[:, zb0:zb0 + CONV_COLS]))

    dt_raw = _dot(h16, win_ref[:, SSD_DT_OFF:])
    dt = jax.nn.softplus(dt_raw + dtb_ref[...])
    pi = lax.broadcasted_iota(jnp.int32, (SSD_N_HEADS, SSD_N_HEADS), 0)
    pj = lax.broadcasted_iota(jnp.int32, (SSD_N_HEADS, SSD_N_HEADS), 1)
    pick = (pj == jnp.where(pi < SSD_N_PAIRS, 2 * pi, 2 * (pi - SSD_N_PAIRS) + 1)).astype(BF16)
    dtT = jax.nn.softplus(sum(_dot_nt(pick, p) for p in _split3(dt_raw)) + dtbT_ref[...])
    dtTp = jnp.concatenate(
        [jnp.concatenate([dtT[0:SSD_N_PAIRS, c * C:(c + 1) * C],
                          dtT[SSD_N_PAIRS:, c * C:(c + 1) * C]], axis=1) for c in range(n_ch)],
        axis=0)
    a_row = -jnp.exp(alog_ref[...])
    a_pair = -jnp.exp(alogp_ref[...])
    cum = sum(_dot(tri_scr[...], p) for p in _split3(dt * a_row))
    last = jnp.concatenate(
        [jnp.broadcast_to(cum[(c + 1) * C - 1:(c + 1) * C, :], (C, SSD_N_HEADS)) for c in range(n_ch)], axis=0)
    wdt = jnp.exp(last - cum) * dt
    cum3 = _stack3(cum)
    colc_scr[...] = _dot(cum3, e_scr[...])
    xs = xbc_scr[:, :SSD_D_INNER]
    xw16_scr[...] = (xs * _dot(_stack3(wdt), e_scr[...])).astype(BF16)
    head_left = (lax.broadcasted_iota(jnp.int32, (R, SSD_D_INNER), 1) & (LANES - 1)) < SSD_HEAD_DIM
    xsl16_scr[...] = jnp.where(head_left, xs, 0.0).astype(BF16)
    xsr16_scr[...] = jnp.where(head_left, 0.0, xs).astype(BF16)
    bc16_scr[...] = xbc_scr[:, SSD_D_INNER:].astype(BF16)
    if colcs_scr is not None:
        colcs_scr[...] = _dot(cum3, es_scr[...])
    r2 = lax.broadcasted_iota(jnp.int32, (W2, W2), 0)
    c2 = lax.broadcasted_iota(jnp.int32, (W2, W2), 1)
    triT2_16 = ((r2 <= c2) & ((r2 < C) == (c2 < C))).astype(BF16)
    aTp = dtTp * jnp.concatenate([a_pair] * n_ch, axis=0)
    cumTp = sum(_dot(p, triT2_16) for p in _split3(aTp))

    tp = lax.broadcasted_iota(jnp.int32, (C, W2), 0)
    sp = lax.broadcasted_iota(jnp.int32, (C, W2), 1)
    tril_pair = jnp.where(sp < C, sp, sp - C) <= tp
    colc_l_scr = colc_scr if colcs_scr is None else colcs_scr
    for c in range(n_ch):
        rows = slice(c * C, (c + 1) * C)
        for g in range(SSD_N_GROUPS):
            b16 = bc16_scr[rows, g * SSD_D_STATE:(g + 1) * SSD_D_STATE]
            c16 = bc16_scr[rows, (SSD_N_GROUPS + g) * SSD_D_STATE:(SSD_N_GROUPS + g + 1) * SSD_D_STATE]
            cbm = jnp.where(tril_pair, _dot_nt(c16, jnp.concatenate([b16, b16], axis=0)), 0.0)
            for q in range(4):
                jp = g * 4 + q
                k = c * SSD_N_PAIRS + jp
                lanes_s = slice(jp * W2, (jp + 1) * W2)
                decay = jnp.exp(jnp.minimum(colc_l_scr[rows, lanes_s] - cumTp[k:k + 1, :], 0.0))
                mix_scr[rows, lanes_s] = (cbm * decay * dtTp[k:k + 1, :]).astype(mix_scr.dtype)
    colc_scr[...] = jnp.exp(colc_scr[...])

    for c in range(n_ch):
        nb = c // ch_per_seq
        rows = slice(c * C, (c + 1) * C)
        for g in range(SSD_N_GROUPS):
            gl = slice(g * SSD_NORM_GROUP, (g + 1) * SSD_NORM_GROUP)
            b16 = bc16_scr[rows, g * SSD_D_STATE:(g + 1) * SSD_D_STATE]
            c16 = bc16_scr[rows, (SSD_N_GROUPS + g) * SSD_D_STATE:(SSD_N_GROUPS + g + 1) * SSD_D_STATE]
            y_inter = _dot(c16, st16_scr[nb, g])
            for q in range(4):
                jp = g * 4 + q
                lanes = slice(jp * LANES, (jp + 1) * LANES)
                mix16 = mix_scr[rows, jp * W2:(jp + 1) * W2].astype(BF16)
                rhs = jnp.concatenate([xsl16_scr[rows, lanes], xsr16_scr[rows, lanes]], axis=0)
                y_scr[rows, lanes] = _dot(mix16, rhs) + colc_scr[rows, lanes] * y_inter[:, q * LANES:(q + 1) * LANES]
            el_g = colc_scr[(c + 1) * C - 1:(c + 1) * C, gl]
            st_new = el_g * st_scr[nb, g] + _dot_tn(b16, xw16_scr[rows, gl])
            st_scr[nb, g] = st_new
            st16_scr[nb, g] = st_new.astype(BF16)

    out = None
    for g in range(SSD_N_GROUPS):
        cols = slice(g * SSD_NORM_GROUP, (g + 1) * SSD_NORM_GROUP)
        yg = (y_scr[:, cols] + dexp_ref[:, cols] * xbc_scr[:, cols]) * z_scr[:, cols]
        yn = yg * lax.rsqrt(jnp.mean(yg * yg, axis=-1, keepdims=True) + NORM_EPS) * gnw_ref[:, cols]
        part = _dot(yn.astype(BF16), wo_ref[cols, :])
        out = part if out is None else out + part
    xo_ref[...] = (x + out).reshape(NB, TL, D_MODEL)

    @pl.when(i == n_i - 1)
    def _fin():
        for nb in range(NB):
            for g in range(SSD_N_GROUPS):
                so_ref[nb, g * SSD_NORM_GROUP:(g + 1) * SSD_NORM_GROUP, :] = st_scr[nb, g].T
            for k in range(SSD_CONV_DIM // CONV_COLS):
                cvo_ref[nb, :, k * CONV_COLS:(k + 1) * CONV_COLS] = hist_scr[nb, k, SUBLANES - 3:SUBLANES, :]


def _const_spec(shape):
    nd = len(shape)
    return pl.BlockSpec(shape, lambda *_: (0,) * nd, pipeline_mode=pl.Buffered(1))


def _layer_spec(arr, layer):
    nd = arr.ndim - 1
    return pl.BlockSpec((None,) + arr.shape[1:], lambda *_: (layer,) + (0,) * nd,
                        pipeline_mode=pl.Buffered(1))


def _side_cast(side, n_steps, step_of):
    in_specs, out_specs, out_shapes = [], [], []
    for w, layer in side:
        rows, cols = w.shape[1:]
        n_blk = max(n for n in range(1, n_steps + 1) if rows % n == 0 and (rows // n) % BF16_TILE_ROWS == 0)
        br = rows // n_blk
        in_specs.append(pl.BlockSpec(
            (None, br, cols), lambda *g, layer=layer, n_blk=n_blk: (layer, jnp.minimum(step_of(*g), n_blk - 1), 0)))
        out_specs.append(pl.BlockSpec(
            (br, cols), lambda *g, n_blk=n_blk: (jnp.minimum(step_of(*g), n_blk - 1), 0)))
        out_shapes.append(jax.ShapeDtypeStruct((rows, cols), BF16))
    return in_specs, out_specs, out_shapes


def _ssd_mixer(x, s0, cv0, p, side, *, NB, TL, C):
    n_seq, L, _ = x.shape
    has_init = s0 is not None
    R = NB * TL
    grid = (n_seq // NB, L // TL)
    side_in, side_out, side_shapes = _side_cast(side, grid[0] * grid[1], lambda b, i: b * grid[1] + i)
    in_specs = [pl.BlockSpec((NB, TL, D_MODEL), lambda b, i: (b, i, 0))]
    args = [x]
    if has_init:
        in_specs += [pl.BlockSpec((NB, SSD_D_INNER, SSD_D_STATE), lambda b, i: (b, 0, 0)),
                     pl.BlockSpec((NB, 3, SSD_CONV_DIM), lambda b, i: (b, 0, 0))]
        args += [s0, cv0]
    alog_pair = jnp.concatenate(
        [jnp.broadcast_to(p["alog"][0::2, None], (SSD_N_PAIRS, C)),
         jnp.broadcast_to(p["alog"][1::2, None], (SSD_N_PAIRS, C))], axis=1)
    consts = [p["nw"], p["win"], p["cw"], p["cb"], p["dtb"], p["dtbT"],
              p["alog"].reshape(1, SSD_N_HEADS), alog_pair, p["dexp"], p["gnw"], p["wo"]]
    for c in consts:
        in_specs.append(_layer_spec(c, 0) if c.ndim == 3 else _const_spec(c.shape))
    args += consts
    in_specs += side_in
    args += [w for w, _ in side]
    out_shape = [jax.ShapeDtypeStruct((n_seq, L, D_MODEL), F32),
                 jax.ShapeDtypeStruct((n_seq, SSD_D_INNER, SSD_D_STATE), F32),
                 jax.ShapeDtypeStruct((n_seq, 3, SSD_CONV_DIM), F32)] + side_shapes
    out_specs = [pl.BlockSpec((NB, TL, D_MODEL), lambda b, i: (b, i, 0)),
                 pl.BlockSpec((NB, SSD_D_INNER, SSD_D_STATE), lambda b, i: (b, 0, 0)),
                 pl.BlockSpec((NB, 3, SSD_CONV_DIM), lambda b, i: (b, 0, 0))] + side_out
    scratch = [pltpu.VMEM((NB, SSD_N_GROUPS, SSD_D_STATE, SSD_NORM_GROUP), F32),
               pltpu.VMEM((NB, SSD_N_GROUPS, SSD_D_STATE, SSD_NORM_GROUP), BF16),
               pltpu.VMEM((NB, SSD_CONV_DIM // CONV_COLS, SUBLANES, CONV_COLS), F32),
               pltpu.VMEM((R, SSD_D_INNER), F32),
               pltpu.VMEM((R, SSD_CONV_DIM), F32),
               pltpu.VMEM((R, SSD_D_INNER), F32),
               pltpu.VMEM((LANES, SSD_D_INNER), BF16),
               pltpu.VMEM((R, R), BF16),
               pltpu.VMEM((R, SSD_D_INNER), F32),
               pltpu.VMEM((R, SSD_N_PAIRS * 2 * C), BF16 if 2 * C == LANES else F32),
               pltpu.VMEM((R, SSD_D_INNER), BF16),
               pltpu.VMEM((R, SSD_D_INNER), BF16),
               pltpu.VMEM((R, SSD_D_INNER), BF16),
               pltpu.VMEM((R, 2 * SSD_N_GROUPS * SSD_D_STATE), BF16)]
    if 2 * C != LANES:
        scratch += [pltpu.VMEM((LANES, SSD_N_PAIRS * 2 * C), BF16),
                    pltpu.VMEM((R, SSD_N_PAIRS * 2 * C), F32)]
    outs = pl.pallas_call(
        functools.partial(_ssd_kernel, has_init=has_init, NB=NB, TL=TL, C=C, n_side=len(side)),
        grid=grid, in_specs=in_specs, out_specs=out_specs, out_shape=out_shape,
        scratch_shapes=scratch,
        compiler_params=pltpu.CompilerParams(
            dimension_semantics=("arbitrary", "arbitrary"), vmem_limit_bytes=VMEM_LIMIT),
        name="ssd_mixer_init" if has_init else "ssd_mixer_zero",
    )(*args)
    return outs[0], outs[1], outs[2], list(outs[3:])


def _hgrn_kernel(*refs, has_init, NB, TL, C):
    it = iter(refs)
    x_ref = next(it)
    s0_ref = next(it) if has_init else None
    nw_ref, win_ref, lb_ref, gnw_ref, wo_ref = [next(it) for _ in range(5)]
    xo_ref = next(it)
    so_ref = next(it)
    (st_scr, tri_scr, qe_scr, qm_scr, ke_scr, kw_scr, v_scr, g_scr, elT_scr, sc_scr,
     o_scr) = [next(it) for _ in range(11)]
    assert C <= HGRN_HEAD_DIM
    TB = tri_scr.shape[0]

    R = NB * TL
    n_ch = R // C
    ch_per_seq = TL // C
    HD = HGRN_HEAD_DIM
    i = pl.program_id(1)
    n_i = pl.num_programs(1)

    @pl.when(i == 0)
    def _init():
        tri_scr[...] = _chunk_triangle(TB, C)
        if has_init:
            st_scr[...] = s0_ref[...]
        else:
            st_scr[...] = jnp.zeros_like(st_scr)

    x = x_ref[...].reshape(R, D_MODEL)
    h16 = _rms(x, nw_ref[...]).astype(BF16)
    lb_soft = jax.nn.softmax(lb_ref[...], axis=0)
    lb = (lb_soft[0:1, :] + lb_soft[1:2, :]) - lb_soft[0:1, :]

    for c0 in range(0, D_MODEL, HGRN_COLS):
        cols = slice(c0, c0 + HGRN_COLS)
        q = _silu(_dot(h16, win_ref[:, c0:c0 + HGRN_COLS]))
        f = _dot(h16, win_ref[:, D_MODEL + c0:D_MODEL + c0 + HGRN_COLS])
        v = _dot(h16, win_ref[:, 2 * D_MODEL + c0:2 * D_MODEL + c0 + HGRN_COLS])
        g = _dot(h16, win_ref[:, 3 * D_MODEL + c0:3 * D_MODEL + c0 + HGRN_COLS])
        lbc = lb[:, cols]
        forget = lbc + (1.0 - lbc) * jax.nn.sigmoid(f)
        k = 1.0 - forget
        logf = _split3(jnp.log(forget))
        b = jnp.concatenate([sum(_dot(tri_scr[...], p[r0:r0 + TB]) for p in logf) for r0 in range(0, R, TB)], axis=0)
        ends = [b[(c + 1) * C - 1:(c + 1) * C, :] for c in range(n_ch)]
        last = jnp.concatenate([jnp.broadcast_to(e, (C, HGRN_COLS)) for e in ends], axis=0)
        mid = jnp.concatenate([jnp.broadcast_to(b[c * C + C // 2:c * C + C // 2 + 1, :], (C, HGRN_COLS))
                               for c in range(n_ch)], axis=0)
        qe_scr[:, cols] = (q * jnp.exp(b)).astype(BF16)
        qm_scr[:, cols] = (q * jnp.exp(b - mid)).astype(BF16)
        ke_scr[:, cols] = (k * jnp.exp(mid - b)).astype(BF16)
        kw_scr[:, cols] = (k * jnp.exp(last - b)).astype(BF16)
        v_scr[:, cols] = v.astype(BF16)
        g_scr[:, cols] = _silu(g)
        el = jnp.exp(jnp.concatenate(ends + [jnp.zeros((LANES - n_ch, HGRN_COLS), F32)], axis=0))
        elT_scr[cols, :] = el.T

    rr = lax.broadcasted_iota(jnp.int32, (C, C), 0)
    cc = lax.broadcasted_iota(jnp.int32, (C, C), 1)
    tril = cc <= rr

    for c in range(n_ch):
        rows = slice(c * C, (c + 1) * C)
        for hh in range(HGRN_N_HEADS):
            sl = slice(hh * HD, (hh + 1) * HD)
            sc = jnp.where(tril, _dot_nt(qm_scr[rows, sl], ke_scr[rows, sl]), 0.0)
            sc_scr[rows, hh * HD:hh * HD + C] = sc.astype(BF16)

    for c in range(n_ch):
        nb = c // ch_per_seq
        rows = slice(c * C, (c + 1) * C)
        for hh in range(HGRN_N_HEADS):
            sl = slice(hh * HD, (hh + 1) * HD)
            v = v_scr[rows, sl]
            s_h = st_scr[nb, hh]
            o_scr[rows, sl] = _dot(jnp.concatenate([qe_scr[rows, sl], sc_scr[rows, hh * HD:hh * HD + C]], axis=1),
                                   jnp.concatenate([s_h.astype(BF16), v], axis=0))
            st_scr[nb, hh] = elT_scr[sl, c:c + 1] * s_h + _dot_tn(kw_scr[rows, sl], v)

    out = None
    for c0 in range(0, D_MODEL, HGRN_COLS):
        parts = []
        for h0 in range(c0, c0 + HGRN_COLS, HD):
            blk = o_scr[:, h0:h0 + HD]
            parts.append(blk * lax.rsqrt(jnp.mean(blk * blk, axis=-1, keepdims=True) + NORM_EPS))
        cols = slice(c0, c0 + HGRN_COLS)
        on = jnp.concatenate(parts, axis=1) * gnw_ref[:, cols] * g_scr[:, cols]
        part = _dot(on.astype(BF16), wo_ref[cols, :])
        out = part if out is None else out + part
    xo_ref[...] = (x + out).reshape(NB, TL, D_MODEL)

    @pl.when(i == n_i - 1)
    def _fin():
        so_ref[...] = st_scr[...]


def _hgrn_mixer(x, s0, p, *, NB, TL, C):
    n_seq, L, _ = x.shape
    has_init = s0 is not None
    R = NB * TL
    grid = (n_seq // NB, L // TL)
    st_block = (NB, HGRN_N_HEADS, HGRN_HEAD_DIM, HGRN_HEAD_DIM)
    in_specs = [pl.BlockSpec((NB, TL, D_MODEL), lambda b, i: (b, i, 0))]
    args = [x]
    if has_init:
        in_specs.append(pl.BlockSpec(st_block, lambda b, i: (b, 0, 0, 0)))
        args.append(s0)
    consts = [p["nw"], p["win"], p["lb"], p["gnw"], p["wo"]]
    for c in consts:
        in_specs.append(_layer_spec(c, 0) if c.ndim == 3 else _const_spec(c.shape))
    args += consts
    out_shape = (jax.ShapeDtypeStruct((n_seq, L, D_MODEL), F32),
                 jax.ShapeDtypeStruct((n_seq,) + st_block[1:], F32))
    out_specs = (pl.BlockSpec((NB, TL, D_MODEL), lambda b, i: (b, i, 0)),
                 pl.BlockSpec(st_block, lambda b, i: (b, 0, 0, 0)))
    scratch = [pltpu.VMEM(st_block, F32),
               pltpu.VMEM((min(R, MXU_DEPTH),) * 2, BF16),
               pltpu.VMEM((R, D_MODEL), BF16),
               pltpu.VMEM((R, D_MODEL), BF16),
               pltpu.VMEM((R, D_MODEL), BF16),
               pltpu.VMEM((R, D_MODEL), BF16),
               pltpu.VMEM((R, D_MODEL), BF16),
               pltpu.VMEM((R, D_MODEL), F32),
               pltpu.VMEM((D_MODEL, LANES), F32),
               pltpu.VMEM((R, D_MODEL), BF16),
               pltpu.VMEM((R, D_MODEL), F32)]
    return pl.pallas_call(
        functools.partial(_hgrn_kernel, has_init=has_init, NB=NB, TL=TL, C=C),
        grid=grid, in_specs=in_specs, out_specs=out_specs, out_shape=out_shape,
        scratch_shapes=scratch,
        compiler_params=pltpu.CompilerParams(
            dimension_semantics=("arbitrary", "arbitrary"), vmem_limit_bytes=VMEM_LIMIT),
        name="hgrn_mixer_init" if has_init else "hgrn_mixer_zero",
    )(*args)


def _ffn_kernel(*refs, final, n_side):
    it = iter(refs)
    x_ref, nw_ref, wg_ref, wu_ref, wd_ref = [next(it) for _ in range(5)]
    fw_ref = next(it) if final else None
    side_in = [next(it) for _ in range(n_side)]
    o_ref = next(it)
    for w_ref in side_in:
        next(it)[...] = w_ref[...].astype(BF16)
    x = x_ref[...]
    h16 = _rms(x, nw_ref[...]).astype(BF16)
    act = _silu(_dot(h16, wg_ref[...])) * _dot(h16, wu_ref[...])
    y = x + _dot(act.astype(BF16), wd_ref[...])
    if final:
        y = _rms(y, fw_ref[...])
    o_ref[...] = y


def _ffn(x2d, nw, wg, wu, wd, final_w, side, *, TM):
    rows = x2d.shape[0]
    final = final_w is not None
    n_steps = rows // TM
    side_in, side_out, side_shapes = _side_cast(side, n_steps, lambda i: i)
    consts = [nw, wg, wu, wd] + ([final_w] if final else [])
    in_specs = [pl.BlockSpec((TM, D_MODEL), lambda i: (i, 0))] + [_const_spec(c.shape) for c in consts] + side_in
    outs = pl.pallas_call(
        functools.partial(_ffn_kernel, final=final, n_side=len(side)),
        grid=(n_steps,),
        in_specs=in_specs,
        out_specs=[pl.BlockSpec((TM, D_MODEL), lambda i: (i, 0))] + side_out,
        out_shape=[jax.ShapeDtypeStruct((rows, D_MODEL), F32)] + side_shapes,
        compiler_params=pltpu.CompilerParams(
            dimension_semantics=("arbitrary",), vmem_limit_bytes=VMEM_LIMIT),
        name="swiglu_final" if final else "swiglu",
    )(x2d, *consts, *[w for w, _ in side])
    return outs[0], list(outs[1:])


def _tiles(n_seq, L):
    if L >= SSD_LONG_TILE:
        return dict(NB=1, TL_SSD=SSD_LONG_TILE, TL_HGRN=HGRN_LONG_TILE, C=SCAN_CHUNK, C_HGRN=HGRN_CHUNK,
                    TM=FFN_ROWS)
    return dict(NB=SHORT_SEQ_PER_STEP, TL_SSD=L, TL_HGRN=L, C=min(L, SCAN_CHUNK), C_HGRN=min(L, HGRN_CHUNK),
                TM=min(FFN_ROWS, n_seq * L))


def _trunk(x, s_ssd, cv, s_hgrn, ssd_p, hgrn_p, ffn_nw, final_w, w16, raw):
    n_seq, L, _ = x.shape
    t = _tiles(n_seq, L)
    s0 = None if s_ssd is None else s_ssd.reshape(n_seq, SSD_D_INNER, SSD_D_STATE)

    def missing(names):
        return [k for k in names if k not in w16]

    need = missing(("wg0", "wu0", "wd0"))
    x, s_new, cv_new, cast = _ssd_mixer(x, s0, cv, ssd_p, [raw[k] for k in need],
                                        NB=t["NB"], TL=t["TL_SSD"], C=t["C"])
    w16.update(zip(need, cast))
    need = missing(("hgrn_in", "hgrn_out", "wg1", "wu1", "wd1"))
    x, cast = _ffn(x.reshape(n_seq * L, D_MODEL), ffn_nw[0], w16["wg0"], w16["wu0"], w16["wd0"], None,
                   [raw[k] for k in need], TM=t["TM"])
    w16.update(zip(need, cast))
    x, h_new = _hgrn_mixer(x.reshape(n_seq, L, D_MODEL), s_hgrn,
                           dict(hgrn_p, win=w16["hgrn_in"], wo=w16["hgrn_out"]),
                           NB=t["NB"], TL=t["TL_HGRN"], C=t["C_HGRN"])
    y, _ = _ffn(x.reshape(n_seq * L, D_MODEL), ffn_nw[1], w16["wg1"], w16["wu1"], w16["wd1"], final_w,
                [], TM=t["TM"])
    s_new = s_new.reshape(1, n_seq, SSD_N_HEADS, SSD_HEAD_DIM, SSD_D_STATE)
    return y.reshape(n_seq, L, D_MODEL), s_new, cv_new[None], h_new[None]


def kernel(x_prompt, x_sample, state_ssd, cache_conv, state_hgrn, ssd_norm_w, ssd_in_w, ssd_conv_w, ssd_conv_b, ssd_dt_bias, ssd_A_log, ssd_D, ssd_gnorm_w, ssd_out_w, hgrn_norm_w, hgrn_in_w, hgrn_lower_bounds, hgrn_gnorm_w, hgrn_out_w, ffn_norm_w, ffn_w_gate, ffn_w_up, ffn_w_down, final_norm_w):
    pair_order = jnp.concatenate([jnp.arange(0, SSD_N_HEADS, 2), jnp.arange(1, SSD_N_HEADS, 2)])
    ssd_p = {
        "nw": ssd_norm_w[0].reshape(1, D_MODEL),
        "win": ssd_in_w.astype(BF16),
        "cw": ssd_conv_w,
        "cb": ssd_conv_b[0].reshape(1, SSD_CONV_DIM),
        "dtb": ssd_dt_bias[0].reshape(1, SSD_N_HEADS),
        "dtbT": ssd_dt_bias[0][pair_order].reshape(SSD_N_HEADS, 1),
        "alog": ssd_A_log[0],
        "dexp": jnp.repeat(ssd_D[0], SSD_HEAD_DIM).reshape(1, SSD_D_INNER),
        "gnw": ssd_gnorm_w[0].reshape(1, SSD_D_INNER),
        "wo": ssd_out_w.astype(BF16),
    }
    hgrn_p = {
        "nw": hgrn_norm_w[0].reshape(1, D_MODEL),
        "lb": hgrn_lower_bounds,
        "gnw": jnp.tile(hgrn_gnorm_w[0], HGRN_N_HEADS).reshape(1, D_MODEL),
    }
    ffn_nw = ffn_norm_w.reshape(2, 1, D_MODEL)
    final_w = final_norm_w.reshape(1, D_MODEL)
    raw = {"wg0": (ffn_w_gate, 0), "wu0": (ffn_w_up, 0), "wd0": (ffn_w_down, 0),
           "hgrn_in": (hgrn_in_w, 0), "hgrn_out": (hgrn_out_w, 0),
           "wg1": (ffn_w_gate, 1), "wu1": (ffn_w_up, 1), "wd1": (ffn_w_down, 1)}
    w16 = {}

    y_p, ssd_s_p, conv_p, hgrn_s_p = _trunk(
        x_prompt, None, None, None, ssd_p, hgrn_p, ffn_nw, final_w, w16, raw)
    y_s, ssd_s_s, conv_s, hgrn_s_s = _trunk(
        x_sample, state_ssd[0], cache_conv[0], state_hgrn[0], ssd_p, hgrn_p, ffn_nw, final_w, w16, raw)
    return (y_p, y_s, ssd_s_p, conv_p, hgrn_s_p, ssd_s_s, conv_s, hgrn_s_s)
```

```python
import functools

import jax
import jax.numpy as jnp
from jax import lax
from jax.experimental import pallas as pl
from jax.experimental.pallas import tpu as pltpu

F32 = jnp.float32
BF16 = jnp.bfloat16

D_MODEL = 1024
NORM_EPS = 1e-6

SSD_D_INNER = 2048
SSD_HEAD_DIM = 64
SSD_N_HEADS = 32
SSD_N_GROUPS = 4
SSD_D_STATE = 128
SSD_CONV_W = 4
SSD_CONV_DIM = 3072
SSD_NORM_GROUP = 512
SSD_N_PAIRS = SSD_N_HEADS // 2
SSD_XBC_OFF = SSD_D_INNER
SSD_DT_OFF = SSD_D_INNER + SSD_CONV_DIM

HGRN_HEAD_DIM = 128
HGRN_N_HEADS = 8

FFN_HIDDEN = 2816

LANES = 128
SUBLANES = 8
BF16_TILE_ROWS = 16
MXU_DEPTH = 256
SCAN_CHUNK = 64
HGRN_CHUNK = 128
SSD_LONG_TILE = 256
HGRN_LONG_TILE = 512
SHORT_SEQ_PER_STEP = 4
FFN_ROWS = 512
CAST_STEPS = 8
CONV_COLS = 512
HGRN_COLS = 256
VMEM_LIMIT = 56 * 1024 * 1024


def _dot(a, b):
    return jnp.dot(a, b, preferred_element_type=F32)


def _dot_nt(a, b):
    return lax.dot_general(a, b, (((1,), (1,)), ((), ())), preferred_element_type=F32)


def _dot_tn(a, b):
    return lax.dot_general(a, b, (((0,), (0,)), ((), ())), preferred_element_type=F32)


def _split3(a):
    hi = a.astype(BF16)
    r1 = a - hi.astype(F32)
    mid = r1.astype(BF16)
    lo = (r1 - mid.astype(F32)).astype(BF16)
    return hi, mid, lo


def _stack3(a):
    hi = a.astype(BF16).astype(F32)
    r1 = a - hi
    mid = r1.astype(BF16).astype(F32)
    return jnp.concatenate([hi, mid, r1 - mid, jnp.zeros_like(a)], axis=1).astype(BF16)


def _rms(x, w):
    return x * lax.rsqrt(jnp.mean(x * x, axis=-1, keepdims=True) + NORM_EPS) * w


def _silu(x):
    return x * jax.nn.sigmoid(x)


def _shift_rows(u, first_rows):
    rows, cols = u.shape
    k = first_rows.shape[0]
    nv = rows // SUBLANES
    r = pltpu.roll(u.reshape(nv, SUBLANES, cols), k, axis=1)
    first = jnp.concatenate([first_rows, jnp.zeros((SUBLANES - k, cols), u.dtype)], axis=0)
    prev = jnp.concatenate([first[None], r[:nv - 1]], axis=0)
    sub = lax.broadcasted_iota(jnp.int32, (nv, SUBLANES, cols), 1)
    return jnp.where(sub < k, prev, r).reshape(rows, cols)


def _chunk_triangle(R, C):
    rr = lax.broadcasted_iota(jnp.int32, (R, R), 0)
    cc = lax.broadcasted_iota(jnp.int32, (R, R), 1)
    return ((cc <= rr) & ((rr & -C) == (cc & -C))).astype(BF16)


def _ssd_kernel(*refs, has_init, NB, TL, C, n_side):
    it = iter(refs)
    x_ref = next(it)
    s0_ref = cv0_ref = None
    if has_init:
        s0_ref = next(it)
        cv0_ref = next(it)
    (nw_ref, win_ref, cw_ref, cb_ref, dtb_ref, dtbT_ref,
     alog_ref, alogp_ref, dexp_ref, gnw_ref, wo_ref) = [next(it) for _ in range(11)]
    side_in = [next(it) for _ in range(n_side)]
    xo_ref = next(it)
    so_ref = next(it)
    cvo_ref = next(it)
    for w_ref in side_in:
        next(it)[...] = w_ref[...].astype(BF16)
    (st_scr, st16_scr, hist_scr, z_scr, xbc_scr, y_scr, e_scr, tri_scr,
     colc_scr, mix_scr, xw16_scr, xsl16_scr, xsr16_scr, bc16_scr) = [next(it) for _ in range(14)]
    es_scr = colcs_scr = None
    if 2 * C != LANES:
        es_scr = next(it)
        colcs_scr = next(it)

    R = NB * TL
    n_ch = R // C
    ch_per_seq = TL // C
    W2 = 2 * C
    i = pl.program_id(1)
    n_i = pl.num_programs(1)

    @pl.when(i == 0)
    def _init():
        ek = lax.broadcasted_iota(jnp.int32, e_scr.shape, 0)
        el = lax.broadcasted_iota(jnp.int32, e_scr.shape, 1)
        e_scr[...] = ((ek < 3 * SSD_N_HEADS) & ((ek & (SSD_N_HEADS - 1)) == el // SSD_HEAD_DIM)).astype(BF16)
        if es_scr is not None:
            ek = lax.broadcasted_iota(jnp.int32, es_scr.shape, 0)
            el = lax.broadcasted_iota(jnp.int32, es_scr.shape, 1)
            es_scr[...] = ((ek < 3 * SSD_N_HEADS) & ((ek & (SSD_N_HEADS - 1)) == el // C)).astype(BF16)
        tri_scr[...] = _chunk_triangle(R, C)
        hist_scr[...] = jnp.zeros_like(hist_scr)
        if has_init:
            for nb in range(NB):
                for g in range(SSD_N_GROUPS):
                    st_scr[nb, g] = s0_ref[nb, g * SSD_NORM_GROUP:(g + 1) * SSD_NORM_GROUP, :].T
                for k in range(SSD_CONV_DIM // CONV_COLS):
                    hist_scr[nb, k, SUBLANES - 3:SUBLANES, :] = cv0_ref[nb, :, k * CONV_COLS:(k + 1) * CONV_COLS]
        else:
            st_scr[...] = jnp.zeros_like(st_scr)
        st16_scr[...] = st_scr[...].astype(BF16)

    x = x_ref[...].reshape(R, D_MODEL)
    h16 = _rms(x, nw_ref[...]).astype(BF16)

    for cb0 in range(0, SSD_CONV_DIM, CONV_COLS):
        cols = slice(cb0, cb0 + CONV_COLS)
        xr = _dot(h16, win_ref[:, SSD_XBC_OFF + cb0:SSD_XBC_OFF + cb0 + CONV_COLS])
        w0, w1, w2, w3 = (cw_ref[k:k + 1, cols] for k in range(SSD_CONV_W))
        for nb in range(NB):
            xn = xr[nb * TL:(nb + 1) * TL]
            hm = hist_scr[nb, cb0 // CONV_COLS]
            sx = _shift_rows(xn, hm[7:8])
            a2 = _shift_rows(w1 * xn + w0 * sx, w1 * hm[6:8] + w0 * hm[5:7])
            xbc_scr[nb * TL:(nb + 1) * TL, cols] = _silu(w3 * xn + w2 * sx + a2 + cb_ref[:, cols])
            hist_scr[nb, cb0 // CONV_COLS] = xn[TL - SUBLANES:TL]
    for zb0 in range(0, SSD_D_INNER, CONV_COLS):
        z_scr[:, zb0:zb0 + CONV_COLS] = _silu(_dot(h16, win_ref[:, zb0:zb0 + CONV_COLS]))

    dt_raw = _dot(h16, win_ref[:, SSD_DT_OFF:])
    dt = jax.nn.softplus(dt_raw + dtb_ref[...])
    pi = lax.broadcasted_iota(jnp.int32, (SSD_N_HEADS, SSD_N_HEADS), 0)
    pj = lax.broadcasted_iota(jnp.int32, (SSD_N_HEADS, SSD_N_HEADS), 1)
    pick = (pj == jnp.where(pi < SSD_N_PAIRS, 2 * pi, 2 * (pi - SSD_N_PAIRS) + 1)).astype(BF16)
    dtT = jax.nn.softplus(sum(_dot_nt(pick, p) for p in _split3(dt_raw)) + dtbT_ref[...])
    dtTp = jnp.concatenate(
        [jnp.concatenate([dtT[0:SSD_N_PAIRS, c * C:(c + 1) * C],
                          dtT[SSD_N_PAIRS:, c * C:(c + 1) * C]], axis=1) for c in range(n_ch)],
        axis=0)
    a_row = -jnp.exp(alog_ref[...])
    a_pair = -jnp.exp(alogp_ref[...])
    cum = sum(_dot(tri_scr[...], p) for p in _split3(dt * a_row))
    last = jnp.concatenate(
        [jnp.broadcast_to(cum[(c + 1) * C - 1:(c + 1) * C, :], (C, SSD_N_HEADS)) for c in range(n_ch)], axis=0)
    wdt = jnp.exp(last - cum) * dt
    cum3 = _stack3(cum)
    colc_scr[...] = _dot(cum3, e_scr[...])
    xs = xbc_scr[:, :SSD_D_INNER]
    xw16_scr[...] = (xs * _dot(_stack3(wdt), e_scr[...])).astype(BF16)
    head_left = (lax.broadcasted_iota(jnp.int32, (R, SSD_D_INNER), 1) & (LANES - 1)) < SSD_HEAD_DIM
    xsl16_scr[...] = jnp.where(head_left, xs, 0.0).astype(BF16)
    xsr16_scr[...] = jnp.where(head_left, 0.0, xs).astype(BF16)
    bc16_scr[...] = xbc_scr[:, SSD_D_INNER:].astype(BF16)
    if colcs_scr is not None:
        colcs_scr[...] = _dot(cum3, es_scr[...])
    r2 = lax.broadcasted_iota(jnp.int32, (W2, W2), 0)
    c2 = lax.broadcasted_iota(jnp.int32, (W2, W2), 1)
    triT2_16 = ((r2 <= c2) & ((r2 < C) == (c2 < C))).astype(BF16)
    aTp = dtTp * jnp.concatenate([a_pair] * n_ch, axis=0)
    cumTp = sum(_dot(p, triT2_16) for p in _split3(aTp))

    tp = lax.broadcasted_iota(jnp.int32, (C, W2), 0)
    sp = lax.broadcasted_iota(jnp.int32, (C, W2), 1)
    tril_pair = jnp.where(sp < C, sp, sp - C) <= tp
    colc_l_scr = colc_scr if colcs_scr is None else colcs_scr
    for c in range(n_ch):
        rows = slice(c * C, (c + 1) * C)
        for g in range(SSD_N_GROUPS):
            b16 = bc16_scr[rows, g * SSD_D_STATE:(g + 1) * SSD_D_STATE]
            c16 = bc16_scr[rows, (SSD_N_GROUPS + g) * SSD_D_STATE:(SSD_N_GROUPS + g + 1) * SSD_D_STATE]
            cbm = jnp.where(tril_pair, _dot_nt(c16, jnp.concatenate([b16, b16], axis=0)), 0.0)
            for q in range(4):
                jp = g * 4 + q
                k = c * SSD_N_PAIRS + jp
                lanes_s = slice(jp * W2, (jp + 1) * W2)
                decay = jnp.exp(jnp.minimum(colc_l_scr[rows, lanes_s] - cumTp[k:k + 1, :], 0.0))
                mix_scr[rows, lanes_s] = (cbm * decay * dtTp[k:k + 1, :]).astype(mix_scr.dtype)
    colc_scr[...] = jnp.exp(colc_scr[...])

    for c in range(n_ch):
        nb = c // ch_per_seq
        rows = slice(c * C, (c + 1) * C)
        for g in range(SSD_N_GROUPS):
            gl = slice(g * SSD_NORM_GROUP, (g + 1) * SSD_NORM_GROUP)
            b16 = bc16_scr[rows, g * SSD_D_STATE:(g + 1) * SSD_D_STATE]
            c16 = bc16_scr[rows, (SSD_N_GROUPS + g) * SSD_D_STATE:(SSD_N_GROUPS + g + 1) * SSD_D_STATE]
            y_inter = _dot(c16, st16_scr[nb, g])
            for q in range(4):
                jp = g * 4 + q
                lanes = slice(jp * LANES, (jp + 1) * LANES)
                mix16 = mix_scr[rows, jp * W2:(jp + 1) * W2].astype(BF16)
                rhs = jnp.concatenate([xsl16_scr[rows, lanes], xsr16_scr[rows, lanes]], axis=0)
                y_scr[rows, lanes] = _dot(mix16, rhs) + colc_scr[rows, lanes] * y_inter[:, q * LANES:(q + 1) * LANES]
            el_g = colc_scr[(c + 1) * C - 1:(c + 1) * C, gl]
            st_new = el_g * st_scr[nb, g] + _dot_tn(b16, xw16_scr[rows, gl])
            st_scr[nb, g] = st_new
            st16_scr[nb, g] = st_new.astype(BF16)

    out = None
    for g in range(SSD_N_GROUPS):
        cols = slice(g * SSD_NORM_GROUP, (g + 1) * SSD_NORM_GROUP)
        yg = (y_scr[:, cols] + dexp_ref[:, cols] * xbc_scr[:, cols]) * z_scr[:, cols]
        yn = yg * lax.rsqrt(jnp.mean(yg * yg, axis=-1, keepdims=True) + NORM_EPS) * gnw_ref[:, cols]
        part = _dot(yn.astype(BF16), wo_ref[cols, :])
        out = part if out is None else out + part
    xo_ref[...] = (x + out).reshape(NB, TL, D_MODEL)

    @pl.when(i == n_i - 1)
    def _fin():
        for nb in range(NB):
            for g in range(SSD_N_GROUPS):
                so_ref[nb, g * SSD_NORM_GROUP:(g + 1) * SSD_NORM_GROUP, :] = st_scr[nb, g].T
            for k in range(SSD_CONV_DIM // CONV_COLS):
                cvo_ref[nb, :, k * CONV_COLS:(k + 1) * CONV_COLS] = hist_scr[nb, k, SUBLANES - 3:SUBLANES, :]


def _const_spec(shape):
    nd = len(shape)
    return pl.BlockSpec(shape, lambda *_: (0,) * nd, pipeline_mode=pl.Buffered(1))


def _layer_spec(arr, layer):
    nd = arr.ndim - 1
    return pl.BlockSpec((None,) + arr.shape[1:], lambda *_: (layer,) + (0,) * nd,
                        pipeline_mode=pl.Buffered(1))


def _side_cast(side, n_steps, step_of):
    in_specs, out_specs, out_shapes = [], [], []
    for w, layer in side:
        rows, cols = w.shape[1:]
        n_blk = max(n for n in range(1, n_steps + 1) if rows % n == 0 and (rows // n) % BF16_TILE_ROWS == 0)
        br = rows // n_blk
        in_specs.append(pl.BlockSpec(
            (None, br, cols), lambda *g, layer=layer, n_blk=n_blk: (layer, jnp.minimum(step_of(*g), n_blk - 1), 0)))
        out_specs.append(pl.BlockSpec(
            (br, cols), lambda *g, n_blk=n_blk: (jnp.minimum(step_of(*g), n_blk - 1), 0)))
        out_shapes.append(jax.ShapeDtypeStruct((rows, cols), BF16))
    return in_specs, out_specs, out_shapes


def _cast_kernel(*refs):
    n = len(refs) // 2
    for w_ref, o_ref in zip(refs[:n], refs[n:]):
        o_ref[...] = w_ref[...].astype(BF16)


def _cast_bf16(side):
    side_in, side_out, side_shapes = _side_cast(side, CAST_STEPS, lambda i: i)
    return pl.pallas_call(
        _cast_kernel, grid=(CAST_STEPS,), in_specs=side_in, out_specs=side_out, out_shape=side_shapes,
        compiler_params=pltpu.CompilerParams(dimension_semantics=("arbitrary",), vmem_limit_bytes=VMEM_LIMIT),
        name="cast_bf16",
    )(*[w for w, _ in side])


def _ssd_mixer(x, s0, cv0, p, side, *, NB, TL, C):
    n_seq, L, _ = x.shape
    has_init = s0 is not None
    R = NB * TL
    grid = (n_seq // NB, L // TL)
    side_in, side_out, side_shapes = _side_cast(side, grid[0] * grid[1], lambda b, i: b * grid[1] + i)
    in_specs = [pl.BlockSpec((NB, TL, D_MODEL), lambda b, i: (b, i, 0))]
    args = [x]
    if has_init:
        in_specs += [pl.BlockSpec((NB, SSD_D_INNER, SSD_D_STATE), lambda b, i: (b, 0, 0)),
                     pl.BlockSpec((NB, 3, SSD_CONV_DIM), lambda b, i: (b, 0, 0))]
        args += [s0, cv0]
    alog_pair = jnp.concatenate(
        [jnp.broadcast_to(p["alog"][0::2, None], (SSD_N_PAIRS, C)),
         jnp.broadcast_to(p["alog"][1::2, None], (SSD_N_PAIRS, C))], axis=1)
    consts = [p["nw"], p["win"], p["cw"], p["cb"], p["dtb"], p["dtbT"],
              p["alog"].reshape(1, SSD_N_HEADS), alog_pair, p["dexp"], p["gnw"], p["wo"]]
    for c in consts:
        in_specs.append(_layer_spec(c, 0) if c.ndim == 3 else _const_spec(c.shape))
    args += consts
    in_specs += side_in
    args += [w for w, _ in side]
    out_shape = [jax.ShapeDtypeStruct((n_seq, L, D_MODEL), F32),
                 jax.ShapeDtypeStruct((n_seq, SSD_D_INNER, SSD_D_STATE), F32),
                 jax.ShapeDtypeStruct((n_seq, 3, SSD_CONV_DIM), F32)] + side_shapes
    out_specs = [pl.BlockSpec((NB, TL, D_MODEL), lambda b, i: (b, i, 0)),
                 pl.BlockSpec((NB, SSD_D_INNER, SSD_D_STATE), lambda b, i: (b, 0, 0)),
                 pl.BlockSpec((NB, 3, SSD_CONV_DIM), lambda b, i: (b, 0, 0))] + side_out
    scratch = [pltpu.VMEM((NB, SSD_N_GROUPS, SSD_D_STATE, SSD_NORM_GROUP), F32),
               pltpu.VMEM((NB, SSD_N_GROUPS, SSD_D_STATE, SSD_NORM_GROUP), BF16),
               pltpu.VMEM((NB, SSD_CONV_DIM // CONV_COLS, SUBLANES, CONV_COLS), F32),
               pltpu.VMEM((R, SSD_D_INNER), F32),
               pltpu.VMEM((R, SSD_CONV_DIM), F32),
               pltpu.VMEM((R, SSD_D_INNER), F32),
               pltpu.VMEM((LANES, SSD_D_INNER), BF16),
               pltpu.VMEM((R, R), BF16),
               pltpu.VMEM((R, SSD_D_INNER), F32),
               pltpu.VMEM((R, SSD_N_PAIRS * 2 * C), BF16 if 2 * C == LANES else F32),
               pltpu.VMEM((R, SSD_D_INNER), BF16),
               pltpu.VMEM((R, SSD_D_INNER), BF16),
               pltpu.VMEM((R, SSD_D_INNER), BF16),
               pltpu.VMEM((R, 2 * SSD_N_GROUPS * SSD_D_STATE), BF16)]
    if 2 * C != LANES:
        scratch += [pltpu.VMEM((LANES, SSD_N_PAIRS * 2 * C), BF16),
                    pltpu.VMEM((R, SSD_N_PAIRS * 2 * C), F32)]
    outs = pl.pallas_call(
        functools.partial(_ssd_kernel, has_init=has_init, NB=NB, TL=TL, C=C, n_side=len(side)),
        grid=grid, in_specs=in_specs, out_specs=out_specs, out_shape=out_shape,
        scratch_shapes=scratch,
        compiler_params=pltpu.CompilerParams(
            dimension_semantics=("arbitrary", "arbitrary"), vmem_limit_bytes=VMEM_LIMIT),
        name="ssd_mixer_init" if has_init else "ssd_mixer_zero",
    )(*args)
    return outs[0], outs[1], outs[2], list(outs[3:])


def _hgrn_kernel(*refs, has_init, NB, TL, C):
    it = iter(refs)
    x_ref = next(it)
    s0_ref = next(it) if has_init else None
    nw_ref, win_ref, lb_ref, gnw_ref, wo_ref = [next(it) for _ in range(5)]
    xo_ref = next(it)
    so_ref = next(it)
    (st_scr, tri_scr, qe_scr, qm_scr, ke_scr, kw_scr, v_scr, g_scr, elT_scr, sc_scr,
     o_scr) = [next(it) for _ in range(11)]
    assert C <= HGRN_HEAD_DIM
    TB = tri_scr.shape[0]

    R = NB * TL
    n_ch = R // C
    ch_per_seq = TL // C
    HD = HGRN_HEAD_DIM
    i = pl.program_id(1)
    n_i = pl.num_programs(1)

    @pl.when(i == 0)
    def _init():
        tri_scr[...] = _chunk_triangle(TB, C)
        if has_init:
            st_scr[...] = s0_ref[...]
        else:
            st_scr[...] = jnp.zeros_like(st_scr)

    x = x_ref[...].reshape(R, D_MODEL)
    h16 = _rms(x, nw_ref[...]).astype(BF16)
    lb_soft = jax.nn.softmax(lb_ref[...], axis=0)
    lb = (lb_soft[0:1, :] + lb_soft[1:2, :]) - lb_soft[0:1, :]

    for c0 in range(0, D_MODEL, HGRN_COLS):
        cols = slice(c0, c0 + HGRN_COLS)
        q = _silu(_dot(h16, win_ref[:, c0:c0 + HGRN_COLS]))
        f = _dot(h16, win_ref[:, D_MODEL + c0:D_MODEL + c0 + HGRN_COLS])
        v = _dot(h16, win_ref[:, 2 * D_MODEL + c0:2 * D_MODEL + c0 + HGRN_COLS])
        g = _dot(h16, win_ref[:, 3 * D_MODEL + c0:3 * D_MODEL + c0 + HGRN_COLS])
        lbc = lb[:, cols]
        forget = lbc + (1.0 - lbc) * jax.nn.sigmoid(f)
        k = 1.0 - forget
        logf = _split3(jnp.log(forget))
        b = jnp.concatenate([sum(_dot(tri_scr[...], p[r0:r0 + TB]) for p in logf) for r0 in range(0, R, TB)], axis=0)
        ends = [b[(c + 1) * C - 1:(c + 1) * C, :] for c in range(n_ch)]
        last = jnp.concatenate([jnp.broadcast_to(e, (C, HGRN_COLS)) for e in ends], axis=0)
        mid = jnp.concatenate([jnp.broadcast_to(b[c * C + C // 2:c * C + C // 2 + 1, :], (C, HGRN_COLS))
                               for c in range(n_ch)], axis=0)
        qe_scr[:, cols] = (q * jnp.exp(b)).astype(BF16)
        qm_scr[:, cols] = (q * jnp.exp(b - mid)).astype(BF16)
        ke_scr[:, cols] = (k * jnp.exp(mid - b)).astype(BF16)
        kw_scr[:, cols] = (k * jnp.exp(last - b)).astype(BF16)
        v_scr[:, cols] = v.astype(BF16)
        g_scr[:, cols] = _silu(g)
        el = jnp.exp(jnp.concatenate(ends + [jnp.zeros((LANES - n_ch, HGRN_COLS), F32)], axis=0))
        elT_scr[cols, :] = el.T

    rr = lax.broadcasted_iota(jnp.int32, (C, C), 0)
    cc = lax.broadcasted_iota(jnp.int32, (C, C), 1)
    tril = cc <= rr

    for c in range(n_ch):
        rows = slice(c * C, (c + 1) * C)
        for hh in range(HGRN_N_HEADS):
            sl = slice(hh * HD, (hh + 1) * HD)
            sc = jnp.where(tril, _dot_nt(qm_scr[rows, sl], ke_scr[rows, sl]), 0.0)
            sc_scr[rows, hh * HD:hh * HD + C] = sc.astype(BF16)

    for c in range(n_ch):
        nb = c // ch_per_seq
        rows = slice(c * C, (c + 1) * C)
        for hh in range(HGRN_N_HEADS):
            sl = slice(hh * HD, (hh + 1) * HD)
            v = v_scr[rows, sl]
            s_h = st_scr[nb, hh]
            o_scr[rows, sl] = _dot(jnp.concatenate([qe_scr[rows, sl], sc_scr[rows, hh * HD:hh * HD + C]], axis=1),
                                   jnp.concatenate([s_h.astype(BF16), v], axis=0))
            st_scr[nb, hh] = elT_scr[sl, c:c + 1] * s_h + _dot_tn(kw_scr[rows, sl], v)

    out = None
    for c0 in range(0, D_MODEL, HGRN_COLS):
        parts = []
        for h0 in range(c0, c0 + HGRN_COLS, HD):
            blk = o_scr[:, h0:h0 + HD]
            parts.append(blk * lax.rsqrt(jnp.mean(blk * blk, axis=-1, keepdims=True) + NORM_EPS))
        cols = slice(c0, c0 + HGRN_COLS)
        on = jnp.concatenate(parts, axis=1) * gnw_ref[:, cols] * g_scr[:, cols]
        part = _dot(on.astype(BF16), wo_ref[cols, :])
        out = part if out is None else out + part
    xo_ref[...] = (x + out).reshape(NB, TL, D_MODEL)

    @pl.when(i == n_i - 1)
    def _fin():
        so_ref[...] = st_scr[...]


def _hgrn_mixer(x, s0, p, *, NB, TL, C):
    n_seq, L, _ = x.shape
    has_init = s0 is not None
    R = NB * TL
    grid = (n_seq // NB, L // TL)
    st_block = (NB, HGRN_N_HEADS, HGRN_HEAD_DIM, HGRN_HEAD_DIM)
    in_specs = [pl.BlockSpec((NB, TL, D_MODEL), lambda b, i: (b, i, 0))]
    args = [x]
    if has_init:
        in_specs.append(pl.BlockSpec(st_block, lambda b, i: (b, 0, 0, 0)))
        args.append(s0)
    consts = [p["nw"], p["win"], p["lb"], p["gnw"], p["wo"]]
    for c in consts:
        in_specs.append(_layer_spec(c, 0) if c.ndim == 3 else _const_spec(c.shape))
    args += consts
    out_shape = (jax.ShapeDtypeStruct((n_seq, L, D_MODEL), F32),
                 jax.ShapeDtypeStruct((n_seq,) + st_block[1:], F32))
    out_specs = (pl.BlockSpec((NB, TL, D_MODEL), lambda b, i: (b, i, 0)),
                 pl.BlockSpec(st_block, lambda b, i: (b, 0, 0, 0)))
    scratch = [pltpu.VMEM(st_block, F32),
               pltpu.VMEM((min(R, MXU_DEPTH),) * 2, BF16),
               pltpu.VMEM((R, D_MODEL), BF16),
               pltpu.VMEM((R, D_MODEL), BF16),
               pltpu.VMEM((R, D_MODEL), BF16),
               pltpu.VMEM((R, D_MODEL), BF16),
               pltpu.VMEM((R, D_MODEL), BF16),
               pltpu.VMEM((R, D_MODEL), F32),
               pltpu.VMEM((D_MODEL, LANES), F32),
               pltpu.VMEM((R, D_MODEL), BF16),
               pltpu.VMEM((R, D_MODEL), F32)]
    return pl.pallas_call(
        functools.partial(_hgrn_kernel, has_init=has_init, NB=NB, TL=TL, C=C),
        grid=grid, in_specs=in_specs, out_specs=out_specs, out_shape=out_shape,
        scratch_shapes=scratch,
        compiler_params=pltpu.CompilerParams(
            dimension_semantics=("arbitrary", "arbitrary"), vmem_limit_bytes=VMEM_LIMIT),
        name="hgrn_mixer_init" if has_init else "hgrn_mixer_zero",
    )(*args)


def _ffn_kernel(*refs, final, n_side):
    it = iter(refs)
    x_ref, nw_ref, wg_ref, wu_ref, wd_ref = [next(it) for _ in range(5)]
    fw_ref = next(it) if final else None
    side_in = [next(it) for _ in range(n_side)]
    o_ref = next(it)
    for w_ref in side_in:
        next(it)[...] = w_ref[...].astype(BF16)
    x = x_ref[...]
    h16 = _rms(x, nw_ref[...]).astype(BF16)
    act = _silu(_dot(h16, wg_ref[...])) * _dot(h16, wu_ref[...])
    y = x + _dot(act.astype(BF16), wd_ref[...])
    if final:
        y = _rms(y, fw_ref[...])
    o_ref[...] = y


def _ffn(x2d, nw, wg, wu, wd, final_w, side, *, TM):
    rows = x2d.shape[0]
    final = final_w is not None
    n_steps = rows // TM
    side_in, side_out, side_shapes = _side_cast(side, n_steps, lambda i: i)
    consts = [nw, wg, wu, wd] + ([final_w] if final else [])
    in_specs = [pl.BlockSpec((TM, D_MODEL), lambda i: (i, 0))] + [_const_spec(c.shape) for c in consts] + side_in
    outs = pl.pallas_call(
        functools.partial(_ffn_kernel, final=final, n_side=len(side)),
        grid=(n_steps,),
        in_specs=in_specs,
        out_specs=[pl.BlockSpec((TM, D_MODEL), lambda i: (i, 0))] + side_out,
        out_shape=[jax.ShapeDtypeStruct((rows, D_MODEL), F32)] + side_shapes,
        compiler_params=pltpu.CompilerParams(
            dimension_semantics=("arbitrary",), vmem_limit_bytes=VMEM_LIMIT),
        name="swiglu_final" if final else "swiglu",
    )(x2d, *consts, *[w for w, _ in side])
    return outs[0], list(outs[1:])


def _tiles(n_seq, L):
    if L >= SSD_LONG_TILE:
        return dict(NB=1, TL_SSD=SSD_LONG_TILE, TL_HGRN=HGRN_LONG_TILE, C=SCAN_CHUNK, C_HGRN=HGRN_CHUNK,
                    TM=FFN_ROWS)
    return dict(NB=SHORT_SEQ_PER_STEP, TL_SSD=L, TL_HGRN=L, C=min(L, SCAN_CHUNK), C_HGRN=min(L, HGRN_CHUNK),
                TM=min(FFN_ROWS, n_seq * L))


def _trunk(x, s_ssd, cv, s_hgrn, ssd_p, hgrn_p, ffn_nw, final_w, w16, raw):
    n_seq, L, _ = x.shape
    t = _tiles(n_seq, L)
    s0 = None if s_ssd is None else s_ssd.reshape(n_seq, SSD_D_INNER, SSD_D_STATE)

    def missing(names):
        return [k for k in names if k not in w16]

    need = missing(("wg0", "wu0", "wd0"))
    x, s_new, cv_new, cast = _ssd_mixer(x, s0, cv, ssd_p, [raw[k] for k in need],
                                        NB=t["NB"], TL=t["TL_SSD"], C=t["C"])
    w16.update(zip(need, cast))
    need = missing(("hgrn_in", "hgrn_out", "wg1", "wu1", "wd1"))
    x, cast = _ffn(x.reshape(n_seq * L, D_MODEL), ffn_nw[0], w16["wg0"], w16["wu0"], w16["wd0"], None,
                   [raw[k] for k in need], TM=t["TM"])
    w16.update(zip(need, cast))
    x, h_new = _hgrn_mixer(x.reshape(n_seq, L, D_MODEL), s_hgrn,
                           dict(hgrn_p, win=w16["hgrn_in"], wo=w16["hgrn_out"]),
                           NB=t["NB"], TL=t["TL_HGRN"], C=t["C_HGRN"])
    y, _ = _ffn(x.reshape(n_seq * L, D_MODEL), ffn_nw[1], w16["wg1"], w16["wu1"], w16["wd1"], final_w,
                [], TM=t["TM"])
    s_new = s_new.reshape(1, n_seq, SSD_N_HEADS, SSD_HEAD_DIM, SSD_D_STATE)
    return y.reshape(n_seq, L, D_MODEL), s_new, cv_new[None], h_new[None]


def kernel(x_prompt, x_sample, state_ssd, cache_conv, state_hgrn, ssd_norm_w, ssd_in_w, ssd_conv_w, ssd_conv_b, ssd_dt_bias, ssd_A_log, ssd_D, ssd_gnorm_w, ssd_out_w, hgrn_norm_w, hgrn_in_w, hgrn_lower_bounds, hgrn_gnorm_w, hgrn_out_w, ffn_norm_w, ffn_w_gate, ffn_w_up, ffn_w_down, final_norm_w):
    pair_order = jnp.concatenate([jnp.arange(0, SSD_N_HEADS, 2), jnp.arange(1, SSD_N_HEADS, 2)])
    ssd_in16, ssd_out16 = _cast_bf16([(ssd_in_w, 0), (ssd_out_w, 0)])
    ssd_p = {
        "nw": ssd_norm_w[0].reshape(1, D_MODEL),
        "win": ssd_in16,
        "cw": ssd_conv_w,
        "cb": ssd_conv_b[0].reshape(1, SSD_CONV_DIM),
        "dtb": ssd_dt_bias[0].reshape(1, SSD_N_HEADS),
        "dtbT": ssd_dt_bias[0][pair_order].reshape(SSD_N_HEADS, 1),
        "alog": ssd_A_log[0],
        "dexp": jnp.repeat(ssd_D[0], SSD_HEAD_DIM).reshape(1, SSD_D_INNER),
        "gnw": ssd_gnorm_w[0].reshape(1, SSD_D_INNER),
        "wo": ssd_out16,
    }
    hgrn_p = {
        "nw": hgrn_norm_w[0].reshape(1, D_MODEL),
        "lb": hgrn_lower_bounds,
        "gnw": jnp.tile(hgrn_gnorm_w[0], HGRN_N_HEADS).reshape(1, D_MODEL),
    }
    ffn_nw = ffn_norm_w.reshape(2, 1, D_MODEL)
    final_w = final_norm_w.reshape(1, D_MODEL)
    raw = {"wg0": (ffn_w_gate, 0), "wu0": (ffn_w_up, 0), "wd0": (ffn_w_down, 0),
           "hgrn_in": (hgrn_in_w, 0), "hgrn_out": (hgrn_out_w, 0),
           "wg1": (ffn_w_gate, 1), "wu1": (ffn_w_up, 1), "wd1": (ffn_w_down, 1)}
    w16 = {}

    y_p, ssd_s_p, conv_p, hgrn_s_p = _trunk(
        x_prompt, None, None, None, ssd_p, hgrn_p, ffn_nw, final_w, w16, raw)
    y_s, ssd_s_s, conv_s, hgrn_s_s = _trunk(
        x_sample, state_ssd[0], cache_conv[0], state_hgrn[0], ssd_p, hgrn_p, ffn_nw, final_w, w16, raw)
    return (y_p, y_s, ssd_s_p, conv_p, hgrn_s_p, ssd_s_s, conv_s, hgrn_s_s)
```

```python
import functools

import jax
import jax.numpy as jnp
from jax import lax
from jax.experimental import pallas as pl
from jax.experimental.pallas import tpu as pltpu

F32 = jnp.float32
BF16 = jnp.bfloat16

D_MODEL = 1024
NORM_EPS = 1e-6

SSD_D_INNER = 2048
SSD_HEAD_DIM = 64
SSD_N_HEADS = 32
SSD_N_GROUPS = 4
SSD_D_STATE = 128
SSD_CONV_W = 4
SSD_CONV_DIM = 3072
SSD_NORM_GROUP = 512
SSD_N_PAIRS = SSD_N_HEADS // 2
SSD_XBC_OFF = SSD_D_INNER
SSD_DT_OFF = SSD_D_INNER + SSD_CONV_DIM
SSD_IN_COLS = SSD_DT_OFF + SSD_N_HEADS

HGRN_HEAD_DIM = 128
HGRN_N_HEADS = 8

FFN_HIDDEN = 2816

LANES = 128
SUBLANES = 8
BF16_TILE_ROWS = 16
MXU_DEPTH = 256
SCAN_CHUNK = 64
HGRN_CHUNK = 128
SSD_LONG_TILE = 256
HGRN_LONG_TILE = 512
SHORT_SEQ_PER_STEP = 4
FFN_ROWS = 512
CONV_COLS = 512
HGRN_COLS = 256
VMEM_LIMIT = 56 * 1024 * 1024


def _dot(a, b):
    return jnp.dot(a, b, preferred_element_type=F32)


def _dot_nt(a, b):
    return lax.dot_general(a, b, (((1,), (1,)), ((), ())), preferred_element_type=F32)


def _dot_tn(a, b):
    return lax.dot_general(a, b, (((0,), (0,)), ((), ())), preferred_element_type=F32)


def _split3(a):
    hi = a.astype(BF16)
    r1 = a - hi.astype(F32)
    mid = r1.astype(BF16)
    lo = (r1 - mid.astype(F32)).astype(BF16)
    return hi, mid, lo


def _stack3(a):
    hi = a.astype(BF16).astype(F32)
    r1 = a - hi
    mid = r1.astype(BF16).astype(F32)
    return jnp.concatenate([hi, mid, r1 - mid, jnp.zeros_like(a)], axis=1).astype(BF16)


def _rms(x, w):
    return x * lax.rsqrt(jnp.mean(x * x, axis=-1, keepdims=True) + NORM_EPS) * w


def _silu(x):
    return x * jax.nn.sigmoid(x)


def _shift_rows(u, first_rows):
    rows, cols = u.shape
    k = first_rows.shape[0]
    nv = rows // SUBLANES
    r = pltpu.roll(u.reshape(nv, SUBLANES, cols), k, axis=1)
    first = jnp.concatenate([first_rows, jnp.zeros((SUBLANES - k, cols), u.dtype)], axis=0)
    prev = jnp.concatenate([first[None], r[:nv - 1]], axis=0)
    sub = lax.broadcasted_iota(jnp.int32, (nv, SUBLANES, cols), 1)
    return jnp.where(sub < k, prev, r).reshape(rows, cols)


def _chunk_triangle(R, C):
    rr = lax.broadcasted_iota(jnp.int32, (R, R), 0)
    cc = lax.broadcasted_iota(jnp.int32, (R, R), 1)
    return ((cc <= rr) & ((rr & -C) == (cc & -C))).astype(BF16)


def _ssd_kernel(*refs, has_init, NB, TL, C, n_side):
    it = iter(refs)
    x_ref = next(it)
    s0_ref = cv0_ref = None
    if has_init:
        s0_ref = next(it)
        cv0_ref = next(it)
    (nw_ref, win_ref, cw_ref, cb_ref, dtb_ref, dtbT_ref,
     alog_ref, alogp_ref, dexp_ref, gnw_ref, wo_ref) = [next(it) for _ in range(11)]
    side_in = [next(it) for _ in range(n_side)]
    xo_ref = next(it)
    so_ref = next(it)
    cvo_ref = next(it)
    for w_ref in side_in:
        next(it)[...] = w_ref[...].astype(BF16)
    (st_scr, st16_scr, hist_scr, z_scr, xbc_scr, y_scr, e_scr, tri_scr,
     colc_scr, mix_scr, xw16_scr, xsl16_scr, xsr16_scr, bc16_scr) = [next(it) for _ in range(14)]
    es_scr = colcs_scr = None
    if 2 * C != LANES:
        es_scr = next(it)
        colcs_scr = next(it)

    R = NB * TL
    n_ch = R // C
    ch_per_seq = TL // C
    W2 = 2 * C
    i = pl.program_id(1)
    n_i = pl.num_programs(1)

    @pl.when(i == 0)
    def _init():
        ek = lax.broadcasted_iota(jnp.int32, e_scr.shape, 0)
        el = lax.broadcasted_iota(jnp.int32, e_scr.shape, 1)
        e_scr[...] = ((ek < 3 * SSD_N_HEADS) & ((ek & (SSD_N_HEADS - 1)) == el // SSD_HEAD_DIM)).astype(BF16)
        if es_scr is not None:
            ek = lax.broadcasted_iota(jnp.int32, es_scr.shape, 0)
            el = lax.broadcasted_iota(jnp.int32, es_scr.shape, 1)
            es_scr[...] = ((ek < 3 * SSD_N_HEADS) & ((ek & (SSD_N_HEADS - 1)) == el // C)).astype(BF16)
        tri_scr[...] = _chunk_triangle(R, C)
        hist_scr[...] = jnp.zeros_like(hist_scr)
        if has_init:
            for nb in range(NB):
                for g in range(SSD_N_GROUPS):
                    st_scr[nb, g] = s0_ref[nb, g * SSD_NORM_GROUP:(g + 1) * SSD_NORM_GROUP, :].T
                for k in range(SSD_CONV_DIM // CONV_COLS):
                    hist_scr[nb, k, SUBLANES - 3:SUBLANES, :] = cv0_ref[nb, :, k * CONV_COLS:(k + 1) * CONV_COLS]
        else:
            st_scr[...] = jnp.zeros_like(st_scr)
        st16_scr[...] = st_scr[...].astype(BF16)

    x = x_ref[...].reshape(R, D_MODEL)
    h16 = _rms(x, nw_ref[...]).astype(BF16)

    for cb0 in range(0, SSD_CONV_DIM, CONV_COLS):
        cols = slice(cb0, cb0 + CONV_COLS)
        xr = _dot(h16, win_ref[:, SSD_XBC_OFF + cb0:SSD_XBC_OFF + cb0 + CONV_COLS])
        w0, w1, w2, w3 = (cw_ref[k:k + 1, cols] for k in range(SSD_CONV_W))
        for nb in range(NB):
            xn = xr[nb * TL:(nb + 1) * TL]
            hm = hist_scr[nb, cb0 // CONV_COLS]
            sx = _shift_rows(xn, hm[7:8])
            a2 = _shift_rows(w1 * xn + w0 * sx, w1 * hm[6:8] + w0 * hm[5:7])
            xbc_scr[nb * TL:(nb + 1) * TL, cols] = _silu(w3 * xn + w2 * sx + a2 + cb_ref[:, cols])
            hist_scr[nb, cb0 // CONV_COLS] = xn[TL - SUBLANES:TL]
    for zb0 in range(0, SSD_D_INNER, CONV_COLS):
        z_scr[:, zb0:zb0 + CONV_COLS] = _silu(_dot(h16, win_ref[:, zb0:zb0 + CONV_COLS]))

    dt_raw = _dot(h16, win_ref[:, SSD_DT_OFF:SSD_DT_OFF + SSD_N_HEADS])
    dt = jax.nn.softplus(dt_raw + dtb_ref[...])
    pi = lax.broadcasted_iota(jnp.int32, (SSD_N_HEADS, SSD_N_HEADS), 0)
    pj = lax.broadcasted_iota(jnp.int32, (SSD_N_HEADS, SSD_N_HEADS), 1)
    pick = (pj == jnp.where(pi < SSD_N_PAIRS, 2 * pi, 2 * (pi - SSD_N_PAIRS) + 1)).astype(BF16)
    dtT = jax.nn.softplus(sum(_dot_nt(pick, p) for p in _split3(dt_raw)) + dtbT_ref[...])
    dtTp = jnp.concatenate(
        [jnp.concatenate([dtT[0:SSD_N_PAIRS, c * C:(c + 1) * C],
                          dtT[SSD_N_PAIRS:, c * C:(c + 1) * C]], axis=1) for c in range(n_ch)],
        axis=0)
    a_row = -jnp.exp(alog_ref[...])
    a_pair = -jnp.exp(alogp_ref[...])
    cum = sum(_dot(tri_scr[...], p) for p in _split3(dt * a_row))
    last = jnp.concatenate(
        [jnp.broadcast_to(cum[(c + 1) * C - 1:(c + 1) * C, :], (C, SSD_N_HEADS)) for c in range(n_ch)], axis=0)
    wdt = jnp.exp(last - cum) * dt
    cum3 = _stack3(cum)
    colc_scr[...] = _dot(cum3, e_scr[...])
    xs = xbc_scr[:, :SSD_D_INNER]
    xw16_scr[...] = (xs * _dot(_stack3(wdt), e_scr[...])).astype(BF16)
    head_left = (lax.broadcasted_iota(jnp.int32, (R, SSD_D_INNER), 1) & (LANES - 1)) < SSD_HEAD_DIM
    xsl16_scr[...] = jnp.where(head_left, xs, 0.0).astype(BF16)
    xsr16_scr[...] = jnp.where(head_left, 0.0, xs).astype(BF16)
    bc16_scr[...] = xbc_scr[:, SSD_D_INNER:].astype(BF16)
    if colcs_scr is not None:
        colcs_scr[...] = _dot(cum3, es_scr[...])
    r2 = lax.broadcasted_iota(jnp.int32, (W2, W2), 0)
    c2 = lax.broadcasted_iota(jnp.int32, (W2, W2), 1)
    triT2_16 = ((r2 <= c2) & ((r2 < C) == (c2 < C))).astype(BF16)
    aTp = dtTp * jnp.concatenate([a_pair] * n_ch, axis=0)
    cumTp = sum(_dot(p, triT2_16) for p in _split3(aTp))

    tp = lax.broadcasted_iota(jnp.int32, (C, W2), 0)
    sp = lax.broadcasted_iota(jnp.int32, (C, W2), 1)
    tril_pair = jnp.where(sp < C, sp, sp - C) <= tp
    colc_l_scr = colc_scr if colcs_scr is None else colcs_scr
    for c in range(n_ch):
        rows = slice(c * C, (c + 1) * C)
        for g in range(SSD_N_GROUPS):
            b16 = bc16_scr[rows, g * SSD_D_STATE:(g + 1) * SSD_D_STATE]
            c16 = bc16_scr[rows, (SSD_N_GROUPS + g) * SSD_D_STATE:(SSD_N_GROUPS + g + 1) * SSD_D_STATE]
            cbm = jnp.where(tril_pair, _dot_nt(c16, jnp.concatenate([b16, b16], axis=0)), 0.0)
            for q in range(4):
                jp = g * 4 + q
                k = c * SSD_N_PAIRS + jp
                lanes_s = slice(jp * W2, (jp + 1) * W2)
                decay = jnp.exp(jnp.minimum(colc_l_scr[rows, lanes_s] - cumTp[k:k + 1, :], 0.0))
                mix_scr[rows, lanes_s] = (cbm * decay * dtTp[k:k + 1, :]).astype(mix_scr.dtype)
    colc_scr[...] = jnp.exp(colc_scr[...])

    for c in range(n_ch):
        nb = c // ch_per_seq
        rows = slice(c * C, (c + 1) * C)
        for g in range(SSD_N_GROUPS):
            gl = slice(g * SSD_NORM_GROUP, (g + 1) * SSD_NORM_GROUP)
            b16 = bc16_scr[rows, g * SSD_D_STATE:(g + 1) * SSD_D_STATE]
            c16 = bc16_scr[rows, (SSD_N_GROUPS + g) * SSD_D_STATE:(SSD_N_GROUPS + g + 1) * SSD_D_STATE]
            y_inter = _dot(c16, st16_scr[nb, g])
            for q in range(4):
                jp = g * 4 + q
                lanes = slice(jp * LANES, (jp + 1) * LANES)
                mix16 = mix_scr[rows, jp * W2:(jp + 1) * W2].astype(BF16)
                rhs = jnp.concatenate([xsl16_scr[rows, lanes], xsr16_scr[rows, lanes]], axis=0)
                y_scr[rows, lanes] = _dot(mix16, rhs) + colc_scr[rows, lanes] * y_inter[:, q * LANES:(q + 1) * LANES]
            el_g = colc_scr[(c + 1) * C - 1:(c + 1) * C, gl]
            st_new = el_g * st_scr[nb, g] + _dot_tn(b16, xw16_scr[rows, gl])
            st_scr[nb, g] = st_new
            st16_scr[nb, g] = st_new.astype(BF16)

    out = None
    for g in range(SSD_N_GROUPS):
        cols = slice(g * SSD_NORM_GROUP, (g + 1) * SSD_NORM_GROUP)
        yg = (y_scr[:, cols] + dexp_ref[:, cols] * xbc_scr[:, cols]) * z_scr[:, cols]
        yn = yg * lax.rsqrt(jnp.mean(yg * yg, axis=-1, keepdims=True) + NORM_EPS) * gnw_ref[:, cols]
        part = _dot(yn.astype(BF16), wo_ref[cols, :])
        out = part if out is None else out + part
    xo_ref[...] = (x + out).reshape(NB, TL, D_MODEL)

    @pl.when(i == n_i - 1)
    def _fin():
        for nb in range(NB):
            for g in range(SSD_N_GROUPS):
                so_ref[nb, g * SSD_NORM_GROUP:(g + 1) * SSD_NORM_GROUP, :] = st_scr[nb, g].T
            for k in range(SSD_CONV_DIM // CONV_COLS):
                cvo_ref[nb, :, k * CONV_COLS:(k + 1) * CONV_COLS] = hist_scr[nb, k, SUBLANES - 3:SUBLANES, :]


def _const_spec(shape):
    nd = len(shape)
    return pl.BlockSpec(shape, lambda *_: (0,) * nd, pipeline_mode=pl.Buffered(1))


def _layer_spec(arr, layer):
    nd = arr.ndim - 1
    return pl.BlockSpec((None,) + arr.shape[1:], lambda *_: (layer,) + (0,) * nd,
                        pipeline_mode=pl.Buffered(1))


def _side_cast(side, n_steps, step_of):
    in_specs, out_specs, out_shapes = [], [], []
    for w, layer in side:
        rows, cols = w.shape[1:]
        n_blk = max(n for n in range(1, n_steps + 1) if rows % n == 0 and (rows // n) % BF16_TILE_ROWS == 0)
        br = rows // n_blk
        in_specs.append(pl.BlockSpec(
            (None, br, cols), lambda *g, layer=layer, n_blk=n_blk: (layer, jnp.minimum(step_of(*g), n_blk - 1), 0)))
        out_specs.append(pl.BlockSpec(
            (br, cols), lambda *g, n_blk=n_blk: (jnp.minimum(step_of(*g), n_blk - 1), 0)))
        out_shapes.append(jax.ShapeDtypeStruct((rows, cols), BF16))
    return in_specs, out_specs, out_shapes


def _ssd_mixer(x, s0, cv0, p, side, *, NB, TL, C):
    n_seq, L, _ = x.shape
    has_init = s0 is not None
    R = NB * TL
    grid = (n_seq // NB, L // TL)
    side_in, side_out, side_shapes = _side_cast(side, grid[0] * grid[1], lambda b, i: b * grid[1] + i)
    in_specs = [pl.BlockSpec((NB, TL, D_MODEL), lambda b, i: (b, i, 0))]
    args = [x]
    if has_init:
        in_specs += [pl.BlockSpec((NB, SSD_D_INNER, SSD_D_STATE), lambda b, i: (b, 0, 0)),
                     pl.BlockSpec((NB, 3, SSD_CONV_DIM), lambda b, i: (b, 0, 0))]
        args += [s0, cv0]
    alog_pair = jnp.concatenate(
        [jnp.broadcast_to(p["alog"][0::2, None], (SSD_N_PAIRS, C)),
         jnp.broadcast_to(p["alog"][1::2, None], (SSD_N_PAIRS, C))], axis=1)
    consts = [p["nw"], p["win"], p["cw"], p["cb"], p["dtb"], p["dtbT"],
              p["alog"].reshape(1, SSD_N_HEADS), alog_pair, p["dexp"], p["gnw"], p["wo"]]
    for c in consts:
        in_specs.append(_layer_spec(c, 0) if c.ndim == 3 else _const_spec(c.shape))
    args += consts
    in_specs += side_in
    args += [w for w, _ in side]
    out_shape = [jax.ShapeDtypeStruct((n_seq, L, D_MODEL), F32),
                 jax.ShapeDtypeStruct((n_seq, SSD_D_INNER, SSD_D_STATE), F32),
                 jax.ShapeDtypeStruct((n_seq, 3, SSD_CONV_DIM), F32)] + side_shapes
    out_specs = [pl.BlockSpec((NB, TL, D_MODEL), lambda b, i: (b, i, 0)),
                 pl.BlockSpec((NB, SSD_D_INNER, SSD_D_STATE), lambda b, i: (b, 0, 0)),
                 pl.BlockSpec((NB, 3, SSD_CONV_DIM), lambda b, i: (b, 0, 0))] + side_out
    scratch = [pltpu.VMEM((NB, SSD_N_GROUPS, SSD_D_STATE, SSD_NORM_GROUP), F32),
               pltpu.VMEM((NB, SSD_N_GROUPS, SSD_D_STATE, SSD_NORM_GROUP), BF16),
               pltpu.VMEM((NB, SSD_CONV_DIM // CONV_COLS, SUBLANES, CONV_COLS), F32),
               pltpu.VMEM((R, SSD_D_INNER), F32),
               pltpu.VMEM((R, SSD_CONV_DIM), F32),
               pltpu.VMEM((R, SSD_D_INNER), F32),
               pltpu.VMEM((LANES, SSD_D_INNER), BF16),
               pltpu.VMEM((R, R), BF16),
               pltpu.VMEM((R, SSD_D_INNER), F32),
               pltpu.VMEM((R, SSD_N_PAIRS * 2 * C), BF16 if 2 * C == LANES else F32),
               pltpu.VMEM((R, SSD_D_INNER), BF16),
               pltpu.VMEM((R, SSD_D_INNER), BF16),
               pltpu.VMEM((R, SSD_D_INNER), BF16),
               pltpu.VMEM((R, 2 * SSD_N_GROUPS * SSD_D_STATE), BF16)]
    if 2 * C != LANES:
        scratch += [pltpu.VMEM((LANES, SSD_N_PAIRS * 2 * C), BF16),
                    pltpu.VMEM((R, SSD_N_PAIRS * 2 * C), F32)]
    outs = pl.pallas_call(
        functools.partial(_ssd_kernel, has_init=has_init, NB=NB, TL=TL, C=C, n_side=len(side)),
        grid=grid, in_specs=in_specs, out_specs=out_specs, out_shape=out_shape,
        scratch_shapes=scratch,
        compiler_params=pltpu.CompilerParams(
            dimension_semantics=("arbitrary", "arbitrary"), vmem_limit_bytes=VMEM_LIMIT),
        name="ssd_mixer_init" if has_init else "ssd_mixer_zero",
    )(*args)
    return outs[0], outs[1], outs[2], list(outs[3:])


def _hgrn_kernel(*refs, has_init, NB, TL, C):
    it = iter(refs)
    x_ref = next(it)
    s0_ref = next(it) if has_init else None
    nw_ref, win_ref, lb_ref, gnw_ref, wo_ref = [next(it) for _ in range(5)]
    xo_ref = next(it)
    so_ref = next(it)
    (st_scr, tri_scr, qe_scr, qm_scr, ke_scr, kw_scr, v_scr, g_scr, elT_scr, sc_scr,
     o_scr) = [next(it) for _ in range(11)]
    assert C <= HGRN_HEAD_DIM
    TB = tri_scr.shape[0]

    R = NB * TL
    n_ch = R // C
    ch_per_seq = TL // C
    HD = HGRN_HEAD_DIM
    i = pl.program_id(1)
    n_i = pl.num_programs(1)

    @pl.when(i == 0)
    def _init():
        tri_scr[...] = _chunk_triangle(TB, C)
        if has_init:
            st_scr[...] = s0_ref[...]
        else:
            st_scr[...] = jnp.zeros_like(st_scr)

    x = x_ref[...].reshape(R, D_MODEL)
    h16 = _rms(x, nw_ref[...]).astype(BF16)
    lb_soft = jax.nn.softmax(lb_ref[...], axis=0)
    lb = (lb_soft[0:1, :] + lb_soft[1:2, :]) - lb_soft[0:1, :]

    for c0 in range(0, D_MODEL, HGRN_COLS):
        cols = slice(c0, c0 + HGRN_COLS)
        q = _silu(_dot(h16, win_ref[:, c0:c0 + HGRN_COLS]))
        f = _dot(h16, win_ref[:, D_MODEL + c0:D_MODEL + c0 + HGRN_COLS])
        v = _dot(h16, win_ref[:, 2 * D_MODEL + c0:2 * D_MODEL + c0 + HGRN_COLS])
        g = _dot(h16, win_ref[:, 3 * D_MODEL + c0:3 * D_MODEL + c0 + HGRN_COLS])
        lbc = lb[:, cols]
        forget = lbc + (1.0 - lbc) * jax.nn.sigmoid(f)
        k = 1.0 - forget
        logf = _split3(jnp.log(forget))
        b = jnp.concatenate([sum(_dot(tri_scr[...], p[r0:r0 + TB]) for p in logf) for r0 in range(0, R, TB)], axis=0)
        ends = [b[(c + 1) * C - 1:(c + 1) * C, :] for c in range(n_ch)]
        last = jnp.concatenate([jnp.broadcast_to(e, (C, HGRN_COLS)) for e in ends], axis=0)
        mid = jnp.concatenate([jnp.broadcast_to(b[c * C + C // 2:c * C + C // 2 + 1, :], (C, HGRN_COLS))
                               for c in range(n_ch)], axis=0)
        qe_scr[:, cols] = (q * jnp.exp(b)).astype(BF16)
        qm_scr[:, cols] = (q * jnp.exp(b - mid)).astype(BF16)
        ke_scr[:, cols] = (k * jnp.exp(mid - b)).astype(BF16)
        kw_scr[:, cols] = (k * jnp.exp(last - b)).astype(BF16)
        v_scr[:, cols] = v.astype(BF16)
        g_scr[:, cols] = _silu(g)
        el = jnp.exp(jnp.concatenate(ends + [jnp.zeros((LANES - n_ch, HGRN_COLS), F32)], axis=0))
        elT_scr[cols, :] = el.T

    rr = lax.broadcasted_iota(jnp.int32, (C, C), 0)
    cc = lax.broadcasted_iota(jnp.int32, (C, C), 1)
    tril = cc <= rr

    for c in range(n_ch):
        rows = slice(c * C, (c + 1) * C)
        for hh in range(HGRN_N_HEADS):
            sl = slice(hh * HD, (hh + 1) * HD)
            sc = jnp.where(tril, _dot_nt(qm_scr[rows, sl], ke_scr[rows, sl]), 0.0)
            sc_scr[rows, hh * HD:hh * HD + C] = sc.astype(BF16)

    for c in range(n_ch):
        nb = c // ch_per_seq
        rows = slice(c * C, (c + 1) * C)
        for hh in range(HGRN_N_HEADS):
            sl = slice(hh * HD, (hh + 1) * HD)
            v = v_scr[rows, sl]
            s_h = st_scr[nb, hh]
            o_scr[rows, sl] = _dot(jnp.concatenate([qe_scr[rows, sl], sc_scr[rows, hh * HD:hh * HD + C]], axis=1),
                                   jnp.concatenate([s_h.astype(BF16), v], axis=0))
            st_scr[nb, hh] = elT_scr[sl, c:c + 1] * s_h + _dot_tn(kw_scr[rows, sl], v)

    out = None
    for c0 in range(0, D_MODEL, HGRN_COLS):
        parts = []
        for h0 in range(c0, c0 + HGRN_COLS, HD):
            blk = o_scr[:, h0:h0 + HD]
            parts.append(blk * lax.rsqrt(jnp.mean(blk * blk, axis=-1, keepdims=True) + NORM_EPS))
        cols = slice(c0, c0 + HGRN_COLS)
        on = jnp.concatenate(parts, axis=1) * gnw_ref[:, cols] * g_scr[:, cols]
        part = _dot(on.astype(BF16), wo_ref[cols, :])
        out = part if out is None else out + part
    xo_ref[...] = (x + out).reshape(NB, TL, D_MODEL)

    @pl.when(i == n_i - 1)
    def _fin():
        so_ref[...] = st_scr[...]


def _hgrn_mixer(x, s0, p, *, NB, TL, C):
    n_seq, L, _ = x.shape
    has_init = s0 is not None
    R = NB * TL
    grid = (n_seq // NB, L // TL)
    st_block = (NB, HGRN_N_HEADS, HGRN_HEAD_DIM, HGRN_HEAD_DIM)
    in_specs = [pl.BlockSpec((NB, TL, D_MODEL), lambda b, i: (b, i, 0))]
    args = [x]
    if has_init:
        in_specs.append(pl.BlockSpec(st_block, lambda b, i: (b, 0, 0, 0)))
        args.append(s0)
    consts = [p["nw"], p["win"], p["lb"], p["gnw"], p["wo"]]
    for c in consts:
        in_specs.append(_layer_spec(c, 0) if c.ndim == 3 else _const_spec(c.shape))
    args += consts
    out_shape = (jax.ShapeDtypeStruct((n_seq, L, D_MODEL), F32),
                 jax.ShapeDtypeStruct((n_seq,) + st_block[1:], F32))
    out_specs = (pl.BlockSpec((NB, TL, D_MODEL), lambda b, i: (b, i, 0)),
                 pl.BlockSpec(st_block, lambda b, i: (b, 0, 0, 0)))
    scratch = [pltpu.VMEM(st_block, F32),
               pltpu.VMEM((min(R, MXU_DEPTH),) * 2, BF16),
               pltpu.VMEM((R, D_MODEL), BF16),
               pltpu.VMEM((R, D_MODEL), BF16),
               pltpu.VMEM((R, D_MODEL), BF16),
               pltpu.VMEM((R, D_MODEL), BF16),
               pltpu.VMEM((R, D_MODEL), BF16),
               pltpu.VMEM((R, D_MODEL), F32),
               pltpu.VMEM((D_MODEL, LANES), F32),
               pltpu.VMEM((R, D_MODEL), BF16),
               pltpu.VMEM((R, D_MODEL), F32)]
    return pl.pallas_call(
        functools.partial(_hgrn_kernel, has_init=has_init, NB=NB, TL=TL, C=C),
        grid=grid, in_specs=in_specs, out_specs=out_specs, out_shape=out_shape,
        scratch_shapes=scratch,
        compiler_params=pltpu.CompilerParams(
            dimension_semantics=("arbitrary", "arbitrary"), vmem_limit_bytes=VMEM_LIMIT),
        name="hgrn_mixer_init" if has_init else "hgrn_mixer_zero",
    )(*args)


def _ffn_kernel(*refs, final, n_side):
    it = iter(refs)
    x_ref, nw_ref, wg_ref, wu_ref, wd_ref = [next(it) for _ in range(5)]
    fw_ref = next(it) if final else None
    side_in = [next(it) for _ in range(n_side)]
    o_ref = next(it)
    for w_ref in side_in:
        next(it)[...] = w_ref[...].astype(BF16)
    x = x_ref[...]
    h16 = _rms(x, nw_ref[...]).astype(BF16)
    act = _silu(_dot(h16, wg_ref[...])) * _dot(h16, wu_ref[...])
    y = x + _dot(act.astype(BF16), wd_ref[...])
    if final:
        y = _rms(y, fw_ref[...])
    o_ref[...] = y


def _ffn(x2d, nw, wg, wu, wd, final_w, side, *, TM):
    rows = x2d.shape[0]
    final = final_w is not None
    n_steps = rows // TM
    side_in, side_out, side_shapes = _side_cast(side, n_steps, lambda i: i)
    consts = [nw, wg, wu, wd] + ([final_w] if final else [])
    in_specs = [pl.BlockSpec((TM, D_MODEL), lambda i: (i, 0))] + [_const_spec(c.shape) for c in consts] + side_in
    outs = pl.pallas_call(
        functools.partial(_ffn_kernel, final=final, n_side=len(side)),
        grid=(n_steps,),
        in_specs=in_specs,
        out_specs=[pl.BlockSpec((TM, D_MODEL), lambda i: (i, 0))] + side_out,
        out_shape=[jax.ShapeDtypeStruct((rows, D_MODEL), F32)] + side_shapes,
        compiler_params=pltpu.CompilerParams(
            dimension_semantics=("arbitrary",), vmem_limit_bytes=VMEM_LIMIT),
        name="swiglu_final" if final else "swiglu",
    )(x2d, *consts, *[w for w, _ in side])
    return outs[0], list(outs[1:])


def _tiles(n_seq, L):
    if L >= SSD_LONG_TILE:
        return dict(NB=1, TL_SSD=SSD_LONG_TILE, TL_HGRN=HGRN_LONG_TILE, C=SCAN_CHUNK, C_HGRN=HGRN_CHUNK,
                    TM=FFN_ROWS)
    return dict(NB=SHORT_SEQ_PER_STEP, TL_SSD=L, TL_HGRN=L, C=min(L, SCAN_CHUNK), C_HGRN=min(L, HGRN_CHUNK),
                TM=min(FFN_ROWS, n_seq * L))


def _trunk(x, s_ssd, cv, s_hgrn, ssd_p, hgrn_p, ffn_nw, final_w, w16, raw):
    n_seq, L, _ = x.shape
    t = _tiles(n_seq, L)
    s0 = None if s_ssd is None else s_ssd.reshape(n_seq, SSD_D_INNER, SSD_D_STATE)

    def missing(names):
        return [k for k in names if k not in w16]

    need = missing(("wg0", "wu0", "wd0"))
    x, s_new, cv_new, cast = _ssd_mixer(x, s0, cv, ssd_p, [raw[k] for k in need],
                                        NB=t["NB"], TL=t["TL_SSD"], C=t["C"])
    w16.update(zip(need, cast))
    need = missing(("hgrn_in", "hgrn_out", "wg1", "wu1", "wd1"))
    x, cast = _ffn(x.reshape(n_seq * L, D_MODEL), ffn_nw[0], w16["wg0"], w16["wu0"], w16["wd0"], None,
                   [raw[k] for k in need], TM=t["TM"])
    w16.update(zip(need, cast))
    x, h_new = _hgrn_mixer(x.reshape(n_seq, L, D_MODEL), s_hgrn,
                           dict(hgrn_p, win=w16["hgrn_in"], wo=w16["hgrn_out"]),
                           NB=t["NB"], TL=t["TL_HGRN"], C=t["C_HGRN"])
    y, _ = _ffn(x.reshape(n_seq * L, D_MODEL), ffn_nw[1], w16["wg1"], w16["wu1"], w16["wd1"], final_w,
                [], TM=t["TM"])
    s_new = s_new.reshape(1, n_seq, SSD_N_HEADS, SSD_HEAD_DIM, SSD_D_STATE)
    return y.reshape(n_seq, L, D_MODEL), s_new, cv_new[None], h_new[None]


def kernel(x_prompt, x_sample, state_ssd, cache_conv, state_hgrn, ssd_norm_w, ssd_in_w, ssd_conv_w, ssd_conv_b, ssd_dt_bias, ssd_A_log, ssd_D, ssd_gnorm_w, ssd_out_w, hgrn_norm_w, hgrn_in_w, hgrn_lower_bounds, hgrn_gnorm_w, hgrn_out_w, ffn_norm_w, ffn_w_gate, ffn_w_up, ffn_w_down, final_norm_w):
    pair_order = jnp.concatenate([jnp.arange(0, SSD_N_HEADS, 2), jnp.arange(1, SSD_N_HEADS, 2)])
    ssd_p = {
        "nw": ssd_norm_w[0].reshape(1, D_MODEL),
        "win": jnp.pad(ssd_in_w[0].astype(BF16), ((0, 0), (0, -SSD_IN_COLS % LANES))),
        "cw": ssd_conv_w,
        "cb": ssd_conv_b[0].reshape(1, SSD_CONV_DIM),
        "dtb": ssd_dt_bias[0].reshape(1, SSD_N_HEADS),
        "dtbT": ssd_dt_bias[0][pair_order].reshape(SSD_N_HEADS, 1),
        "alog": ssd_A_log[0],
        "dexp": jnp.repeat(ssd_D[0], SSD_HEAD_DIM).reshape(1, SSD_D_INNER),
        "gnw": ssd_gnorm_w[0].reshape(1, SSD_D_INNER),
        "wo": ssd_out_w.astype(BF16),
    }
    hgrn_p = {
        "nw": hgrn_norm_w[0].reshape(1, D_MODEL),
        "lb": hgrn_lower_bounds,
        "gnw": jnp.tile(hgrn_gnorm_w[0], HGRN_N_HEADS).reshape(1, D_MODEL),
    }
    ffn_nw = ffn_norm_w.reshape(2, 1, D_MODEL)
    final_w = final_norm_w.reshape(1, D_MODEL)
    raw = {"wg0": (ffn_w_gate, 0), "wu0": (ffn_w_up, 0), "wd0": (ffn_w_down, 0),
           "hgrn_in": (hgrn_in_w, 0), "hgrn_out": (hgrn_out_w, 0),
           "wg1": (ffn_w_gate, 1), "wu1": (ffn_w_up, 1), "wd1": (ffn_w_down, 1)}
    w16 = {}

    y_p, ssd_s_p, conv_p, hgrn_s_p = _trunk(
        x_prompt, None, None, None, ssd_p, hgrn_p, ffn_nw, final_w, w16, raw)
    y_s, ssd_s_s, conv_s, hgrn_s_s = _trunk(
        x_sample, state_ssd[0], cache_conv[0], state_hgrn[0], ssd_p, hgrn_p, ffn_nw, final_w, w16, raw)
    return (y_p, y_s, ssd_s_p, conv_p, hgrn_s_p, ssd_s_s, conv_s, hgrn_s_s)
```

```python
import functools

import jax
import jax.numpy as jnp
from jax import lax
from jax.experimental import pallas as pl
from jax.experimental.pallas import tpu as pltpu

F32 = jnp.float32
BF16 = jnp.bfloat16

D_MODEL = 1024
NORM_EPS = 1e-6

SSD_D_INNER = 2048
SSD_HEAD_DIM = 64
SSD_N_HEADS = 32
SSD_N_GROUPS = 4
SSD_D_STATE = 128
SSD_CONV_W = 4
SSD_CONV_DIM = 3072
SSD_NORM_GROUP = 512
SSD_N_PAIRS = SSD_N_HEADS // 2
SSD_XBC_OFF = SSD_D_INNER
SSD_DT_OFF = SSD_D_INNER + SSD_CONV_DIM

HGRN_HEAD_DIM = 128
HGRN_N_HEADS = 8

FFN_HIDDEN = 2816

LANES = 128
SUBLANES = 8
BF16_TILE_ROWS = 16
MXU_DEPTH = 256
SCAN_CHUNK = 64
HGRN_CHUNK = 128
SSD_LONG_TILE = 256
HGRN_LONG_TILE = 1024
SHORT_SEQ_PER_STEP = 4
FFN_ROWS = 512
CONV_COLS = 512
HGRN_COLS = 256
VMEM_LIMIT = 56 * 1024 * 1024


def _dot(a, b):
    return jnp.dot(a, b, preferred_element_type=F32)


def _dot_nt(a, b):
    return lax.dot_general(a, b, (((1,), (1,)), ((), ())), preferred_element_type=F32)


def _dot_tn(a, b):
    return lax.dot_general(a, b, (((0,), (0,)), ((), ())), preferred_element_type=F32)


def _split3(a):
    hi = a.astype(BF16)
    r1 = a - hi.astype(F32)
    mid = r1.astype(BF16)
    lo = (r1 - mid.astype(F32)).astype(BF16)
    return hi, mid, lo


def _stack3(a):
    hi = a.astype(BF16).astype(F32)
    r1 = a - hi
    mid = r1.astype(BF16).astype(F32)
    return jnp.concatenate([hi, mid, r1 - mid, jnp.zeros_like(a)], axis=1).astype(BF16)


def _rms(x, w):
    return x * lax.rsqrt(jnp.mean(x * x, axis=-1, keepdims=True) + NORM_EPS) * w


def _silu(x):
    return x * jax.nn.sigmoid(x)


def _shift_rows(u, first_rows):
    rows, cols = u.shape
    k = first_rows.shape[0]
    nv = rows // SUBLANES
    r = pltpu.roll(u.reshape(nv, SUBLANES, cols), k, axis=1)
    first = jnp.concatenate([first_rows, jnp.zeros((SUBLANES - k, cols), u.dtype)], axis=0)
    prev = jnp.concatenate([first[None], r[:nv - 1]], axis=0)
    sub = lax.broadcasted_iota(jnp.int32, (nv, SUBLANES, cols), 1)
    return jnp.where(sub < k, prev, r).reshape(rows, cols)


def _chunk_triangle(R, C):
    rr = lax.broadcasted_iota(jnp.int32, (R, R), 0)
    cc = lax.broadcasted_iota(jnp.int32, (R, R), 1)
    return ((cc <= rr) & ((rr & -C) == (cc & -C))).astype(BF16)


def _ssd_kernel(*refs, has_init, NB, TL, C, n_side):
    it = iter(refs)
    x_ref = next(it)
    s0_ref = cv0_ref = None
    if has_init:
        s0_ref = next(it)
        cv0_ref = next(it)
    (nw_ref, win_ref, cw_ref, cb_ref, dtb_ref, dtbT_ref,
     alog_ref, alogp_ref, dexp_ref, gnw_ref, wo_ref) = [next(it) for _ in range(11)]
    side_in = [next(it) for _ in range(n_side)]
    xo_ref = next(it)
    so_ref = next(it)
    cvo_ref = next(it)
    for w_ref in side_in:
        next(it)[...] = w_ref[...].astype(BF16)
    (st_scr, st16_scr, hist_scr, z_scr, xbc_scr, y_scr, e_scr, tri_scr,
     colc_scr, mix_scr, xw16_scr, xsl16_scr, xsr16_scr, bc16_scr) = [next(it) for _ in range(14)]
    es_scr = colcs_scr = None
    if 2 * C != LANES:
        es_scr = next(it)
        colcs_scr = next(it)

    R = NB * TL
    n_ch = R // C
    ch_per_seq = TL // C
    W2 = 2 * C
    i = pl.program_id(1)
    n_i = pl.num_programs(1)

    @pl.when(i == 0)
    def _init():
        ek = lax.broadcasted_iota(jnp.int32, e_scr.shape, 0)
        el = lax.broadcasted_iota(jnp.int32, e_scr.shape, 1)
        e_scr[...] = ((ek < 3 * SSD_N_HEADS) & ((ek & (SSD_N_HEADS - 1)) == el // SSD_HEAD_DIM)).astype(BF16)
        if es_scr is not None:
            ek = lax.broadcasted_iota(jnp.int32, es_scr.shape, 0)
            el = lax.broadcasted_iota(jnp.int32, es_scr.shape, 1)
            es_scr[...] = ((ek < 3 * SSD_N_HEADS) & ((ek & (SSD_N_HEADS - 1)) == el // C)).astype(BF16)
        tri_scr[...] = _chunk_triangle(R, C)
        hist_scr[...] = jnp.zeros_like(hist_scr)
        if has_init:
            for nb in range(NB):
                for g in range(SSD_N_GROUPS):
                    st_scr[nb, g] = s0_ref[nb, g * SSD_NORM_GROUP:(g + 1) * SSD_NORM_GROUP, :].T
                for k in range(SSD_CONV_DIM // CONV_COLS):
                    hist_scr[nb, k, SUBLANES - 3:SUBLANES, :] = cv0_ref[nb, :, k * CONV_COLS:(k + 1) * CONV_COLS]
        else:
            st_scr[...] = jnp.zeros_like(st_scr)
        st16_scr[...] = st_scr[...].astype(BF16)

    x = x_ref[...].reshape(R, D_MODEL)
    h16 = _rms(x, nw_ref[...]).astype(BF16)

    for cb0 in range(0, SSD_CONV_DIM, CONV_COLS):
        cols = slice(cb0, cb0 + CONV_COLS)
        xr = _dot(h16, win_ref[:, SSD_XBC_OFF + cb0:SSD_XBC_OFF + cb0 + CONV_COLS])
        w0, w1, w2, w3 = (cw_ref[k:k + 1, cols] for k in range(SSD_CONV_W))
        for nb in range(NB):
            xn = xr[nb * TL:(nb + 1) * TL]
            hm = hist_scr[nb, cb0 // CONV_COLS]
            sx = _shift_rows(xn, hm[7:8])
            a2 = _shift_rows(w1 * xn + w0 * sx, w1 * hm[6:8] + w0 * hm[5:7])
            xbc_scr[nb * TL:(nb + 1) * TL, cols] = _silu(w3 * xn + w2 * sx + a2 + cb_ref[:, cols])
            hist_scr[nb, cb0 // CONV_COLS] = xn[TL - SUBLANES:TL]
    for zb0 in range(0, SSD_D_INNER, CONV_COLS):
        z_scr[:, zb0:zb0 + CONV_COLS] = _silu(_dot(h16, win_ref[:, zb0:zb0 + CONV_COLS]))

    dt_raw = _dot(h16, win_ref[:, SSD_DT_OFF:])
    dt = jax.nn.softplus(dt_raw + dtb_ref[...])
    pi = lax.broadcasted_iota(jnp.int32, (SSD_N_HEADS, SSD_N_HEADS), 0)
    pj = lax.broadcasted_iota(jnp.int32, (SSD_N_HEADS, SSD_N_HEADS), 1)
    pick = (pj == jnp.where(pi < SSD_N_PAIRS, 2 * pi, 2 * (pi - SSD_N_PAIRS) + 1)).astype(BF16)
    dtT = jax.nn.softplus(sum(_dot_nt(pick, p) for p in _split3(dt_raw)) + dtbT_ref[...])
    dtTp = jnp.concatenate(
        [jnp.concatenate([dtT[0:SSD_N_PAIRS, c * C:(c + 1) * C],
                          dtT[SSD_N_PAIRS:, c * C:(c + 1) * C]], axis=1) for c in range(n_ch)],
        axis=0)
    a_row = -jnp.exp(alog_ref[...])
    a_pair = -jnp.exp(alogp_ref[...])
    cum = sum(_dot(tri_scr[...], p) for p in _split3(dt * a_row))
    last = jnp.concatenate(
        [jnp.broadcast_to(cum[(c + 1) * C - 1:(c + 1) * C, :], (C, SSD_N_HEADS)) for c in range(n_ch)], axis=0)
    wdt = jnp.exp(last - cum) * dt
    cum3 = _stack3(cum)
    colc_scr[...] = _dot(cum3, e_scr[...])
    xs = xbc_scr[:, :SSD_D_INNER]
    xw16_scr[...] = (xs * _dot(_stack3(wdt), e_scr[...])).astype(BF16)
    head_left = (lax.broadcasted_iota(jnp.int32, (R, SSD_D_INNER), 1) & (LANES - 1)) < SSD_HEAD_DIM
    xsl16_scr[...] = jnp.where(head_left, xs, 0.0).astype(BF16)
    xsr16_scr[...] = jnp.where(head_left, 0.0, xs).astype(BF16)
    bc16_scr[...] = xbc_scr[:, SSD_D_INNER:].astype(BF16)
    if colcs_scr is not None:
        colcs_scr[...] = _dot(cum3, es_scr[...])
    r2 = lax.broadcasted_iota(jnp.int32, (W2, W2), 0)
    c2 = lax.broadcasted_iota(jnp.int32, (W2, W2), 1)
    triT2_16 = ((r2 <= c2) & ((r2 < C) == (c2 < C))).astype(BF16)
    aTp = dtTp * jnp.concatenate([a_pair] * n_ch, axis=0)
    cumTp = sum(_dot(p, triT2_16) for p in _split3(aTp))

    tp = lax.broadcasted_iota(jnp.int32, (C, W2), 0)
    sp = lax.broadcasted_iota(jnp.int32, (C, W2), 1)
    tril_pair = jnp.where(sp < C, sp, sp - C) <= tp
    colc_l_scr = colc_scr if colcs_scr is None else colcs_scr
    for c in range(n_ch):
        rows = slice(c * C, (c + 1) * C)
        for g in range(SSD_N_GROUPS):
            b16 = bc16_scr[rows, g * SSD_D_STATE:(g + 1) * SSD_D_STATE]
            c16 = bc16_scr[rows, (SSD_N_GROUPS + g) * SSD_D_STATE:(SSD_N_GROUPS + g + 1) * SSD_D_STATE]
            cbm = jnp.where(tril_pair, _dot_nt(c16, jnp.concatenate([b16, b16], axis=0)), 0.0)
            for q in range(4):
                jp = g * 4 + q
                k = c * SSD_N_PAIRS + jp
                lanes_s = slice(jp * W2, (jp + 1) * W2)
                decay = jnp.exp(jnp.minimum(colc_l_scr[rows, lanes_s] - cumTp[k:k + 1, :], 0.0))
                mix_scr[rows, lanes_s] = (cbm * decay * dtTp[k:k + 1, :]).astype(mix_scr.dtype)
    colc_scr[...] = jnp.exp(colc_scr[...])

    for c in range(n_ch):
        nb = c // ch_per_seq
        rows = slice(c * C, (c + 1) * C)
        for g in range(SSD_N_GROUPS):
            gl = slice(g * SSD_NORM_GROUP, (g + 1) * SSD_NORM_GROUP)
            b16 = bc16_scr[rows, g * SSD_D_STATE:(g + 1) * SSD_D_STATE]
            c16 = bc16_scr[rows, (SSD_N_GROUPS + g) * SSD_D_STATE:(SSD_N_GROUPS + g + 1) * SSD_D_STATE]
            y_inter = _dot(c16, st16_scr[nb, g])
            for q in range(4):
                jp = g * 4 + q
                lanes = slice(jp * LANES, (jp + 1) * LANES)
                mix16 = mix_scr[rows, jp * W2:(jp + 1) * W2].astype(BF16)
                rhs = jnp.concatenate([xsl16_scr[rows, lanes], xsr16_scr[rows, lanes]], axis=0)
                y_scr[rows, lanes] = _dot(mix16, rhs) + colc_scr[rows, lanes] * y_inter[:, q * LANES:(q + 1) * LANES]
            el_g = colc_scr[(c + 1) * C - 1:(c + 1) * C, gl]
            st_new = el_g * st_scr[nb, g] + _dot_tn(b16, xw16_scr[rows, gl])
            st_scr[nb, g] = st_new
            st16_scr[nb, g] = st_new.astype(BF16)

    out = None
    for g in range(SSD_N_GROUPS):
        cols = slice(g * SSD_NORM_GROUP, (g + 1) * SSD_NORM_GROUP)
        yg = (y_scr[:, cols] + dexp_ref[:, cols] * xbc_scr[:, cols]) * z_scr[:, cols]
        yn = yg * lax.rsqrt(jnp.mean(yg * yg, axis=-1, keepdims=True) + NORM_EPS) * gnw_ref[:, cols]
        part = _dot(yn.astype(BF16), wo_ref[cols, :])
        out = part if out is None else out + part
    xo_ref[...] = (x + out).reshape(NB, TL, D_MODEL)

    @pl.when(i == n_i - 1)
    def _fin():
        for nb in range(NB):
            for g in range(SSD_N_GROUPS):
                so_ref[nb, g * SSD_NORM_GROUP:(g + 1) * SSD_NORM_GROUP, :] = st_scr[nb, g].T
            for k in range(SSD_CONV_DIM // CONV_COLS):
                cvo_ref[nb, :, k * CONV_COLS:(k + 1) * CONV_COLS] = hist_scr[nb, k, SUBLANES - 3:SUBLANES, :]


def _const_spec(shape):
    nd = len(shape)
    return pl.BlockSpec(shape, lambda *_: (0,) * nd, pipeline_mode=pl.Buffered(1))


def _layer_spec(arr, layer):
    nd = arr.ndim - 1
    return pl.BlockSpec((None,) + arr.shape[1:], lambda *_: (layer,) + (0,) * nd,
                        pipeline_mode=pl.Buffered(1))


def _side_cast(side, n_steps, step_of):
    in_specs, out_specs, out_shapes = [], [], []
    for w, layer in side:
        rows, cols = w.shape[1:]
        n_blk = max(n for n in range(1, n_steps + 1) if rows % n == 0 and (rows // n) % BF16_TILE_ROWS == 0)
        br = rows // n_blk
        in_specs.append(pl.BlockSpec(
            (None, br, cols), lambda *g, layer=layer, n_blk=n_blk: (layer, jnp.minimum(step_of(*g), n_blk - 1), 0)))
        out_specs.append(pl.BlockSpec(
            (br, cols), lambda *g, n_blk=n_blk: (jnp.minimum(step_of(*g), n_blk - 1), 0)))
        out_shapes.append(jax.ShapeDtypeStruct((rows, cols), BF16))
    return in_specs, out_specs, out_shapes


def _ssd_mixer(x, s0, cv0, p, side, *, NB, TL, C):
    n_seq, L, _ = x.shape
    has_init = s0 is not None
    R = NB * TL
    grid = (n_seq // NB, L // TL)
    side_in, side_out, side_shapes = _side_cast(side, grid[0] * grid[1], lambda b, i: b * grid[1] + i)
    in_specs = [pl.BlockSpec((NB, TL, D_MODEL), lambda b, i: (b, i, 0))]
    args = [x]
    if has_init:
        in_specs += [pl.BlockSpec((NB, SSD_D_INNER, SSD_D_STATE), lambda b, i: (b, 0, 0)),
                     pl.BlockSpec((NB, 3, SSD_CONV_DIM), lambda b, i: (b, 0, 0))]
        args += [s0, cv0]
    alog_pair = jnp.concatenate(
        [jnp.broadcast_to(p["alog"][0::2, None], (SSD_N_PAIRS, C)),
         jnp.broadcast_to(p["alog"][1::2, None], (SSD_N_PAIRS, C))], axis=1)
    consts = [p["nw"], p["win"], p["cw"], p["cb"], p["dtb"], p["dtbT"],
              p["alog"].reshape(1, SSD_N_HEADS), alog_pair, p["dexp"], p["gnw"], p["wo"]]
    for c in consts:
        in_specs.append(_layer_spec(c, 0) if c.ndim == 3 else _const_spec(c.shape))
    args += consts
    in_specs += side_in
    args += [w for w, _ in side]
    out_shape = [jax.ShapeDtypeStruct((n_seq, L, D_MODEL), F32),
                 jax.ShapeDtypeStruct((n_seq, SSD_D_INNER, SSD_D_STATE), F32),
                 jax.ShapeDtypeStruct((n_seq, 3, SSD_CONV_DIM), F32)] + side_shapes
    out_specs = [pl.BlockSpec((NB, TL, D_MODEL), lambda b, i: (b, i, 0)),
                 pl.BlockSpec((NB, SSD_D_INNER, SSD_D_STATE), lambda b, i: (b, 0, 0)),
                 pl.BlockSpec((NB, 3, SSD_CONV_DIM), lambda b, i: (b, 0, 0))] + side_out
    scratch = [pltpu.VMEM((NB, SSD_N_GROUPS, SSD_D_STATE, SSD_NORM_GROUP), F32),
               pltpu.VMEM((NB, SSD_N_GROUPS, SSD_D_STATE, SSD_NORM_GROUP), BF16),
               pltpu.VMEM((NB, SSD_CONV_DIM // CONV_COLS, SUBLANES, CONV_COLS), F32),
               pltpu.VMEM((R, SSD_D_INNER), F32),
               pltpu.VMEM((R, SSD_CONV_DIM), F32),
               pltpu.VMEM((R, SSD_D_INNER), F32),
               pltpu.VMEM((LANES, SSD_D_INNER), BF16),
               pltpu.VMEM((R, R), BF16),
               pltpu.VMEM((R, SSD_D_INNER), F32),
               pltpu.VMEM((R, SSD_N_PAIRS * 2 * C), BF16 if 2 * C == LANES else F32),
               pltpu.VMEM((R, SSD_D_INNER), BF16),
               pltpu.VMEM((R, SSD_D_INNER), BF16),
               pltpu.VMEM((R, SSD_D_INNER), BF16),
               pltpu.VMEM((R, 2 * SSD_N_GROUPS * SSD_D_STATE), BF16)]
    if 2 * C != LANES:
        scratch += [pltpu.VMEM((LANES, SSD_N_PAIRS * 2 * C), BF16),
                    pltpu.VMEM((R, SSD_N_PAIRS * 2 * C), F32)]
    outs = pl.pallas_call(
        functools.partial(_ssd_kernel, has_init=has_init, NB=NB, TL=TL, C=C, n_side=len(side)),
        grid=grid, in_specs=in_specs, out_specs=out_specs, out_shape=out_shape,
        scratch_shapes=scratch,
        compiler_params=pltpu.CompilerParams(
            dimension_semantics=("arbitrary", "arbitrary"), vmem_limit_bytes=VMEM_LIMIT),
        name="ssd_mixer_init" if has_init else "ssd_mixer_zero",
    )(*args)
    return outs[0], outs[1], outs[2], list(outs[3:])


def _hgrn_kernel(*refs, has_init, NB, TL, C):
    it = iter(refs)
    x_ref = next(it)
    s0_ref = next(it) if has_init else None
    nw_ref, win_ref, lb_ref, gnw_ref, wo_ref = [next(it) for _ in range(5)]
    xo_ref = next(it)
    so_ref = next(it)
    (st_scr, tri_scr, qe_scr, qm_scr, ke_scr, kw_scr, v_scr, g_scr, elT_scr, sc_scr,
     o_scr) = [next(it) for _ in range(11)]
    assert C <= HGRN_HEAD_DIM
    TB = tri_scr.shape[0]

    R = NB * TL
    n_ch = R // C
    ch_per_seq = TL // C
    HD = HGRN_HEAD_DIM
    i = pl.program_id(1)
    n_i = pl.num_programs(1)

    @pl.when(i == 0)
    def _init():
        tri_scr[...] = _chunk_triangle(TB, C)
        if has_init:
            st_scr[...] = s0_ref[...]
        else:
            st_scr[...] = jnp.zeros_like(st_scr)

    x = x_ref[...].reshape(R, D_MODEL)
    h16 = _rms(x, nw_ref[...]).astype(BF16)
    lb_soft = jax.nn.softmax(lb_ref[...], axis=0)
    lb = (lb_soft[0:1, :] + lb_soft[1:2, :]) - lb_soft[0:1, :]

    for c0 in range(0, D_MODEL, HGRN_COLS):
        cols = slice(c0, c0 + HGRN_COLS)
        q = _silu(_dot(h16, win_ref[:, c0:c0 + HGRN_COLS]))
        f = _dot(h16, win_ref[:, D_MODEL + c0:D_MODEL + c0 + HGRN_COLS])
        v = _dot(h16, win_ref[:, 2 * D_MODEL + c0:2 * D_MODEL + c0 + HGRN_COLS])
        g = _dot(h16, win_ref[:, 3 * D_MODEL + c0:3 * D_MODEL + c0 + HGRN_COLS])
        lbc = lb[:, cols]
        forget = lbc + (1.0 - lbc) * jax.nn.sigmoid(f)
        k = 1.0 - forget
        logf = _split3(jnp.log(forget))
        b = jnp.concatenate([sum(_dot(tri_scr[...], p[r0:r0 + TB]) for p in logf) for r0 in range(0, R, TB)], axis=0)
        ends = [b[(c + 1) * C - 1:(c + 1) * C, :] for c in range(n_ch)]
        last = jnp.concatenate([jnp.broadcast_to(e, (C, HGRN_COLS)) for e in ends], axis=0)
        mid = jnp.concatenate([jnp.broadcast_to(b[c * C + C // 2:c * C + C // 2 + 1, :], (C, HGRN_COLS))
                               for c in range(n_ch)], axis=0)
        qe_scr[:, cols] = (q * jnp.exp(b)).astype(BF16)
        qm_scr[:, cols] = (q * jnp.exp(b - mid)).astype(BF16)
        ke_scr[:, cols] = (k * jnp.exp(mid - b)).astype(BF16)
        kw_scr[:, cols] = (k * jnp.exp(last - b)).astype(BF16)
        v_scr[:, cols] = v.astype(BF16)
        g_scr[:, cols] = _silu(g)
        el = jnp.exp(jnp.concatenate(ends + [jnp.zeros((LANES - n_ch, HGRN_COLS), F32)], axis=0))
        elT_scr[cols, :] = el.T

    rr = lax.broadcasted_iota(jnp.int32, (C, C), 0)
    cc = lax.broadcasted_iota(jnp.int32, (C, C), 1)
    tril = cc <= rr

    for c in range(n_ch):
        rows = slice(c * C, (c + 1) * C)
        for hh in range(HGRN_N_HEADS):
            sl = slice(hh * HD, (hh + 1) * HD)
            sc = jnp.where(tril, _dot_nt(qm_scr[rows, sl], ke_scr[rows, sl]), 0.0)
            sc_scr[rows, hh * HD:hh * HD + C] = sc.astype(BF16)

    for c in range(n_ch):
        nb = c // ch_per_seq
        rows = slice(c * C, (c + 1) * C)
        for hh in range(HGRN_N_HEADS):
            sl = slice(hh * HD, (hh + 1) * HD)
            v = v_scr[rows, sl]
            s_h = st_scr[nb, hh]
            o_scr[rows, sl] = _dot(jnp.concatenate([qe_scr[rows, sl], sc_scr[rows, hh * HD:hh * HD + C]], axis=1),
                                   jnp.concatenate([s_h.astype(BF16), v], axis=0))
            st_scr[nb, hh] = elT_scr[sl, c:c + 1] * s_h + _dot_tn(kw_scr[rows, sl], v)

    out = None
    for c0 in range(0, D_MODEL, HGRN_COLS):
        parts = []
        for h0 in range(c0, c0 + HGRN_COLS, HD):
            blk = o_scr[:, h0:h0 + HD]
            parts.append(blk * lax.rsqrt(jnp.mean(blk * blk, axis=-1, keepdims=True) + NORM_EPS))
        cols = slice(c0, c0 + HGRN_COLS)
        on = jnp.concatenate(parts, axis=1) * gnw_ref[:, cols] * g_scr[:, cols]
        part = _dot(on.astype(BF16), wo_ref[cols, :])
        out = part if out is None else out + part
    xo_ref[...] = (x + out).reshape(NB, TL, D_MODEL)

    @pl.when(i == n_i - 1)
    def _fin():
        so_ref[...] = st_scr[...]


def _hgrn_mixer(x, s0, p, *, NB, TL, C):
    n_seq, L, _ = x.shape
    has_init = s0 is not None
    R = NB * TL
    grid = (n_seq // NB, L // TL)
    st_block = (NB, HGRN_N_HEADS, HGRN_HEAD_DIM, HGRN_HEAD_DIM)
    in_specs = [pl.BlockSpec((NB, TL, D_MODEL), lambda b, i: (b, i, 0))]
    args = [x]
    if has_init:
        in_specs.append(pl.BlockSpec(st_block, lambda b, i: (b, 0, 0, 0)))
        args.append(s0)
    consts = [p["nw"], p["win"], p["lb"], p["gnw"], p["wo"]]
    for c in consts:
        in_specs.append(_layer_spec(c, 0) if c.ndim == 3 else _const_spec(c.shape))
    args += consts
    out_shape = (jax.ShapeDtypeStruct((n_seq, L, D_MODEL), F32),
                 jax.ShapeDtypeStruct((n_seq,) + st_block[1:], F32))
    out_specs = (pl.BlockSpec((NB, TL, D_MODEL), lambda b, i: (b, i, 0)),
                 pl.BlockSpec(st_block, lambda b, i: (b, 0, 0, 0)))
    scratch = [pltpu.VMEM(st_block, F32),
               pltpu.VMEM((min(R, MXU_DEPTH),) * 2, BF16),
               pltpu.VMEM((R, D_MODEL), BF16),
               pltpu.VMEM((R, D_MODEL), BF16),
               pltpu.VMEM((R, D_MODEL), BF16),
               pltpu.VMEM((R, D_MODEL), BF16),
               pltpu.VMEM((R, D_MODEL), BF16),
               pltpu.VMEM((R, D_MODEL), F32),
               pltpu.VMEM((D_MODEL, LANES), F32),
               pltpu.VMEM((R, D_MODEL), BF16),
               pltpu.VMEM((R, D_MODEL), F32)]
    return pl.pallas_call(
        functools.partial(_hgrn_kernel, has_init=has_init, NB=NB, TL=TL, C=C),
        grid=grid, in_specs=in_specs, out_specs=out_specs, out_shape=out_shape,
        scratch_shapes=scratch,
        compiler_params=pltpu.CompilerParams(
            dimension_semantics=("arbitrary", "arbitrary"), vmem_limit_bytes=VMEM_LIMIT),
        name="hgrn_mixer_init" if has_init else "hgrn_mixer_zero",
    )(*args)


def _ffn_kernel(*refs, final, n_side):
    it = iter(refs)
    x_ref, nw_ref, wg_ref, wu_ref, wd_ref = [next(it) for _ in range(5)]
    fw_ref = next(it) if final else None
    side_in = [next(it) for _ in range(n_side)]
    o_ref = next(it)
    for w_ref in side_in:
        next(it)[...] = w_ref[...].astype(BF16)
    x = x_ref[...]
    h16 = _rms(x, nw_ref[...]).astype(BF16)
    act = _silu(_dot(h16, wg_ref[...])) * _dot(h16, wu_ref[...])
    y = x + _dot(act.astype(BF16), wd_ref[...])
    if final:
        y = _rms(y, fw_ref[...])
    o_ref[...] = y


def _ffn(x2d, nw, wg, wu, wd, final_w, side, *, TM):
    rows = x2d.shape[0]
    final = final_w is not None
    n_steps = rows // TM
    side_in, side_out, side_shapes = _side_cast(side, n_steps, lambda i: i)
    consts = [nw, wg, wu, wd] + ([final_w] if final else [])
    in_specs = [pl.BlockSpec((TM, D_MODEL), lambda i: (i, 0))] + [_const_spec(c.shape) for c in consts] + side_in
    outs = pl.pallas_call(
        functools.partial(_ffn_kernel, final=final, n_side=len(side)),
        grid=(n_steps,),
        in_specs=in_specs,
        out_specs=[pl.BlockSpec((TM, D_MODEL), lambda i: (i, 0))] + side_out,
        out_shape=[jax.ShapeDtypeStruct((rows, D_MODEL), F32)] + side_shapes,
        compiler_params=pltpu.CompilerParams(
            dimension_semantics=("arbitrary",), vmem_limit_bytes=VMEM_LIMIT),
        name="swiglu_final" if final else "swiglu",
    )(x2d, *consts, *[w for w, _ in side])
    return outs[0], list(outs[1:])


def _tiles(n_seq, L):
    if L >= SSD_LONG_TILE:
        return dict(NB=1, TL_SSD=SSD_LONG_TILE, TL_HGRN=HGRN_LONG_TILE, C=SCAN_CHUNK, C_HGRN=HGRN_CHUNK,
                    TM=FFN_ROWS)
    return dict(NB=SHORT_SEQ_PER_STEP, TL_SSD=L, TL_HGRN=L, C=min(L, SCAN_CHUNK), C_HGRN=min(L, HGRN_CHUNK),
                TM=min(FFN_ROWS, n_seq * L))


def _trunk(x, s_ssd, cv, s_hgrn, ssd_p, hgrn_p, ffn_nw, final_w, w16, raw):
    n_seq, L, _ = x.shape
    t = _tiles(n_seq, L)
    s0 = None if s_ssd is None else s_ssd.reshape(n_seq, SSD_D_INNER, SSD_D_STATE)

    def missing(names):
        return [k for k in names if k not in w16]

    need = missing(("wg0", "wu0", "wd0"))
    x, s_new, cv_new, cast = _ssd_mixer(x, s0, cv, ssd_p, [raw[k] for k in need],
                                        NB=t["NB"], TL=t["TL_SSD"], C=t["C"])
    w16.update(zip(need, cast))
    need = missing(("hgrn_in", "hgrn_out", "wg1", "wu1", "wd1"))
    x, cast = _ffn(x.reshape(n_seq * L, D_MODEL), ffn_nw[0], w16["wg0"], w16["wu0"], w16["wd0"], None,
                   [raw[k] for k in need], TM=t["TM"])
    w16.update(zip(need, cast))
    x, h_new = _hgrn_mixer(x.reshape(n_seq, L, D_MODEL), s_hgrn,
                           dict(hgrn_p, win=w16["hgrn_in"], wo=w16["hgrn_out"]),
                           NB=t["NB"], TL=t["TL_HGRN"], C=t["C_HGRN"])
    y, _ = _ffn(x.reshape(n_seq * L, D_MODEL), ffn_nw[1], w16["wg1"], w16["wu1"], w16["wd1"], final_w,
                [], TM=t["TM"])
    s_new = s_new.reshape(1, n_seq, SSD_N_HEADS, SSD_HEAD_DIM, SSD_D_STATE)
    return y.reshape(n_seq, L, D_MODEL), s_new, cv_new[None], h_new[None]


def kernel(x_prompt, x_sample, state_ssd, cache_conv, state_hgrn, ssd_norm_w, ssd_in_w, ssd_conv_w, ssd_conv_b, ssd_dt_bias, ssd_A_log, ssd_D, ssd_gnorm_w, ssd_out_w, hgrn_norm_w, hgrn_in_w, hgrn_lower_bounds, hgrn_gnorm_w, hgrn_out_w, ffn_norm_w, ffn_w_gate, ffn_w_up, ffn_w_down, final_norm_w):
    pair_order = jnp.concatenate([jnp.arange(0, SSD_N_HEADS, 2), jnp.arange(1, SSD_N_HEADS, 2)])
    ssd_p = {
        "nw": ssd_norm_w[0].reshape(1, D_MODEL),
        "win": ssd_in_w.astype(BF16),
        "cw": ssd_conv_w,
        "cb": ssd_conv_b[0].reshape(1, SSD_CONV_DIM),
        "dtb": ssd_dt_bias[0].reshape(1, SSD_N_HEADS),
        "dtbT": ssd_dt_bias[0][pair_order].reshape(SSD_N_HEADS, 1),
        "alog": ssd_A_log[0],
        "dexp": jnp.repeat(ssd_D[0], SSD_HEAD_DIM).reshape(1, SSD_D_INNER),
        "gnw": ssd_gnorm_w[0].reshape(1, SSD_D_INNER),
        "wo": ssd_out_w.astype(BF16),
    }
    hgrn_p = {
        "nw": hgrn_norm_w[0].reshape(1, D_MODEL),
        "lb": hgrn_lower_bounds,
        "gnw": jnp.tile(hgrn_gnorm_w[0], HGRN_N_HEADS).reshape(1, D_MODEL),
    }
    ffn_nw = ffn_norm_w.reshape(2, 1, D_MODEL)
    final_w = final_norm_w.reshape(1, D_MODEL)
    raw = {"wg0": (ffn_w_gate, 0), "wu0": (ffn_w_up, 0), "wd0": (ffn_w_down, 0),
           "hgrn_in": (hgrn_in_w, 0), "hgrn_out": (hgrn_out_w, 0),
           "wg1": (ffn_w_gate, 1), "wu1": (ffn_w_up, 1), "wd1": (ffn_w_down, 1)}
    w16 = {}

    y_p, ssd_s_p, conv_p, hgrn_s_p = _trunk(
        x_prompt, None, None, None, ssd_p, hgrn_p, ffn_nw, final_w, w16, raw)
    y_s, ssd_s_s, conv_s, hgrn_s_s = _trunk(
        x_sample, state_ssd[0], cache_conv[0], state_hgrn[0], ssd_p, hgrn_p, ffn_nw, final_w, w16, raw)
    return (y_p, y_s, ssd_s_p, conv_p, hgrn_s_p, ssd_s_s, conv_s, hgrn_s_s)
```

```python
import functools

import jax
import jax.numpy as jnp
from jax import lax
from jax.experimental import pallas as pl
from jax.experimental.pallas import tpu as pltpu

F32 = jnp.float32
BF16 = jnp.bfloat16

D_MODEL = 1024
NORM_EPS = 1e-6

SSD_D_INNER = 2048
SSD_HEAD_DIM = 64
SSD_N_HEADS = 32
SSD_N_GROUPS = 4
SSD_D_STATE = 128
SSD_CONV_W = 4
SSD_CONV_DIM = 3072
SSD_NORM_GROUP = 512
SSD_N_PAIRS = SSD_N_HEADS // 2
SSD_XBC_OFF = SSD_D_INNER
SSD_DT_OFF = SSD_D_INNER + SSD_CONV_DIM

HGRN_HEAD_DIM = 128
HGRN_N_HEADS = 8

FFN_HIDDEN = 2816

LANES = 128
SUBLANES = 8
BF16_TILE_ROWS = 16
MXU_DEPTH = 256
SCAN_CHUNK = 64
HGRN_CHUNK = 128
SSD_LONG_TILE = 512
HGRN_LONG_TILE = 1024
SHORT_SEQ_PER_STEP = 4
FFN_ROWS = 512
CONV_COLS = 512
HGRN_COLS = 256
VMEM_LIMIT = 56 * 1024 * 1024
SSD_VMEM_LIMIT = 62 * 1024 * 1024


def _dot(a, b):
    return jnp.dot(a, b, preferred_element_type=F32)


def _dot_nt(a, b):
    return lax.dot_general(a, b, (((1,), (1,)), ((), ())), preferred_element_type=F32)


def _dot_tn(a, b):
    return lax.dot_general(a, b, (((0,), (0,)), ((), ())), preferred_element_type=F32)


def _split3(a):
    hi = a.astype(BF16)
    r1 = a - hi.astype(F32)
    mid = r1.astype(BF16)
    lo = (r1 - mid.astype(F32)).astype(BF16)
    return hi, mid, lo


def _stack3(a):
    hi = a.astype(BF16).astype(F32)
    r1 = a - hi
    mid = r1.astype(BF16).astype(F32)
    return jnp.concatenate([hi, mid, r1 - mid, jnp.zeros_like(a)], axis=1).astype(BF16)


def _rms(x, w):
    return x * lax.rsqrt(jnp.mean(x * x, axis=-1, keepdims=True) + NORM_EPS) * w


def _silu(x):
    return x * jax.nn.sigmoid(x)


def _shift_rows(u, first_rows):
    rows, cols = u.shape
    k = first_rows.shape[0]
    nv = rows // SUBLANES
    r = pltpu.roll(u.reshape(nv, SUBLANES, cols), k, axis=1)
    first = jnp.concatenate([first_rows, jnp.zeros((SUBLANES - k, cols), u.dtype)], axis=0)
    prev = jnp.concatenate([first[None], r[:nv - 1]], axis=0)
    sub = lax.broadcasted_iota(jnp.int32, (nv, SUBLANES, cols), 1)
    return jnp.where(sub < k, prev, r).reshape(rows, cols)


def _chunk_triangle(R, C):
    rr = lax.broadcasted_iota(jnp.int32, (R, R), 0)
    cc = lax.broadcasted_iota(jnp.int32, (R, R), 1)
    return ((cc <= rr) & ((rr & -C) == (cc & -C))).astype(BF16)


def _ssd_kernel(*refs, has_init, NB, TL, C, n_side):
    it = iter(refs)
    x_ref = next(it)
    s0_ref = cv0_ref = None
    if has_init:
        s0_ref = next(it)
        cv0_ref = next(it)
    (nw_ref, win_ref, cw_ref, cb_ref, dtb_ref, dtbT_ref,
     alog_ref, alogp_ref, dexp_ref, gnw_ref, wo_ref) = [next(it) for _ in range(11)]
    side_in = [next(it) for _ in range(n_side)]
    xo_ref = next(it)
    so_ref = next(it)
    cvo_ref = next(it)
    for w_ref in side_in:
        next(it)[...] = w_ref[...].astype(BF16)
    (st_scr, st16_scr, hist_scr, z_scr, xbc_scr, y_scr, e_scr, tri_scr,
     colc_scr, mix_scr, xw16_scr, xsl16_scr, xsr16_scr, bc16_scr) = [next(it) for _ in range(14)]
    es_scr = colcs_scr = None
    if 2 * C != LANES:
        es_scr = next(it)
        colcs_scr = next(it)

    R = NB * TL
    n_ch = R // C
    ch_per_seq = TL // C
    W2 = 2 * C
    i = pl.program_id(1)
    n_i = pl.num_programs(1)

    @pl.when(i == 0)
    def _init():
        ek = lax.broadcasted_iota(jnp.int32, e_scr.shape, 0)
        el = lax.broadcasted_iota(jnp.int32, e_scr.shape, 1)
        e_scr[...] = ((ek < 3 * SSD_N_HEADS) & ((ek & (SSD_N_HEADS - 1)) == el // SSD_HEAD_DIM)).astype(BF16)
        if es_scr is not None:
            ek = lax.broadcasted_iota(jnp.int32, es_scr.shape, 0)
            el = lax.broadcasted_iota(jnp.int32, es_scr.shape, 1)
            es_scr[...] = ((ek < 3 * SSD_N_HEADS) & ((ek & (SSD_N_HEADS - 1)) == el // C)).astype(BF16)
        tri_scr[...] = _chunk_triangle(R, C)
        hist_scr[...] = jnp.zeros_like(hist_scr)
        if has_init:
            for nb in range(NB):
                for g in range(SSD_N_GROUPS):
                    st_scr[nb, g] = s0_ref[nb, g * SSD_NORM_GROUP:(g + 1) * SSD_NORM_GROUP, :].T
                for k in range(SSD_CONV_DIM // CONV_COLS):
                    hist_scr[nb, k, SUBLANES - 3:SUBLANES, :] = cv0_ref[nb, :, k * CONV_COLS:(k + 1) * CONV_COLS]
        else:
            st_scr[...] = jnp.zeros_like(st_scr)
        st16_scr[...] = st_scr[...].astype(BF16)

    x = x_ref[...].reshape(R, D_MODEL)
    h16 = _rms(x, nw_ref[...]).astype(BF16)

    for cb0 in range(0, SSD_CONV_DIM, CONV_COLS):
        cols = slice(cb0, cb0 + CONV_COLS)
        xr = _dot(h16, win_ref[:, SSD_XBC_OFF + cb0:SSD_XBC_OFF + cb0 + CONV_COLS])
        w0, w1, w2, w3 = (cw_ref[k:k + 1, cols] for k in range(SSD_CONV_W))
        for nb in range(NB):
            xn = xr[nb * TL:(nb + 1) * TL]
            hm = hist_scr[nb, cb0 // CONV_COLS]
            sx = _shift_rows(xn, hm[7:8])
            a2 = _shift_rows(w1 * xn + w0 * sx, w1 * hm[6:8] + w0 * hm[5:7])
            xbc_scr[nb * TL:(nb + 1) * TL, cols] = _silu(w3 * xn + w2 * sx + a2 + cb_ref[:, cols])
            hist_scr[nb, cb0 // CONV_COLS] = xn[TL - SUBLANES:TL]
    for zb0 in range(0, SSD_D_INNER, CONV_COLS):
        z_scr[:, zb0:zb0 + CONV_COLS] = _silu(_dot(h16, win_ref[:, zb0:zb0 + CONV_COLS]))

    dt_raw = _dot(h16, win_ref[:, SSD_DT_OFF:])
    dt = jax.nn.softplus(dt_raw + dtb_ref[...])
    pi = lax.broadcasted_iota(jnp.int32, (SSD_N_HEADS, SSD_N_HEADS), 0)
    pj = lax.broadcasted_iota(jnp.int32, (SSD_N_HEADS, SSD_N_HEADS), 1)
    pick = (pj == jnp.where(pi < SSD_N_PAIRS, 2 * pi, 2 * (pi - SSD_N_PAIRS) + 1)).astype(BF16)
    dtT = jax.nn.softplus(sum(_dot_nt(pick, p) for p in _split3(dt_raw)) + dtbT_ref[...])
    dtTp = jnp.concatenate(
        [jnp.concatenate([dtT[0:SSD_N_PAIRS, c * C:(c + 1) * C],
                          dtT[SSD_N_PAIRS:, c * C:(c + 1) * C]], axis=1) for c in range(n_ch)],
        axis=0)
    a_row = -jnp.exp(alog_ref[...])
    a_pair = -jnp.exp(alogp_ref[...])
    cum = sum(_dot(tri_scr[...], p) for p in _split3(dt * a_row))
    last = jnp.concatenate(
        [jnp.broadcast_to(cum[(c + 1) * C - 1:(c + 1) * C, :], (C, SSD_N_HEADS)) for c in range(n_ch)], axis=0)
    wdt = jnp.exp(last - cum) * dt
    cum3 = _stack3(cum)
    colc_scr[...] = _dot(cum3, e_scr[...])
    xs = xbc_scr[:, :SSD_D_INNER]
    xw16_scr[...] = (xs * _dot(_stack3(wdt), e_scr[...])).astype(BF16)
    head_left = (lax.broadcasted_iota(jnp.int32, (R, SSD_D_INNER), 1) & (LANES - 1)) < SSD_HEAD_DIM
    xsl16_scr[...] = jnp.where(head_left, xs, 0.0).astype(BF16)
    xsr16_scr[...] = jnp.where(head_left, 0.0, xs).astype(BF16)
    bc16_scr[...] = xbc_scr[:, SSD_D_INNER:].astype(BF16)
    if colcs_scr is not None:
        colcs_scr[...] = _dot(cum3, es_scr[...])
    r2 = lax.broadcasted_iota(jnp.int32, (W2, W2), 0)
    c2 = lax.broadcasted_iota(jnp.int32, (W2, W2), 1)
    triT2_16 = ((r2 <= c2) & ((r2 < C) == (c2 < C))).astype(BF16)
    aTp = dtTp * jnp.concatenate([a_pair] * n_ch, axis=0)
    cumTp = sum(_dot(p, triT2_16) for p in _split3(aTp))

    tp = lax.broadcasted_iota(jnp.int32, (C, W2), 0)
    sp = lax.broadcasted_iota(jnp.int32, (C, W2), 1)
    tril_pair = jnp.where(sp < C, sp, sp - C) <= tp
    colc_l_scr = colc_scr if colcs_scr is None else colcs_scr
    for c in range(n_ch):
        rows = slice(c * C, (c + 1) * C)
        for g in range(SSD_N_GROUPS):
            b16 = bc16_scr[rows, g * SSD_D_STATE:(g + 1) * SSD_D_STATE]
            c16 = bc16_scr[rows, (SSD_N_GROUPS + g) * SSD_D_STATE:(SSD_N_GROUPS + g + 1) * SSD_D_STATE]
            cbm = jnp.where(tril_pair, _dot_nt(c16, jnp.concatenate([b16, b16], axis=0)), 0.0)
            for q in range(4):
                jp = g * 4 + q
                k = c * SSD_N_PAIRS + jp
                lanes_s = slice(jp * W2, (jp + 1) * W2)
                decay = jnp.exp(jnp.minimum(colc_l_scr[rows, lanes_s] - cumTp[k:k + 1, :], 0.0))
                mix_scr[rows, lanes_s] = (cbm * decay * dtTp[k:k + 1, :]).astype(mix_scr.dtype)
    colc_scr[...] = jnp.exp(colc_scr[...])

    for c in range(n_ch):
        nb = c // ch_per_seq
        rows = slice(c * C, (c + 1) * C)
        for g in range(SSD_N_GROUPS):
            gl = slice(g * SSD_NORM_GROUP, (g + 1) * SSD_NORM_GROUP)
            b16 = bc16_scr[rows, g * SSD_D_STATE:(g + 1) * SSD_D_STATE]
            c16 = bc16_scr[rows, (SSD_N_GROUPS + g) * SSD_D_STATE:(SSD_N_GROUPS + g + 1) * SSD_D_STATE]
            y_inter = _dot(c16, st16_scr[nb, g])
            for q in range(4):
                jp = g * 4 + q
                lanes = slice(jp * LANES, (jp + 1) * LANES)
                mix16 = mix_scr[rows, jp * W2:(jp + 1) * W2].astype(BF16)
                rhs = jnp.concatenate([xsl16_scr[rows, lanes], xsr16_scr[rows, lanes]], axis=0)
                y_scr[rows, lanes] = _dot(mix16, rhs) + colc_scr[rows, lanes] * y_inter[:, q * LANES:(q + 1) * LANES]
            el_g = colc_scr[(c + 1) * C - 1:(c + 1) * C, gl]
            st_new = el_g * st_scr[nb, g] + _dot_tn(b16, xw16_scr[rows, gl])
            st_scr[nb, g] = st_new
            st16_scr[nb, g] = st_new.astype(BF16)

    out = None
    for g in range(SSD_N_GROUPS):
        cols = slice(g * SSD_NORM_GROUP, (g + 1) * SSD_NORM_GROUP)
        yg = (y_scr[:, cols] + dexp_ref[:, cols] * xbc_scr[:, cols]) * z_scr[:, cols]
        yn = yg * lax.rsqrt(jnp.mean(yg * yg, axis=-1, keepdims=True) + NORM_EPS) * gnw_ref[:, cols]
        part = _dot(yn.astype(BF16), wo_ref[cols, :])
        out = part if out is None else out + part
    xo_ref[...] = (x + out).reshape(NB, TL, D_MODEL)

    @pl.when(i == n_i - 1)
    def _fin():
        for nb in range(NB):
            for g in range(SSD_N_GROUPS):
                so_ref[nb, g * SSD_NORM_GROUP:(g + 1) * SSD_NORM_GROUP, :] = st_scr[nb, g].T
            for k in range(SSD_CONV_DIM // CONV_COLS):
                cvo_ref[nb, :, k * CONV_COLS:(k + 1) * CONV_COLS] = hist_scr[nb, k, SUBLANES - 3:SUBLANES, :]


def _const_spec(shape):
    nd = len(shape)
    return pl.BlockSpec(shape, lambda *_: (0,) * nd, pipeline_mode=pl.Buffered(1))


def _layer_spec(arr, layer):
    nd = arr.ndim - 1
    return pl.BlockSpec((None,) + arr.shape[1:], lambda *_: (layer,) + (0,) * nd,
                        pipeline_mode=pl.Buffered(1))


def _side_cast(side, n_steps, step_of):
    in_specs, out_specs, out_shapes = [], [], []
    for w, layer in side:
        rows, cols = w.shape[1:]
        n_blk = max(n for n in range(1, n_steps + 1) if rows % n == 0 and (rows // n) % BF16_TILE_ROWS == 0)
        br = rows // n_blk
        in_specs.append(pl.BlockSpec(
            (None, br, cols), lambda *g, layer=layer, n_blk=n_blk: (layer, jnp.minimum(step_of(*g), n_blk - 1), 0)))
        out_specs.append(pl.BlockSpec(
            (br, cols), lambda *g, n_blk=n_blk: (jnp.minimum(step_of(*g), n_blk - 1), 0)))
        out_shapes.append(jax.ShapeDtypeStruct((rows, cols), BF16))
    return in_specs, out_specs, out_shapes


def _ssd_mixer(x, s0, cv0, p, side, *, NB, TL, C):
    n_seq, L, _ = x.shape
    has_init = s0 is not None
    R = NB * TL
    grid = (n_seq // NB, L // TL)
    side_in, side_out, side_shapes = _side_cast(side, grid[0] * grid[1], lambda b, i: b * grid[1] + i)
    in_specs = [pl.BlockSpec((NB, TL, D_MODEL), lambda b, i: (b, i, 0))]
    args = [x]
    if has_init:
        in_specs += [pl.BlockSpec((NB, SSD_D_INNER, SSD_D_STATE), lambda b, i: (b, 0, 0)),
                     pl.BlockSpec((NB, 3, SSD_CONV_DIM), lambda b, i: (b, 0, 0))]
        args += [s0, cv0]
    alog_pair = jnp.concatenate(
        [jnp.broadcast_to(p["alog"][0::2, None], (SSD_N_PAIRS, C)),
         jnp.broadcast_to(p["alog"][1::2, None], (SSD_N_PAIRS, C))], axis=1)
    consts = [p["nw"], p["win"], p["cw"], p["cb"], p["dtb"], p["dtbT"],
              p["alog"].reshape(1, SSD_N_HEADS), alog_pair, p["dexp"], p["gnw"], p["wo"]]
    for c in consts:
        in_specs.append(_layer_spec(c, 0) if c.ndim == 3 else _const_spec(c.shape))
    args += consts
    in_specs += side_in
    args += [w for w, _ in side]
    out_shape = [jax.ShapeDtypeStruct((n_seq, L, D_MODEL), F32),
                 jax.ShapeDtypeStruct((n_seq, SSD_D_INNER, SSD_D_STATE), F32),
                 jax.ShapeDtypeStruct((n_seq, 3, SSD_CONV_DIM), F32)] + side_shapes
    out_specs = [pl.BlockSpec((NB, TL, D_MODEL), lambda b, i: (b, i, 0)),
                 pl.BlockSpec((NB, SSD_D_INNER, SSD_D_STATE), lambda b, i: (b, 0, 0)),
                 pl.BlockSpec((NB, 3, SSD_CONV_DIM), lambda b, i: (b, 0, 0))] + side_out
    scratch = [pltpu.VMEM((NB, SSD_N_GROUPS, SSD_D_STATE, SSD_NORM_GROUP), F32),
               pltpu.VMEM((NB, SSD_N_GROUPS, SSD_D_STATE, SSD_NORM_GROUP), BF16),
               pltpu.VMEM((NB, SSD_CONV_DIM // CONV_COLS, SUBLANES, CONV_COLS), F32),
               pltpu.VMEM((R, SSD_D_INNER), F32),
               pltpu.VMEM((R, SSD_CONV_DIM), F32),
               pltpu.VMEM((R, SSD_D_INNER), F32),
               pltpu.VMEM((LANES, SSD_D_INNER), BF16),
               pltpu.VMEM((R, R), BF16),
               pltpu.VMEM((R, SSD_D_INNER), F32),
               pltpu.VMEM((R, SSD_N_PAIRS * 2 * C), BF16 if 2 * C == LANES else F32),
               pltpu.VMEM((R, SSD_D_INNER), BF16),
               pltpu.VMEM((R, SSD_D_INNER), BF16),
               pltpu.VMEM((R, SSD_D_INNER), BF16),
               pltpu.VMEM((R, 2 * SSD_N_GROUPS * SSD_D_STATE), BF16)]
    if 2 * C != LANES:
        scratch += [pltpu.VMEM((LANES, SSD_N_PAIRS * 2 * C), BF16),
                    pltpu.VMEM((R, SSD_N_PAIRS * 2 * C), F32)]
    outs = pl.pallas_call(
        functools.partial(_ssd_kernel, has_init=has_init, NB=NB, TL=TL, C=C, n_side=len(side)),
        grid=grid, in_specs=in_specs, out_specs=out_specs, out_shape=out_shape,
        scratch_shapes=scratch,
        compiler_params=pltpu.CompilerParams(
            dimension_semantics=("arbitrary", "arbitrary"), vmem_limit_bytes=SSD_VMEM_LIMIT),
        name="ssd_mixer_init" if has_init else "ssd_mixer_zero",
    )(*args)
    return outs[0], outs[1], outs[2], list(outs[3:])


def _hgrn_kernel(*refs, has_init, NB, TL, C):
    it = iter(refs)
    x_ref = next(it)
    s0_ref = next(it) if has_init else None
    nw_ref, win_ref, lb_ref, gnw_ref, wo_ref = [next(it) for _ in range(5)]
    xo_ref = next(it)
    so_ref = next(it)
    (st_scr, tri_scr, qe_scr, qm_scr, ke_scr, kw_scr, v_scr, g_scr, elT_scr, sc_scr,
     o_scr) = [next(it) for _ in range(11)]
    assert C <= HGRN_HEAD_DIM
    TB = tri_scr.shape[0]

    R = NB * TL
    n_ch = R // C
    ch_per_seq = TL // C
    HD = HGRN_HEAD_DIM
    i = pl.program_id(1)
    n_i = pl.num_programs(1)

    @pl.when(i == 0)
    def _init():
        tri_scr[...] = _chunk_triangle(TB, C)
        if has_init:
            st_scr[...] = s0_ref[...]
        else:
            st_scr[...] = jnp.zeros_like(st_scr)

    x = x_ref[...].reshape(R, D_MODEL)
    h16 = _rms(x, nw_ref[...]).astype(BF16)
    lb_soft = jax.nn.softmax(lb_ref[...], axis=0)
    lb = (lb_soft[0:1, :] + lb_soft[1:2, :]) - lb_soft[0:1, :]

    for c0 in range(0, D_MODEL, HGRN_COLS):
        cols = slice(c0, c0 + HGRN_COLS)
        q = _silu(_dot(h16, win_ref[:, c0:c0 + HGRN_COLS]))
        f = _dot(h16, win_ref[:, D_MODEL + c0:D_MODEL + c0 + HGRN_COLS])
        v = _dot(h16, win_ref[:, 2 * D_MODEL + c0:2 * D_MODEL + c0 + HGRN_COLS])
        g = _dot(h16, win_ref[:, 3 * D_MODEL + c0:3 * D_MODEL + c0 + HGRN_COLS])
        lbc = lb[:, cols]
        forget = lbc + (1.0 - lbc) * jax.nn.sigmoid(f)
        k = 1.0 - forget
        logf = _split3(jnp.log(forget))
        b = jnp.concatenate([sum(_dot(tri_scr[...], p[r0:r0 + TB]) for p in logf) for r0 in range(0, R, TB)], axis=0)
        ends = [b[(c + 1) * C - 1:(c + 1) * C, :] for c in range(n_ch)]
        last = jnp.concatenate([jnp.broadcast_to(e, (C, HGRN_COLS)) for e in ends], axis=0)
        mid = jnp.concatenate([jnp.broadcast_to(b[c * C + C // 2:c * C + C // 2 + 1, :], (C, HGRN_COLS))
                               for c in range(n_ch)], axis=0)
        qe_scr[:, cols] = (q * jnp.exp(b)).astype(BF16)
        qm_scr[:, cols] = (q * jnp.exp(b - mid)).astype(BF16)
        ke_scr[:, cols] = (k * jnp.exp(mid - b)).astype(BF16)
        kw_scr[:, cols] = (k * jnp.exp(last - b)).astype(BF16)
        v_scr[:, cols] = v.astype(BF16)
        g_scr[:, cols] = _silu(g)
        el = jnp.exp(jnp.concatenate(ends + [jnp.zeros((LANES - n_ch, HGRN_COLS), F32)], axis=0))
        elT_scr[cols, :] = el.T

    rr = lax.broadcasted_iota(jnp.int32, (C, C), 0)
    cc = lax.broadcasted_iota(jnp.int32, (C, C), 1)
    tril = cc <= rr

    for c in range(n_ch):
        rows = slice(c * C, (c + 1) * C)
        for hh in range(HGRN_N_HEADS):
            sl = slice(hh * HD, (hh + 1) * HD)
            sc = jnp.where(tril, _dot_nt(qm_scr[rows, sl], ke_scr[rows, sl]), 0.0)
            sc_scr[rows, hh * HD:hh * HD + C] = sc.astype(BF16)

    for c in range(n_ch):
        nb = c // ch_per_seq
        rows = slice(c * C, (c + 1) * C)
        for hh in range(HGRN_N_HEADS):
            sl = slice(hh * HD, (hh + 1) * HD)
            v = v_scr[rows, sl]
            s_h = st_scr[nb, hh]
            o_scr[rows, sl] = _dot(jnp.concatenate([qe_scr[rows, sl], sc_scr[rows, hh * HD:hh * HD + C]], axis=1),
                                   jnp.concatenate([s_h.astype(BF16), v], axis=0))
            st_scr[nb, hh] = elT_scr[sl, c:c + 1] * s_h + _dot_tn(kw_scr[rows, sl], v)

    out = None
    for c0 in range(0, D_MODEL, HGRN_COLS):
        parts = []
        for h0 in range(c0, c0 + HGRN_COLS, HD):
            blk = o_scr[:, h0:h0 + HD]
            parts.append(blk * lax.rsqrt(jnp.mean(blk * blk, axis=-1, keepdims=True) + NORM_EPS))
        cols = slice(c0, c0 + HGRN_COLS)
        on = jnp.concatenate(parts, axis=1) * gnw_ref[:, cols] * g_scr[:, cols]
        part = _dot(on.astype(BF16), wo_ref[cols, :])
        out = part if out is None else out + part
    xo_ref[...] = (x + out).reshape(NB, TL, D_MODEL)

    @pl.when(i == n_i - 1)
    def _fin():
        so_ref[...] = st_scr[...]


def _hgrn_mixer(x, s0, p, *, NB, TL, C):
    n_seq, L, _ = x.shape
    has_init = s0 is not None
    R = NB * TL
    grid = (n_seq // NB, L // TL)
    st_block = (NB, HGRN_N_HEADS, HGRN_HEAD_DIM, HGRN_HEAD_DIM)
    in_specs = [pl.BlockSpec((NB, TL, D_MODEL), lambda b, i: (b, i, 0))]
    args = [x]
    if has_init:
        in_specs.append(pl.BlockSpec(st_block, lambda b, i: (b, 0, 0, 0)))
        args.append(s0)
    consts = [p["nw"], p["win"], p["lb"], p["gnw"], p["wo"]]
    for c in consts:
        in_specs.append(_layer_spec(c, 0) if c.ndim == 3 else _const_spec(c.shape))
    args += consts
    out_shape = (jax.ShapeDtypeStruct((n_seq, L, D_MODEL), F32),
                 jax.ShapeDtypeStruct((n_seq,) + st_block[1:], F32))
    out_specs = (pl.BlockSpec((NB, TL, D_MODEL), lambda b, i: (b, i, 0)),
                 pl.BlockSpec(st_block, lambda b, i: (b, 0, 0, 0)))
    scratch = [pltpu.VMEM(st_block, F32),
               pltpu.VMEM((min(R, MXU_DEPTH),) * 2, BF16),
               pltpu.VMEM((R, D_MODEL), BF16),
               pltpu.VMEM((R, D_MODEL), BF16),
               pltpu.VMEM((R, D_MODEL), BF16),
               pltpu.VMEM((R, D_MODEL), BF16),
               pltpu.VMEM((R, D_MODEL), BF16),
               pltpu.VMEM((R, D_MODEL), F32),
               pltpu.VMEM((D_MODEL, LANES), F32),
               pltpu.VMEM((R, D_MODEL), BF16),
               pltpu.VMEM((R, D_MODEL), F32)]
    return pl.pallas_call(
        functools.partial(_hgrn_kernel, has_init=has_init, NB=NB, TL=TL, C=C),
        grid=grid, in_specs=in_specs, out_specs=out_specs, out_shape=out_shape,
        scratch_shapes=scratch,
        compiler_params=pltpu.CompilerParams(
            dimension_semantics=("arbitrary", "arbitrary"), vmem_limit_bytes=VMEM_LIMIT),
        name="hgrn_mixer_init" if has_init else "hgrn_mixer_zero",
    )(*args)


def _ffn_kernel(*refs, final, n_side):
    it = iter(refs)
    x_ref, nw_ref, wg_ref, wu_ref, wd_ref = [next(it) for _ in range(5)]
    fw_ref = next(it) if final else None
    side_in = [next(it) for _ in range(n_side)]
    o_ref = next(it)
    for w_ref in side_in:
        next(it)[...] = w_ref[...].astype(BF16)
    x = x_ref[...]
    h16 = _rms(x, nw_ref[...]).astype(BF16)
    act = _silu(_dot(h16, wg_ref[...])) * _dot(h16, wu_ref[...])
    y = x + _dot(act.astype(BF16), wd_ref[...])
    if final:
        y = _rms(y, fw_ref[...])
    o_ref[...] = y


def _ffn(x2d, nw, wg, wu, wd, final_w, side, *, TM):
    rows = x2d.shape[0]
    final = final_w is not None
    n_steps = rows // TM
    side_in, side_out, side_shapes = _side_cast(side, n_steps, lambda i: i)
    consts = [nw, wg, wu, wd] + ([final_w] if final else [])
    in_specs = [pl.BlockSpec((TM, D_MODEL), lambda i: (i, 0))] + [_const_spec(c.shape) for c in consts] + side_in
    outs = pl.pallas_call(
        functools.partial(_ffn_kernel, final=final, n_side=len(side)),
        grid=(n_steps,),
        in_specs=in_specs,
        out_specs=[pl.BlockSpec((TM, D_MODEL), lambda i: (i, 0))] + side_out,
        out_shape=[jax.ShapeDtypeStruct((rows, D_MODEL), F32)] + side_shapes,
        compiler_params=pltpu.CompilerParams(
            dimension_semantics=("arbitrary",), vmem_limit_bytes=VMEM_LIMIT),
        name="swiglu_final" if final else "swiglu",
    )(x2d, *consts, *[w for w, _ in side])
    return outs[0], list(outs[1:])


def _tiles(n_seq, L):
    if L >= SSD_LONG_TILE:
        return dict(NB=1, TL_SSD=SSD_LONG_TILE, TL_HGRN=HGRN_LONG_TILE, C=SCAN_CHUNK, C_HGRN=HGRN_CHUNK,
                    TM=FFN_ROWS)
    return dict(NB=SHORT_SEQ_PER_STEP, TL_SSD=L, TL_HGRN=L, C=min(L, SCAN_CHUNK), C_HGRN=min(L, HGRN_CHUNK),
                TM=min(FFN_ROWS, n_seq * L))


def _trunk(x, s_ssd, cv, s_hgrn, ssd_p, hgrn_p, ffn_nw, final_w, w16, raw):
    n_seq, L, _ = x.shape
    t = _tiles(n_seq, L)
    s0 = None if s_ssd is None else s_ssd.reshape(n_seq, SSD_D_INNER, SSD_D_STATE)

    def missing(names):
        return [k for k in names if k not in w16]

    need = missing(("wg0", "wu0", "wd0"))
    x, s_new, cv_new, cast = _ssd_mixer(x, s0, cv, ssd_p, [raw[k] for k in need],
                                        NB=t["NB"], TL=t["TL_SSD"], C=t["C"])
    w16.update(zip(need, cast))
    need = missing(("hgrn_in", "hgrn_out", "wg1", "wu1", "wd1"))
    x, cast = _ffn(x.reshape(n_seq * L, D_MODEL), ffn_nw[0], w16["wg0"], w16["wu0"], w16["wd0"], None,
                   [raw[k] for k in need], TM=t["TM"])
    w16.update(zip(need, cast))
    x, h_new = _hgrn_mixer(x.reshape(n_seq, L, D_MODEL), s_hgrn,
                           dict(hgrn_p, win=w16["hgrn_in"], wo=w16["hgrn_out"]),
                           NB=t["NB"], TL=t["TL_HGRN"], C=t["C_HGRN"])
    y, _ = _ffn(x.reshape(n_seq * L, D_MODEL), ffn_nw[1], w16["wg1"], w16["wu1"], w16["wd1"], final_w,
                [], TM=t["TM"])
    s_new = s_new.reshape(1, n_seq, SSD_N_HEADS, SSD_HEAD_DIM, SSD_D_STATE)
    return y.reshape(n_seq, L, D_MODEL), s_new, cv_new[None], h_new[None]


def kernel(x_prompt, x_sample, state_ssd, cache_conv, state_hgrn, ssd_norm_w, ssd_in_w, ssd_conv_w, ssd_conv_b, ssd_dt_bias, ssd_A_log, ssd_D, ssd_gnorm_w, ssd_out_w, hgrn_norm_w, hgrn_in_w, hgrn_lower_bounds, hgrn_gnorm_w, hgrn_out_w, ffn_norm_w, ffn_w_gate, ffn_w_up, ffn_w_down, final_norm_w):
    pair_order = jnp.concatenate([jnp.arange(0, SSD_N_HEADS, 2), jnp.arange(1, SSD_N_HEADS, 2)])
    ssd_p = {
        "nw": ssd_norm_w[0].reshape(1, D_MODEL),
        "win": ssd_in_w.astype(BF16),
        "cw": ssd_conv_w,
        "cb": ssd_conv_b[0].reshape(1, SSD_CONV_DIM),
        "dtb": ssd_dt_bias[0].reshape(1, SSD_N_HEADS),
        "dtbT": ssd_dt_bias[0][pair_order].reshape(SSD_N_HEADS, 1),
        "alog": ssd_A_log[0],
        "dexp": jnp.repeat(ssd_D[0], SSD_HEAD_DIM).reshape(1, SSD_D_INNER),
        "gnw": ssd_gnorm_w[0].reshape(1, SSD_D_INNER),
        "wo": ssd_out_w.astype(BF16),
    }
    hgrn_p = {
        "nw": hgrn_norm_w[0].reshape(1, D_MODEL),
        "lb": hgrn_lower_bounds,
        "gnw": jnp.tile(hgrn_gnorm_w[0], HGRN_N_HEADS).reshape(1, D_MODEL),
    }
    ffn_nw = ffn_norm_w.reshape(2, 1, D_MODEL)
    final_w = final_norm_w.reshape(1, D_MODEL)
    raw = {"wg0": (ffn_w_gate, 0), "wu0": (ffn_w_up, 0), "wd0": (ffn_w_down, 0),
           "hgrn_in": (hgrn_in_w, 0), "hgrn_out": (hgrn_out_w, 0),
           "wg1": (ffn_w_gate, 1), "wu1": (ffn_w_up, 1), "wd1": (ffn_w_down, 1)}
    w16 = {}

    y_p, ssd_s_p, conv_p, hgrn_s_p = _trunk(
        x_prompt, None, None, None, ssd_p, hgrn_p, ffn_nw, final_w, w16, raw)
    y_s, ssd_s_s, conv_s, hgrn_s_s = _trunk(
        x_sample, state_ssd[0], cache_conv[0], state_hgrn[0], ssd_p, hgrn_p, ffn_nw, final_w, w16, raw)
    return (y_p, y_s, ssd_s_p, conv_p, hgrn_s_p, ssd_s_s, conv_s, hgrn_s_s)
```

```python
import functools

import jax
import jax.numpy as jnp
from jax import lax
from jax.experimental import pallas as pl
from jax.experimental.pallas import tpu as pltpu

F32 = jnp.float32
BF16 = jnp.bfloat16

D_MODEL = 1024
NORM_EPS = 1e-6

SSD_D_INNER = 2048
SSD_HEAD_DIM = 64
SSD_N_HEADS = 32
SSD_N_GROUPS = 4
SSD_D_STATE = 128
SSD_CONV_W = 4
SSD_CONV_DIM = 3072
SSD_NORM_GROUP = 512
SSD_N_PAIRS = SSD_N_HEADS // 2
SSD_XBC_OFF = SSD_D_INNER
SSD_DT_OFF = SSD_D_INNER + SSD_CONV_DIM

HGRN_HEAD_DIM = 128
HGRN_N_HEADS = 8

FFN_HIDDEN = 2816

LANES = 128
SUBLANES = 8
BF16_TILE_ROWS = 16
MXU_DEPTH = 256
SCAN_CHUNK = 64
HGRN_CHUNK = 128
SSD_LONG_TILE = 512
HGRN_LONG_TILE = 1024
SHORT_SEQ_PER_STEP = 4
FFN_ROWS = 1024
CONV_COLS = 512
HGRN_COLS = 256
VMEM_LIMIT = 56 * 1024 * 1024
SSD_VMEM_LIMIT = 62 * 1024 * 1024


def _dot(a, b):
    return jnp.dot(a, b, preferred_element_type=F32)


def _dot_nt(a, b):
    return lax.dot_general(a, b, (((1,), (1,)), ((), ())), preferred_element_type=F32)


def _dot_tn(a, b):
    return lax.dot_general(a, b, (((0,), (0,)), ((), ())), preferred_element_type=F32)


def _split3(a):
    hi = a.astype(BF16)
    r1 = a - hi.astype(F32)
    mid = r1.astype(BF16)
    lo = (r1 - mid.astype(F32)).astype(BF16)
    return hi, mid, lo


def _stack3(a):
    hi = a.astype(BF16).astype(F32)
    r1 = a - hi
    mid = r1.astype(BF16).astype(F32)
    return jnp.concatenate([hi, mid, r1 - mid, jnp.zeros_like(a)], axis=1).astype(BF16)


def _rms(x, w):
    return x * lax.rsqrt(jnp.mean(x * x, axis=-1, keepdims=True) + NORM_EPS) * w


def _silu(x):
    return x * jax.nn.sigmoid(x)


def _shift_rows(u, first_rows):
    rows, cols = u.shape
    k = first_rows.shape[0]
    nv = rows // SUBLANES
    r = pltpu.roll(u.reshape(nv, SUBLANES, cols), k, axis=1)
    first = jnp.concatenate([first_rows, jnp.zeros((SUBLANES - k, cols), u.dtype)], axis=0)
    prev = jnp.concatenate([first[None], r[:nv - 1]], axis=0)
    sub = lax.broadcasted_iota(jnp.int32, (nv, SUBLANES, cols), 1)
    return jnp.where(sub < k, prev, r).reshape(rows, cols)


def _chunk_triangle(R, C):
    rr = lax.broadcasted_iota(jnp.int32, (R, R), 0)
    cc = lax.broadcasted_iota(jnp.int32, (R, R), 1)
    return ((cc <= rr) & ((rr & -C) == (cc & -C))).astype(BF16)


def _ssd_kernel(*refs, has_init, NB, TL, C, n_side):
    it = iter(refs)
    x_ref = next(it)
    s0_ref = cv0_ref = None
    if has_init:
        s0_ref = next(it)
        cv0_ref = next(it)
    (nw_ref, win_ref, cw_ref, cb_ref, dtb_ref, dtbT_ref,
     alog_ref, alogp_ref, dexp_ref, gnw_ref, wo_ref) = [next(it) for _ in range(11)]
    side_in = [next(it) for _ in range(n_side)]
    xo_ref = next(it)
    so_ref = next(it)
    cvo_ref = next(it)
    for w_ref in side_in:
        next(it)[...] = w_ref[...].astype(BF16)
    (st_scr, st16_scr, hist_scr, z_scr, xbc_scr, y_scr, e_scr, tri_scr,
     colc_scr, mix_scr, xw16_scr, xsl16_scr, xsr16_scr, bc16_scr) = [next(it) for _ in range(14)]
    es_scr = colcs_scr = None
    if 2 * C != LANES:
        es_scr = next(it)
        colcs_scr = next(it)

    R = NB * TL
    n_ch = R // C
    ch_per_seq = TL // C
    W2 = 2 * C
    i = pl.program_id(1)
    n_i = pl.num_programs(1)

    @pl.when(i == 0)
    def _init():
        ek = lax.broadcasted_iota(jnp.int32, e_scr.shape, 0)
        el = lax.broadcasted_iota(jnp.int32, e_scr.shape, 1)
        e_scr[...] = ((ek < 3 * SSD_N_HEADS) & ((ek & (SSD_N_HEADS - 1)) == el // SSD_HEAD_DIM)).astype(BF16)
        if es_scr is not None:
            ek = lax.broadcasted_iota(jnp.int32, es_scr.shape, 0)
            el = lax.broadcasted_iota(jnp.int32, es_scr.shape, 1)
            es_scr[...] = ((ek < 3 * SSD_N_HEADS) & ((ek & (SSD_N_HEADS - 1)) == el // C)).astype(BF16)
        tri_scr[...] = _chunk_triangle(R, C)
        hist_scr[...] = jnp.zeros_like(hist_scr)
        if has_init:
            for nb in range(NB):
                for g in range(SSD_N_GROUPS):
                    st_scr[nb, g] = s0_ref[nb, g * SSD_NORM_GROUP:(g + 1) * SSD_NORM_GROUP, :].T
                for k in range(SSD_CONV_DIM // CONV_COLS):
                    hist_scr[nb, k, SUBLANES - 3:SUBLANES, :] = cv0_ref[nb, :, k * CONV_COLS:(k + 1) * CONV_COLS]
        else:
            st_scr[...] = jnp.zeros_like(st_scr)
        st16_scr[...] = st_scr[...].astype(BF16)

    x = x_ref[...].reshape(R, D_MODEL)
    h16 = _rms(x, nw_ref[...]).astype(BF16)

    for cb0 in range(0, SSD_CONV_DIM, CONV_COLS):
        cols = slice(cb0, cb0 + CONV_COLS)
        xr = _dot(h16, win_ref[:, SSD_XBC_OFF + cb0:SSD_XBC_OFF + cb0 + CONV_COLS])
        w0, w1, w2, w3 = (cw_ref[k:k + 1, cols] for k in range(SSD_CONV_W))
        for nb in range(NB):
            xn = xr[nb * TL:(nb + 1) * TL]
            hm = hist_scr[nb, cb0 // CONV_COLS]
            sx = _shift_rows(xn, hm[7:8])
            a2 = _shift_rows(w1 * xn + w0 * sx, w1 * hm[6:8] + w0 * hm[5:7])
            xbc_scr[nb * TL:(nb + 1) * TL, cols] = _silu(w3 * xn + w2 * sx + a2 + cb_ref[:, cols])
            hist_scr[nb, cb0 // CONV_COLS] = xn[TL - SUBLANES:TL]
    for zb0 in range(0, SSD_D_INNER, CONV_COLS):
        z_scr[:, zb0:zb0 + CONV_COLS] = _silu(_dot(h16, win_ref[:, zb0:zb0 + CONV_COLS]))

    dt_raw = _dot(h16, win_ref[:, SSD_DT_OFF:])
    dt = jax.nn.softplus(dt_raw + dtb_ref[...])
    pi = lax.broadcasted_iota(jnp.int32, (SSD_N_HEADS, SSD_N_HEADS), 0)
    pj = lax.broadcasted_iota(jnp.int32, (SSD_N_HEADS, SSD_N_HEADS), 1)
    pick = (pj == jnp.where(pi < SSD_N_PAIRS, 2 * pi, 2 * (pi - SSD_N_PAIRS) + 1)).astype(BF16)
    dtT = jax.nn.softplus(sum(_dot_nt(pick, p) for p in _split3(dt_raw)) + dtbT_ref[...])
    dtTp = jnp.concatenate(
        [jnp.concatenate([dtT[0:SSD_N_PAIRS, c * C:(c + 1) * C],
                          dtT[SSD_N_PAIRS:, c * C:(c + 1) * C]], axis=1) for c in range(n_ch)],
        axis=0)
    a_row = -jnp.exp(alog_ref[...])
    a_pair = -jnp.exp(alogp_ref[...])
    cum = sum(_dot(tri_scr[...], p) for p in _split3(dt * a_row))
    last = jnp.concatenate(
        [jnp.broadcast_to(cum[(c + 1) * C - 1:(c + 1) * C, :], (C, SSD_N_HEADS)) for c in range(n_ch)], axis=0)
    wdt = jnp.exp(last - cum) * dt
    cum3 = _stack3(cum)
    colc_scr[...] = _dot(cum3, e_scr[...])
    xs = xbc_scr[:, :SSD_D_INNER]
    xw16_scr[...] = (xs * _dot(_stack3(wdt), e_scr[...])).astype(BF16)
    head_left = (lax.broadcasted_iota(jnp.int32, (R, SSD_D_INNER), 1) & (LANES - 1)) < SSD_HEAD_DIM
    xsl16_scr[...] = jnp.where(head_left, xs, 0.0).astype(BF16)
    xsr16_scr[...] = jnp.where(head_left, 0.0, xs).astype(BF16)
    bc16_scr[...] = xbc_scr[:, SSD_D_INNER:].astype(BF16)
    if colcs_scr is not None:
        colcs_scr[...] = _dot(cum3, es_scr[...])
    r2 = lax.broadcasted_iota(jnp.int32, (W2, W2), 0)
    c2 = lax.broadcasted_iota(jnp.int32, (W2, W2), 1)
    triT2_16 = ((r2 <= c2) & ((r2 < C) == (c2 < C))).astype(BF16)
    aTp = dtTp * jnp.concatenate([a_pair] * n_ch, axis=0)
    cumTp = sum(_dot(p, triT2_16) for p in _split3(aTp))

    tp = lax.broadcasted_iota(jnp.int32, (C, W2), 0)
    sp = lax.broadcasted_iota(jnp.int32, (C, W2), 1)
    tril_pair = jnp.where(sp < C, sp, sp - C) <= tp
    colc_l_scr = colc_scr if colcs_scr is None else colcs_scr
    for c in range(n_ch):
        rows = slice(c * C, (c + 1) * C)
        for g in range(SSD_N_GROUPS):
            b16 = bc16_scr[rows, g * SSD_D_STATE:(g + 1) * SSD_D_STATE]
            c16 = bc16_scr[rows, (SSD_N_GROUPS + g) * SSD_D_STATE:(SSD_N_GROUPS + g + 1) * SSD_D_STATE]
            cbm = jnp.where(tril_pair, _dot_nt(c16, jnp.concatenate([b16, b16], axis=0)), 0.0)
            for q in range(4):
                jp = g * 4 + q
                k = c * SSD_N_PAIRS + jp
                lanes_s = slice(jp * W2, (jp + 1) * W2)
                decay = jnp.exp(jnp.minimum(colc_l_scr[rows, lanes_s] - cumTp[k:k + 1, :], 0.0))
                mix_scr[rows, lanes_s] = (cbm * decay * dtTp[k:k + 1, :]).astype(mix_scr.dtype)
    colc_scr[...] = jnp.exp(colc_scr[...])

    for c in range(n_ch):
        nb = c // ch_per_seq
        rows = slice(c * C, (c + 1) * C)
        for g in range(SSD_N_GROUPS):
            gl = slice(g * SSD_NORM_GROUP, (g + 1) * SSD_NORM_GROUP)
            b16 = bc16_scr[rows, g * SSD_D_STATE:(g + 1) * SSD_D_STATE]
            c16 = bc16_scr[rows, (SSD_N_GROUPS + g) * SSD_D_STATE:(SSD_N_GROUPS + g + 1) * SSD_D_STATE]
            y_inter = _dot(c16, st16_scr[nb, g])
            for q in range(4):
                jp = g * 4 + q
                lanes = slice(jp * LANES, (jp + 1) * LANES)
                mix16 = mix_scr[rows, jp * W2:(jp + 1) * W2].astype(BF16)
                rhs = jnp.concatenate([xsl16_scr[rows, lanes], xsr16_scr[rows, lanes]], axis=0)
                y_scr[rows, lanes] = _dot(mix16, rhs) + colc_scr[rows, lanes] * y_inter[:, q * LANES:(q + 1) * LANES]
            el_g = colc_scr[(c + 1) * C - 1:(c + 1) * C, gl]
            st_new = el_g * st_scr[nb, g] + _dot_tn(b16, xw16_scr[rows, gl])
            st_scr[nb, g] = st_new
            st16_scr[nb, g] = st_new.astype(BF16)

    out = None
    for g in range(SSD_N_GROUPS):
        cols = slice(g * SSD_NORM_GROUP, (g + 1) * SSD_NORM_GROUP)
        yg = (y_scr[:, cols] + dexp_ref[:, cols] * xbc_scr[:, cols]) * z_scr[:, cols]
        yn = yg * lax.rsqrt(jnp.mean(yg * yg, axis=-1, keepdims=True) + NORM_EPS) * gnw_ref[:, cols]
        part = _dot(yn.astype(BF16), wo_ref[cols, :])
        out = part if out is None else out + part
    xo_ref[...] = (x + out).reshape(NB, TL, D_MODEL)

    @pl.when(i == n_i - 1)
    def _fin():
        for nb in range(NB):
            for g in range(SSD_N_GROUPS):
                so_ref[nb, g * SSD_NORM_GROUP:(g + 1) * SSD_NORM_GROUP, :] = st_scr[nb, g].T
            for k in range(SSD_CONV_DIM // CONV_COLS):
                cvo_ref[nb, :, k * CONV_COLS:(k + 1) * CONV_COLS] = hist_scr[nb, k, SUBLANES - 3:SUBLANES, :]


def _const_spec(shape):
    nd = len(shape)
    return pl.BlockSpec(shape, lambda *_: (0,) * nd, pipeline_mode=pl.Buffered(1))


def _layer_spec(arr, layer):
    nd = arr.ndim - 1
    return pl.BlockSpec((None,) + arr.shape[1:], lambda *_: (layer,) + (0,) * nd,
                        pipeline_mode=pl.Buffered(1))


def _side_cast(side, n_steps, step_of):
    in_specs, out_specs, out_shapes = [], [], []
    for w, layer in side:
        rows, cols = w.shape[1:]
        n_blk = max(n for n in range(1, n_steps + 1) if rows % n == 0 and (rows // n) % BF16_TILE_ROWS == 0)
        br = rows // n_blk
        in_specs.append(pl.BlockSpec(
            (None, br, cols), lambda *g, layer=layer, n_blk=n_blk: (layer, jnp.minimum(step_of(*g), n_blk - 1), 0)))
        out_specs.append(pl.BlockSpec(
            (br, cols), lambda *g, n_blk=n_blk: (jnp.minimum(step_of(*g), n_blk - 1), 0)))
        out_shapes.append(jax.ShapeDtypeStruct((rows, cols), BF16))
    return in_specs, out_specs, out_shapes


def _ssd_mixer(x, s0, cv0, p, side, *, NB, TL, C):
    n_seq, L, _ = x.shape
    has_init = s0 is not None
    R = NB * TL
    grid = (n_seq // NB, L // TL)
    side_in, side_out, side_shapes = _side_cast(side, grid[0] * grid[1], lambda b, i: b * grid[1] + i)
    in_specs = [pl.BlockSpec((NB, TL, D_MODEL), lambda b, i: (b, i, 0))]
    args = [x]
    if has_init:
        in_specs += [pl.BlockSpec((NB, SSD_D_INNER, SSD_D_STATE), lambda b, i: (b, 0, 0)),
                     pl.BlockSpec((NB, 3, SSD_CONV_DIM), lambda b, i: (b, 0, 0))]
        args += [s0, cv0]
    alog_pair = jnp.concatenate(
        [jnp.broadcast_to(p["alog"][0::2, None], (SSD_N_PAIRS, C)),
         jnp.broadcast_to(p["alog"][1::2, None], (SSD_N_PAIRS, C))], axis=1)
    consts = [p["nw"], p["win"], p["cw"], p["cb"], p["dtb"], p["dtbT"],
              p["alog"].reshape(1, SSD_N_HEADS), alog_pair, p["dexp"], p["gnw"], p["wo"]]
    for c in consts:
        in_specs.append(_layer_spec(c, 0) if c.ndim == 3 else _const_spec(c.shape))
    args += consts
    in_specs += side_in
    args += [w for w, _ in side]
    out_shape = [jax.ShapeDtypeStruct((n_seq, L, D_MODEL), F32),
                 jax.ShapeDtypeStruct((n_seq, SSD_D_INNER, SSD_D_STATE), F32),
                 jax.ShapeDtypeStruct((n_seq, 3, SSD_CONV_DIM), F32)] + side_shapes
    out_specs = [pl.BlockSpec((NB, TL, D_MODEL), lambda b, i: (b, i, 0)),
                 pl.BlockSpec((NB, SSD_D_INNER, SSD_D_STATE), lambda b, i: (b, 0, 0)),
                 pl.BlockSpec((NB, 3, SSD_CONV_DIM), lambda b, i: (b, 0, 0))] + side_out
    scratch = [pltpu.VMEM((NB, SSD_N_GROUPS, SSD_D_STATE, SSD_NORM_GROUP), F32),
               pltpu.VMEM((NB, SSD_N_GROUPS, SSD_D_STATE, SSD_NORM_GROUP), BF16),
               pltpu.VMEM((NB, SSD_CONV_DIM // CONV_COLS, SUBLANES, CONV_COLS), F32),
               pltpu.VMEM((R, SSD_D_INNER), F32),
               pltpu.VMEM((R, SSD_CONV_DIM), F32),
               pltpu.VMEM((R, SSD_D_INNER), F32),
               pltpu.VMEM((LANES, SSD_D_INNER), BF16),
               pltpu.VMEM((R, R), BF16),
               pltpu.VMEM((R, SSD_D_INNER), F32),
               pltpu.VMEM((R, SSD_N_PAIRS * 2 * C), BF16 if 2 * C == LANES else F32),
               pltpu.VMEM((R, SSD_D_INNER), BF16),
               pltpu.VMEM((R, SSD_D_INNER), BF16),
               pltpu.VMEM((R, SSD_D_INNER), BF16),
               pltpu.VMEM((R, 2 * SSD_N_GROUPS * SSD_D_STATE), BF16)]
    if 2 * C != LANES:
        scratch += [pltpu.VMEM((LANES, SSD_N_PAIRS * 2 * C), BF16),
                    pltpu.VMEM((R, SSD_N_PAIRS * 2 * C), F32)]
    outs = pl.pallas_call(
        functools.partial(_ssd_kernel, has_init=has_init, NB=NB, TL=TL, C=C, n_side=len(side)),
        grid=grid, in_specs=in_specs, out_specs=out_specs, out_shape=out_shape,
        scratch_shapes=scratch,
        compiler_params=pltpu.CompilerParams(
            dimension_semantics=("arbitrary", "arbitrary"), vmem_limit_bytes=SSD_VMEM_LIMIT),
        name="ssd_mixer_init" if has_init else "ssd_mixer_zero",
    )(*args)
    return outs[0], outs[1], outs[2], list(outs[3:])


def _hgrn_kernel(*refs, has_init, NB, TL, C):
    it = iter(refs)
    x_ref = next(it)
    s0_ref = next(it) if has_init else None
    nw_ref, win_ref, lb_ref, gnw_ref, wo_ref = [next(it) for _ in range(5)]
    xo_ref = next(it)
    so_ref = next(it)
    (st_scr, tri_scr, qe_scr, qm_scr, ke_scr, kw_scr, v_scr, g_scr, elT_scr, sc_scr,
     o_scr) = [next(it) for _ in range(11)]
    assert C <= HGRN_HEAD_DIM
    TB = tri_scr.shape[0]

    R = NB * TL
    n_ch = R // C
    ch_per_seq = TL // C
    HD = HGRN_HEAD_DIM
    i = pl.program_id(1)
    n_i = pl.num_programs(1)

    @pl.when(i == 0)
    def _init():
        tri_scr[...] = _chunk_triangle(TB, C)
        if has_init:
            st_scr[...] = s0_ref[...]
        else:
            st_scr[...] = jnp.zeros_like(st_scr)

    x = x_ref[...].reshape(R, D_MODEL)
    h16 = _rms(x, nw_ref[...]).astype(BF16)
    lb_soft = jax.nn.softmax(lb_ref[...], axis=0)
    lb = (lb_soft[0:1, :] + lb_soft[1:2, :]) - lb_soft[0:1, :]

    for c0 in range(0, D_MODEL, HGRN_COLS):
        cols = slice(c0, c0 + HGRN_COLS)
        q = _silu(_dot(h16, win_ref[:, c0:c0 + HGRN_COLS]))
        f = _dot(h16, win_ref[:, D_MODEL + c0:D_MODEL + c0 + HGRN_COLS])
        v = _dot(h16, win_ref[:, 2 * D_MODEL + c0:2 * D_MODEL + c0 + HGRN_COLS])
        g = _dot(h16, win_ref[:, 3 * D_MODEL + c0:3 * D_MODEL + c0 + HGRN_COLS])
        lbc = lb[:, cols]
        forget = lbc + (1.0 - lbc) * jax.nn.sigmoid(f)
        k = 1.0 - forget
        logf = _split3(jnp.log(forget))
        b = jnp.concatenate([sum(_dot(tri_scr[...], p[r0:r0 + TB]) for p in logf) for r0 in range(0, R, TB)], axis=0)
        ends = [b[(c + 1) * C - 1:(c + 1) * C, :] for c in range(n_ch)]
        last = jnp.concatenate([jnp.broadcast_to(e, (C, HGRN_COLS)) for e in ends], axis=0)
        mid = jnp.concatenate([jnp.broadcast_to(b[c * C + C // 2:c * C + C // 2 + 1, :], (C, HGRN_COLS))
                               for c in range(n_ch)], axis=0)
        qe_scr[:, cols] = (q * jnp.exp(b)).astype(BF16)
        qm_scr[:, cols] = (q * jnp.exp(b - mid)).astype(BF16)
        ke_scr[:, cols] = (k * jnp.exp(mid - b)).astype(BF16)
        kw_scr[:, cols] = (k * jnp.exp(last - b)).astype(BF16)
        v_scr[:, cols] = v.astype(BF16)
        g_scr[:, cols] = _silu(g)
        el = jnp.exp(jnp.concatenate(ends + [jnp.zeros((LANES - n_ch, HGRN_COLS), F32)], axis=0))
        elT_scr[cols, :] = el.T

    rr = lax.broadcasted_iota(jnp.int32, (C, C), 0)
    cc = lax.broadcasted_iota(jnp.int32, (C, C), 1)
    tril = cc <= rr

    for c in range(n_ch):
        rows = slice(c * C, (c + 1) * C)
        for hh in range(HGRN_N_HEADS):
            sl = slice(hh * HD, (hh + 1) * HD)
            sc = jnp.where(tril, _dot_nt(qm_scr[rows, sl], ke_scr[rows, sl]), 0.0)
            sc_scr[rows, hh * HD:hh * HD + C] = sc.astype(BF16)

    for c in range(n_ch):
        nb = c // ch_per_seq
        rows = slice(c * C, (c + 1) * C)
        for hh in range(HGRN_N_HEADS):
            sl = slice(hh * HD, (hh + 1) * HD)
            v = v_scr[rows, sl]
            s_h = st_scr[nb, hh]
            o_scr[rows, sl] = _dot(jnp.concatenate([qe_scr[rows, sl], sc_scr[rows, hh * HD:hh * HD + C]], axis=1),
                                   jnp.concatenate([s_h.astype(BF16), v], axis=0))
            st_scr[nb, hh] = elT_scr[sl, c:c + 1] * s_h + _dot_tn(kw_scr[rows, sl], v)

    out = None
    for c0 in range(0, D_MODEL, HGRN_COLS):
        parts = []
        for h0 in range(c0, c0 + HGRN_COLS, HD):
            blk = o_scr[:, h0:h0 + HD]
            parts.append(blk * lax.rsqrt(jnp.mean(blk * blk, axis=-1, keepdims=True) + NORM_EPS))
        cols = slice(c0, c0 + HGRN_COLS)
        on = jnp.concatenate(parts, axis=1) * gnw_ref[:, cols] * g_scr[:, cols]
        part = _dot(on.astype(BF16), wo_ref[cols, :])
        out = part if out is None else out + part
    xo_ref[...] = (x + out).reshape(NB, TL, D_MODEL)

    @pl.when(i == n_i - 1)
    def _fin():
        so_ref[...] = st_scr[...]


def _hgrn_mixer(x, s0, p, *, NB, TL, C):
    n_seq, L, _ = x.shape
    has_init = s0 is not None
    R = NB * TL
    grid = (n_seq // NB, L // TL)
    st_block = (NB, HGRN_N_HEADS, HGRN_HEAD_DIM, HGRN_HEAD_DIM)
    in_specs = [pl.BlockSpec((NB, TL, D_MODEL), lambda b, i: (b, i, 0))]
    args = [x]
    if has_init:
        in_specs.append(pl.BlockSpec(st_block, lambda b, i: (b, 0, 0, 0)))
        args.append(s0)
    consts = [p["nw"], p["win"], p["lb"], p["gnw"], p["wo"]]
    for c in consts:
        in_specs.append(_layer_spec(c, 0) if c.ndim == 3 else _const_spec(c.shape))
    args += consts
    out_shape = (jax.ShapeDtypeStruct((n_seq, L, D_MODEL), F32),
                 jax.ShapeDtypeStruct((n_seq,) + st_block[1:], F32))
    out_specs = (pl.BlockSpec((NB, TL, D_MODEL), lambda b, i: (b, i, 0)),
                 pl.BlockSpec(st_block, lambda b, i: (b, 0, 0, 0)))
    scratch = [pltpu.VMEM(st_block, F32),
               pltpu.VMEM((min(R, MXU_DEPTH),) * 2, BF16),
               pltpu.VMEM((R, D_MODEL), BF16),
               pltpu.VMEM((R, D_MODEL), BF16),
               pltpu.VMEM((R, D_MODEL), BF16),
               pltpu.VMEM((R, D_MODEL), BF16),
               pltpu.VMEM((R, D_MODEL), BF16),
               pltpu.VMEM((R, D_MODEL), F32),
               pltpu.VMEM((D_MODEL, LANES), F32),
               pltpu.VMEM((R, D_MODEL), BF16),
               pltpu.VMEM((R, D_MODEL), F32)]
    return pl.pallas_call(
        functools.partial(_hgrn_kernel, has_init=has_init, NB=NB, TL=TL, C=C),
        grid=grid, in_specs=in_specs, out_specs=out_specs, out_shape=out_shape,
        scratch_shapes=scratch,
        compiler_params=pltpu.CompilerParams(
            dimension_semantics=("arbitrary", "arbitrary"), vmem_limit_bytes=VMEM_LIMIT),
        name="hgrn_mixer_init" if has_init else "hgrn_mixer_zero",
    )(*args)


def _ffn_kernel(*refs, final, n_side):
    it = iter(refs)
    x_ref, nw_ref, wg_ref, wu_ref, wd_ref = [next(it) for _ in range(5)]
    fw_ref = next(it) if final else None
    side_in = [next(it) for _ in range(n_side)]
    o_ref = next(it)
    for w_ref in side_in:
        next(it)[...] = w_ref[...].astype(BF16)
    x = x_ref[...]
    h16 = _rms(x, nw_ref[...]).astype(BF16)
    act = _silu(_dot(h16, wg_ref[...])) * _dot(h16, wu_ref[...])
    y = x + _dot(act.astype(BF16), wd_ref[...])
    if final:
        y = _rms(y, fw_ref[...])
    o_ref[...] = y


def _ffn(x2d, nw, wg, wu, wd, final_w, side, *, TM):
    rows = x2d.shape[0]
    final = final_w is not None
    n_steps = rows // TM
    side_in, side_out, side_shapes = _side_cast(side, n_steps, lambda i: i)
    consts = [nw, wg, wu, wd] + ([final_w] if final else [])
    in_specs = [pl.BlockSpec((TM, D_MODEL), lambda i: (i, 0))] + [_const_spec(c.shape) for c in consts] + side_in
    outs = pl.pallas_call(
        functools.partial(_ffn_kernel, final=final, n_side=len(side)),
        grid=(n_steps,),
        in_specs=in_specs,
        out_specs=[pl.BlockSpec((TM, D_MODEL), lambda i: (i, 0))] + side_out,
        out_shape=[jax.ShapeDtypeStruct((rows, D_MODEL), F32)] + side_shapes,
        compiler_params=pltpu.CompilerParams(
            dimension_semantics=("arbitrary",), vmem_limit_bytes=SSD_VMEM_LIMIT),
        name="swiglu_final" if final else "swiglu",
    )(x2d, *consts, *[w for w, _ in side])
    return outs[0], list(outs[1:])


def _tiles(n_seq, L):
    if L >= SSD_LONG_TILE:
        return dict(NB=1, TL_SSD=SSD_LONG_TILE, TL_HGRN=HGRN_LONG_TILE, C=SCAN_CHUNK, C_HGRN=HGRN_CHUNK,
                    TM=FFN_ROWS)
    return dict(NB=SHORT_SEQ_PER_STEP, TL_SSD=L, TL_HGRN=L, C=min(L, SCAN_CHUNK), C_HGRN=min(L, HGRN_CHUNK),
                TM=min(FFN_ROWS, n_seq * L))


def _trunk(x, s_ssd, cv, s_hgrn, ssd_p, hgrn_p, ffn_nw, final_w, w16, raw):
    n_seq, L, _ = x.shape
    t = _tiles(n_seq, L)
    s0 = None if s_ssd is None else s_ssd.reshape(n_seq, SSD_D_INNER, SSD_D_STATE)

    def missing(names):
        return [k for k in names if k not in w16]

    need = missing(("wg0", "wu0", "wd0"))
    x, s_new, cv_new, cast = _ssd_mixer(x, s0, cv, ssd_p, [raw[k] for k in need],
                                        NB=t["NB"], TL=t["TL_SSD"], C=t["C"])
    w16.update(zip(need, cast))
    need = missing(("hgrn_in", "hgrn_out", "wg1", "wu1", "wd1"))
    x, cast = _ffn(x.reshape(n_seq * L, D_MODEL), ffn_nw[0], w16["wg0"], w16["wu0"], w16["wd0"], None,
                   [raw[k] for k in need], TM=t["TM"])
    w16.update(zip(need, cast))
    x, h_new = _hgrn_mixer(x.reshape(n_seq, L, D_MODEL), s_hgrn,
                           dict(hgrn_p, win=w16["hgrn_in"], wo=w16["hgrn_out"]),
                           NB=t["NB"], TL=t["TL_HGRN"], C=t["C_HGRN"])
    y, _ = _ffn(x.reshape(n_seq * L, D_MODEL), ffn_nw[1], w16["wg1"], w16["wu1"], w16["wd1"], final_w,
                [], TM=t["TM"])
    s_new = s_new.reshape(1, n_seq, SSD_N_HEADS, SSD_HEAD_DIM, SSD_D_STATE)
    return y.reshape(n_seq, L, D_MODEL), s_new, cv_new[None], h_new[None]


def kernel(x_prompt, x_sample, state_ssd, cache_conv, state_hgrn, ssd_norm_w, ssd_in_w, ssd_conv_w, ssd_conv_b, ssd_dt_bias, ssd_A_log, ssd_D, ssd_gnorm_w, ssd_out_w, hgrn_norm_w, hgrn_in_w, hgrn_lower_bounds, hgrn_gnorm_w, hgrn_out_w, ffn_norm_w, ffn_w_gate, ffn_w_up, ffn_w_down, final_norm_w):
    pair_order = jnp.concatenate([jnp.arange(0, SSD_N_HEADS, 2), jnp.arange(1, SSD_N_HEADS, 2)])
    ssd_p = {
        "nw": ssd_norm_w[0].reshape(1, D_MODEL),
        "win": ssd_in_w.astype(BF16),
        "cw": ssd_conv_w,
        "cb": ssd_conv_b[0].reshape(1, SSD_CONV_DIM),
        "dtb": ssd_dt_bias[0].reshape(1, SSD_N_HEADS),
        "dtbT": ssd_dt_bias[0][pair_order].reshape(SSD_N_HEADS, 1),
        "alog": ssd_A_log[0],
        "dexp": jnp.repeat(ssd_D[0], SSD_HEAD_DIM).reshape(1, SSD_D_INNER),
        "gnw": ssd_gnorm_w[0].reshape(1, SSD_D_INNER),
        "wo": ssd_out_w.astype(BF16),
    }
    hgrn_p = {
        "nw": hgrn_norm_w[0].reshape(1, D_MODEL),
        "lb": hgrn_lower_bounds,
        "gnw": jnp.tile(hgrn_gnorm_w[0], HGRN_N_HEADS).reshape(1, D_MODEL),
    }
    ffn_nw = ffn_norm_w.reshape(2, 1, D_MODEL)
    final_w = final_norm_w.reshape(1, D_MODEL)
    raw = {"wg0": (ffn_w_gate, 0), "wu0": (ffn_w_up, 0), "wd0": (ffn_w_down, 0),
           "hgrn_in": (hgrn_in_w, 0), "hgrn_out": (hgrn_out_w, 0),
           "wg1": (ffn_w_gate, 1), "wu1": (ffn_w_up, 1), "wd1": (ffn_w_down, 1)}
    w16 = {}

    y_p, ssd_s_p, conv_p, hgrn_s_p = _trunk(
        x_prompt, None, None, None, ssd_p, hgrn_p, ffn_nw, final_w, w16, raw)
    y_s, ssd_s_s, conv_s, hgrn_s_s = _trunk(
        x_sample, state_ssd[0], cache_conv[0], state_hgrn[0], ssd_p, hgrn_p, ffn_nw, final_w, w16, raw)
    return (y_p, y_s, ssd_s_p, conv_p, hgrn_s_p, ssd_s_s, conv_s, hgrn_s_s)
```
